```python
import jax
import jax.numpy as jnp
from jax import lax
import numpy as np

D_MODEL = 1024
BATCH = 2
SEQ = 8192
DEPTH = 4
DEC_BATCH = 32
DEC_SEQ = 32
PAST_LEN = 2048

CHUNK = 64
N_A_LAYERS = DEPTH // 2
N_B_LAYERS = DEPTH - N_A_LAYERS
CONV_WIDTH = 31
N_HEADS = 16
N_KV_HEADS = 2
HEAD_DIM = 64
GROUP = N_HEADS // N_KV_HEADS
Q_WIDTH = N_HEADS * HEAD_DIM
KV_WIDTH = N_KV_HEADS * HEAD_DIM
WINDOW = 128
WIN_CHUNKS = WINDOW // CHUNK
ROT_DIM = HEAD_DIM // 4
ROPE_THETA = 500000.0
N_GROUPS = 4
EXPERTS_PER_GROUP = 8
TOP_K = 2
EXPERT_FF = D_MODEL // 2
DEEPNORM_ALPHA = (2 * DEPTH) ** 0.25
DEEPNORM_BETA = (8 * DEPTH) ** -0.25
LN_EPS = 1e-5
ATTN_SCALE = HEAD_DIM ** -0.5
NEG_INF = -1e30

kernel_name = 'yoco_conformer_swa_sink_hmoe_step'


def _layer_norm(x, g, b):
    xf = x.astype(jnp.float32)
    mu = jnp.mean(xf, axis=-1, keepdims=True)
    xc = xf - mu
    var = jnp.mean(xc * xc, axis=-1, keepdims=True)
    y = xc * lax.rsqrt(var + LN_EPS) * g.astype(jnp.float32) + b.astype(jnp.float32)
    return y.astype(x.dtype)


def _rope(x, pos):
    half = ROT_DIM // 2
    inv_freq = ROPE_THETA ** (-jnp.arange(0, ROT_DIM, 2, dtype=jnp.float32) / ROT_DIM)
    ang = pos.astype(jnp.float32)[:, None] * inv_freq[None, :]
    cos = jnp.cos(ang)[None, :, None, :]
    sin = jnp.sin(ang)[None, :, None, :]
    xr = x[..., :ROT_DIM].astype(jnp.float32)
    x1, x2 = xr[..., :half], xr[..., half:]
    rot = jnp.concatenate([x1 * cos - x2 * sin, x2 * cos + x1 * sin], axis=-1).astype(x.dtype)
    return jnp.concatenate([rot, x[..., ROT_DIM:]], axis=-1)


def _conv_module(x, prev, w_in, w_dw, ln_g, ln_b, w_out):
    h = jnp.einsum('btd,de->bte', x, w_in)
    a, gate = jnp.split(h, 2, axis=-1)
    u = a * jax.nn.sigmoid(gate)
    u_pad = jnp.concatenate([prev.astype(u.dtype), u], axis=1)
    dw = lax.conv_general_dilated(u_pad, w_dw[:, None, :].astype(u.dtype), (1,), 'VALID',
                                  dimension_numbers=('NWC', 'WIO', 'NWC'),
                                  feature_group_count=D_MODEL)
    s = jax.nn.silu(_layer_norm(dw, ln_g, ln_b))
    out = jnp.einsum('btd,de->bte', s, w_out)
    return out, u_pad[:, -(CONV_WIDTH - 1):]


def _sink_attention(q, k, v, mask, sinks):
    s = jnp.einsum('bnqkgd,bnskd->bnkgqs', q, k, preferred_element_type=jnp.float32) * ATTN_SCALE
    s = jnp.where(mask[None, :, None, None, None, :], s, NEG_INF)
    sink = sinks.astype(jnp.float32).reshape(1, 1, N_KV_HEADS, GROUP, 1, 1)
    m = jnp.maximum(jnp.max(s, axis=-1, keepdims=True), sink)
    p = jnp.exp(s - m)
    denom = jnp.sum(p, axis=-1, keepdims=True) + jnp.exp(sink - m)
    return jnp.einsum('bnkgqs,bnskd->bnqkgd', (p / denom).astype(v.dtype), v)


def _band_attention(q, k, v, sinks):
    b, t = q.shape[0], q.shape[1]
    nc = t // CHUNK
    qb = q.reshape(b, nc, CHUNK, N_KV_HEADS, GROUP, HEAD_DIM)
    pad = ((0, 0), (WINDOW, 0), (0, 0), (0, 0))
    kc = jnp.pad(k, pad).reshape(b, nc + WIN_CHUNKS, CHUNK, N_KV_HEADS, HEAD_DIM)
    vc = jnp.pad(v, pad).reshape(b, nc + WIN_CHUNKS, CHUNK, N_KV_HEADS, HEAD_DIM)
    kb = jnp.concatenate([kc[:, i:i + nc] for i in range(WIN_CHUNKS + 1)], axis=2)
    vb = jnp.concatenate([vc[:, i:i + nc] for i in range(WIN_CHUNKS + 1)], axis=2)
    key_chunk = (jnp.arange(nc)[:, None] + (jnp.arange((WIN_CHUNKS + 1) * CHUNK) // CHUNK)[None, :]
                 - WIN_CHUNKS)
    mask = key_chunk >= 0
    return _sink_attention(qb, kb, vb, mask, sinks).reshape(b, t, N_HEADS, HEAD_DIM)


def _window_step_attention(q, k_all, v_all, sinks):
    b, t = q.shape[0], q.shape[1]
    qb = q.reshape(b, 1, t, N_KV_HEADS, GROUP, HEAD_DIM)
    mask = jnp.ones((1, k_all.shape[1]), dtype=bool)
    return _sink_attention(qb, k_all[:, None], v_all[:, None], mask, sinks).reshape(b, t, N_HEADS, HEAD_DIM)


def _hier_moe(x, w_grp, b_grp, w_exp, b_exp, w_gate, w_up, w_down):
    shp = x.shape
    xt = x.reshape(-1, D_MODEL)
    g_logits = jnp.dot(xt, w_grp, preferred_element_type=jnp.float32) + b_grp.astype(jnp.float32)
    g_prob = jax.nn.softmax(g_logits, axis=-1)
    _, g_idx = lax.top_k(g_logits, 1)
    p_grp = jnp.take_along_axis(g_prob, g_idx, axis=-1)
    e_logits = (jnp.dot(xt, w_exp, preferred_element_type=jnp.float32).reshape(-1, N_GROUPS, EXPERTS_PER_GROUP)
                + b_exp.astype(jnp.float32))
    e_sel = jnp.take_along_axis(e_logits, g_idx[:, :, None], axis=1)[:, 0]
    e_top, e_idx = lax.top_k(e_sel, TOP_K)
    e_w = jax.nn.softmax(e_top, axis=-1) * p_grp
    within = jnp.sum(jax.nn.one_hot(e_idx, EXPERTS_PER_GROUP, dtype=jnp.float32) * e_w[..., None], axis=1)
    gates = (jax.nn.one_hot(g_idx[:, 0], N_GROUPS, dtype=jnp.float32)[:, :, None]
             * within[:, None, :]).astype(xt.dtype)
    y = jnp.zeros((xt.shape[0], D_MODEL), jnp.float32)
    for gi in range(N_GROUPS):
        hg = jnp.einsum('td,edf->tef', xt, w_gate[gi])
        hu = jnp.einsum('td,edf->tef', xt, w_up[gi])
        h = jax.nn.silu(hg) * hu * gates[:, gi, :, None]
        y = y + jnp.einsum('tef,efd->td', h, w_down[gi], preferred_element_type=jnp.float32)
    return y.astype(x.dtype).reshape(shp)


def _trunk(x, pos, conv_prev, k_prev, v_prev, weights):
    (ln_mix_g, ln_mix_b, ln_ffn_g, ln_ffn_b, conv_w_in, conv_w_dw, conv_ln_g, conv_ln_b, conv_w_out,
     w_kv, attn_w_q, attn_sinks, attn_w_o, router_w_group, router_b_group, router_w_expert,
     router_b_expert, expert_w_gate, expert_w_up, expert_w_down) = weights
    b, t = x.shape[0], x.shape[1]
    conv_new = []
    k_all = v_all = k_new = v_new = None
    for layer in range(DEPTH):
        if layer < N_A_LAYERS:
            mix, rows = _conv_module(x, conv_prev[layer], conv_w_in[layer], conv_w_dw[layer],
                                     conv_ln_g[layer], conv_ln_b[layer], conv_w_out[layer])
            conv_new.append(rows)
        else:
            j = layer - N_A_LAYERS
            q = _rope(jnp.einsum('btd,dh->bth', x, attn_w_q[j]).reshape(b, t, N_HEADS, HEAD_DIM), pos)
            if k_prev is None:
                o = _band_attention(q, k_all, v_all, attn_sinks[j])
            else:
                o = _window_step_attention(q, k_all, v_all, attn_sinks[j])
            mix = jnp.einsum('bth,hd->btd', o.reshape(b, t, Q_WIDTH), attn_w_o[j])
        x = _layer_norm(DEEPNORM_ALPHA * x + mix, ln_mix_g[layer], ln_mix_b[layer])
        f = _hier_moe(x, router_w_group[layer], router_b_group[layer], router_w_expert[layer],
                      router_b_expert[layer], expert_w_gate[layer], expert_w_up[layer], expert_w_down[layer])
        x = _layer_norm(DEEPNORM_ALPHA * x + f, ln_ffn_g[layer], ln_ffn_b[layer])
        if layer == N_A_LAYERS - 1:
            kv = jnp.einsum('btd,dh->bth', x, w_kv)
            k = _rope(kv[..., :KV_WIDTH].reshape(b, t, N_KV_HEADS, HEAD_DIM), pos)
            v = kv[..., KV_WIDTH:].reshape(b, t, N_KV_HEADS, HEAD_DIM)
            if k_prev is None:
                k_all, v_all = k, v
            else:
                k_all = jnp.concatenate([k_prev.astype(k.dtype), k], axis=1)
                v_all = jnp.concatenate([v_prev.astype(v.dtype), v], axis=1)
            k_new = k_all[:, -WINDOW:]
            v_new = v_all[:, -WINDOW:]
    return x, jnp.stack(conv_new, axis=0), k_new, v_new


def setup_inputs(seed: int = 0) -> dict:
    key = jax.random.key(seed)
    ks = jax.random.split(key, 28)

    def nrm(k, shape, scale):
        return jax.random.normal(k, shape, jnp.float32) * scale

    d = D_MODEL
    e = EXPERTS_PER_GROUP
    w_k = nrm(ks[13], (d, KV_WIDTH), d ** -0.5)
    w_v = nrm(ks[14], (d, KV_WIDTH), DEEPNORM_BETA * d ** -0.5)
    return {
        'x_prompt': nrm(ks[0], (BATCH, SEQ, d), 1.0),
        'x_sample': nrm(ks[1], (DEC_BATCH, DEC_SEQ, d), 1.0),
        'state_conv': nrm(ks[2], (N_A_LAYERS, DEC_BATCH, CONV_WIDTH - 1, d), 1.0),
        'cache_k': nrm(ks[3], (DEC_BATCH, WINDOW, N_KV_HEADS, HEAD_DIM), 1.0),
        'cache_v': nrm(ks[4], (DEC_BATCH, WINDOW, N_KV_HEADS, HEAD_DIM), DEEPNORM_BETA),
        'ln_mix_g': 1.0 + nrm(ks[5], (DEPTH, d), 0.02),
        'ln_mix_b': nrm(ks[6], (DEPTH, d), 0.02),
        'ln_ffn_g': 1.0 + nrm(ks[7], (DEPTH, d), 0.02),
        'ln_ffn_b': nrm(ks[8], (DEPTH, d), 0.02),
        'conv_w_in': nrm(ks[9], (N_A_LAYERS, d, 2 * d), d ** -0.5),
        'conv_w_dw': nrm(ks[10], (N_A_LAYERS, CONV_WIDTH, d), CONV_WIDTH ** -0.5),
        'conv_ln_g': 1.0 + nrm(ks[11], (N_A_LAYERS, d), 0.02),
        'conv_ln_b': nrm(ks[12], (N_A_LAYERS, d), 0.02),
        'conv_w_out': nrm(ks[15], (N_A_LAYERS, d, d), DEEPNORM_BETA * d ** -0.5),
        'w_kv': jnp.concatenate([w_k, w_v], axis=1),
        'attn_w_q': nrm(ks[16], (N_B_LAYERS, d, Q_WIDTH), d ** -0.5),
        'attn_sinks': nrm(ks[17], (N_B_LAYERS, N_HEADS), 0.5),
        'attn_w_o': nrm(ks[18], (N_B_LAYERS, Q_WIDTH, d), DEEPNORM_BETA * Q_WIDTH ** -0.5),
        'router_w_group': nrm(ks[19], (DEPTH, d, N_GROUPS), d ** -0.5),
        'router_b_group': nrm(ks[20], (DEPTH, N_GROUPS), 0.01),
        'router_w_expert': nrm(ks[21], (DEPTH, d, N_GROUPS * e), d ** -0.5),
        'router_b_expert': nrm(ks[22], (DEPTH, N_GROUPS, e), 0.01),
        'expert_w_gate': nrm(ks[23], (DEPTH, N_GROUPS, e, d, EXPERT_FF), d ** -0.5),
        'expert_w_up': nrm(ks[24], (DEPTH, N_GROUPS, e, d, EXPERT_FF), d ** -0.5),
        'expert_w_down': nrm(ks[25], (DEPTH, N_GROUPS, e, EXPERT_FF, d), DEEPNORM_BETA * EXPERT_FF ** -0.5),
    }


def reference(x_prompt, x_sample, state_conv, cache_k, cache_v, ln_mix_g, ln_mix_b, ln_ffn_g, ln_ffn_b,
              conv_w_in, conv_w_dw, conv_ln_g, conv_ln_b, conv_w_out, w_kv, attn_w_q, attn_sinks, attn_w_o,
              router_w_group, router_b_group, router_w_expert, router_b_expert,
              expert_w_gate, expert_w_up, expert_w_down):
    weights = (ln_mix_g, ln_mix_b, ln_ffn_g, ln_ffn_b, conv_w_in, conv_w_dw, conv_ln_g, conv_ln_b, conv_w_out,
               w_kv, attn_w_q, attn_sinks, attn_w_o, router_w_group, router_b_group, router_w_expert,
               router_b_expert, expert_w_gate, expert_w_up, expert_w_down)
    pos_prompt = jnp.arange(x_prompt.shape[1], dtype=jnp.int32)
    pos_sample = PAST_LEN + jnp.arange(x_sample.shape[1], dtype=jnp.int32)
    conv_zero = jnp.zeros((N_A_LAYERS, x_prompt.shape[0], CONV_WIDTH - 1, D_MODEL), x_prompt.dtype)
    y_prompt, new_state_conv_prompt, new_cache_k_prompt, new_cache_v_prompt = _trunk(
        x_prompt, pos_prompt, conv_zero, None, None, weights)
    y_sample, new_state_conv_sample, new_cache_k_sample, new_cache_v_sample = _trunk(
        x_sample, pos_sample, state_conv, cache_k, cache_v, weights)
    return (y_prompt, y_sample, new_state_conv_prompt, new_state_conv_sample,
            new_cache_k_prompt, new_cache_v_prompt, new_cache_k_sample, new_cache_v_sample)
```

```python
import functools

import jax
import jax.numpy as jnp
from jax import lax
from jax.experimental import pallas as pl
from jax.experimental.pallas import tpu as pltpu

D_MODEL = 1024
BATCH = 2
SEQ = 8192
DEPTH = 4
DEC_BATCH = 32
DEC_SEQ = 32
PAST_LEN = 2048
CHUNK = 64
N_A_LAYERS = DEPTH // 2
CONV_WIDTH = 31
N_HEADS = 16
N_KV_HEADS = 2
HEAD_DIM = 64
KV_WIDTH = N_KV_HEADS * HEAD_DIM
WINDOW = 128
ROT_DIM = HEAD_DIM // 4
ROPE_THETA = 500000.0
N_GROUPS = 4
EXPERTS_PER_GROUP = 8
N_EXPERTS = N_GROUPS * EXPERTS_PER_GROUP
EXPERT_FF = D_MODEL // 2
DEEPNORM_ALPHA = (2 * DEPTH) ** 0.25
LN_EPS = 1e-5
ATTN_SCALE = HEAD_DIM ** -0.5
NEG_INF = -1e30

LANES = 128
N_LANE_CHUNKS = D_MODEL // LANES
T_PROMPT = BATCH * SEQ
T_SAMPLE = DEC_BATCH * DEC_SEQ
T_ALL = T_PROMPT + T_SAMPLE
TM = 256
STATE_PAD = 32
KEY_WIN = 256
UQ_PROMPT = 2 * CHUNK
MOE_TM = 256
MOE_ROWS = 2 * T_ALL + N_EXPERTS * MOE_TM
MOE_TILES = MOE_ROWS // MOE_TM
VMEM_LIMIT = 56 * 1024 * 1024

F32 = jnp.float32
BF16 = jnp.bfloat16


def _layer_norm(x, g, b):
    mu = jnp.mean(x, axis=-1, keepdims=True)
    xc = x - mu
    var = jnp.mean(xc * xc, axis=-1, keepdims=True)
    return xc * lax.rsqrt(var + LN_EPS) * g + b


def _router(x1, rwh_ref, rwl_ref, rb_ref):
    x_hi = x1.astype(BF16)
    x_lo = (x1 - x_hi.astype(F32)).astype(BF16)
    w_hi = rwh_ref[...]
    logits = (jnp.dot(x_hi, w_hi, preferred_element_type=F32)
              + jnp.dot(x_hi, rwl_ref[...], preferred_element_type=F32)
              + jnp.dot(x_lo, w_hi, preferred_element_type=F32)) + rb_ref[...]
    lane_i = lax.broadcasted_iota(jnp.int32, logits.shape, 1)
    lane = lane_i.astype(F32)
    neg = jnp.float32(-3.0e38)
    none = jnp.float32(LANES)
    gl = jnp.where(lane_i < N_GROUPS, logits, neg)
    gmax = jnp.max(gl, axis=-1, keepdims=True)
    g_idx = jnp.min(jnp.where(gl == gmax, lane, none), axis=-1, keepdims=True)
    gsum = jnp.sum(jnp.where(lane_i < N_GROUPS, jnp.exp(gl - gmax), 0.0), axis=-1, keepdims=True)
    p_grp = 1.0 / gsum
    lo = N_GROUPS + EXPERTS_PER_GROUP * g_idx
    el = jnp.where(lane >= lo, jnp.where(lane < lo + EXPERTS_PER_GROUP, logits, neg), neg)
    l1 = jnp.max(el, axis=-1, keepdims=True)
    i1 = jnp.min(jnp.where(el == l1, lane, none), axis=-1, keepdims=True)
    el2 = jnp.where(lane == i1, neg, el)
    l2 = jnp.max(el2, axis=-1, keepdims=True)
    i2 = jnp.min(jnp.where(el2 == l2, lane, none), axis=-1, keepdims=True)
    t = jnp.exp(l2 - l1)
    inv = p_grp / (1.0 + t)
    e1 = i1 - N_GROUPS
    e2 = i2 - N_GROUPS
    return jnp.where(lane_i == 0, e1, jnp.where(lane_i == 1, e2, jnp.where(lane_i == 2, inv,
                     jnp.where(lane_i == 3, inv * t, 0.0))))


def _post_mixer(x, mix, lg_ref, lb_ref, rwh_ref, rwl_ref, rb_ref, x1_ref, info_ref):
    x1 = _layer_norm(DEEPNORM_ALPHA * x + mix, lg_ref[...], lb_ref[...])
    x1_ref[...] = x1
    info_ref[...] = _router(x1, rwh_ref, rwl_ref, rb_ref)


def _conv_mixer_body(x_ref, st_ref, win_ref, wdw_ref, cg_ref, cb_ref, wout_ref, lg_ref, lb_ref,
                     rwh_ref, rwl_ref, rb_ref, x1_ref, info_ref, u_ref, ubuf, cbuf, *, nseg, seg):
    carry = st_ref is None
    x = x_ref[...]
    h = jnp.dot(x.astype(BF16), win_ref[...], preferred_element_type=F32)
    u = h[:, :D_MODEL] * jax.nn.sigmoid(h[:, D_MODEL:])
    if carry:
        @pl.when(pl.program_id(1) == 0)
        def _():
            ubuf[:, 0, 0:STATE_PAD, :] = jnp.zeros((N_LANE_CHUNKS, STATE_PAD, LANES), F32)
        u_ref[0] = u[seg - STATE_PAD:, :]
    else:
        u_ref[...] = u
    for lc in range(N_LANE_CHUNKS):
        for s in range(nseg):
            ubuf[lc, s, STATE_PAD:STATE_PAD + seg, :] = u[s * seg:(s + 1) * seg, lc * LANES:(lc + 1) * LANES]
            if not carry:
                ubuf[lc, s, 0:STATE_PAD, :] = st_ref[s, :, lc * LANES:(lc + 1) * LANES]

    def conv_lane_chunk(lc, c):
        w = wdw_ref[lc]
        for s in range(nseg):
            acc = jnp.zeros((seg, LANES), F32)
            for k in range(CONV_WIDTH):
                off = k + STATE_PAD - (CONV_WIDTH - 1)
                acc = acc + w[k:k + 1, :] * ubuf[lc, s, off:off + seg, :]
            cbuf[lc, s * seg:(s + 1) * seg, :] = acc
        return c

    lax.fori_loop(0, N_LANE_CHUNKS, conv_lane_chunk, 0)
    if carry:
        for lc in range(N_LANE_CHUNKS):
            ubuf[lc, 0, 0:STATE_PAD, :] = u[seg - STATE_PAD:, lc * LANES:(lc + 1) * LANES]
    dw = jnp.concatenate([cbuf[lc] for lc in range(N_LANE_CHUNKS)], axis=1)
    sw = _layer_norm(dw, cg_ref[...], cb_ref[...])
    sw = sw * jax.nn.sigmoid(sw)
    mix = jnp.dot(sw.astype(BF16), wout_ref[...], preferred_element_type=F32)
    _post_mixer(x, mix, lg_ref, lb_ref, rwh_ref, rwl_ref, rb_ref, x1_ref, info_ref)


def _conv_prompt_kernel(x_ref, *rest):
    _conv_mixer_body(x_ref, None, *rest, nseg=1, seg=TM)


def _conv_sample_kernel(x_ref, st_ref, *rest):
    weights, (x1_any, info_any), outs = rest[:10], rest[10:12], rest[12:]
    del x1_any, info_any
    _conv_mixer_body(x_ref, st_ref, *weights, *outs, nseg=TM // DEC_SEQ, seg=DEC_SEQ)


def _full(shape):
    return pl.BlockSpec(shape, lambda *_: (0,) * len(shape))


def _conv_mixer(x, state_pad, w):
    win, wdw, cg, cb, wout, lg, lb, rwh, rwl, rb = w
    w_specs = [_full(win.shape), _full(wdw.shape), _full(cg.shape), _full(cb.shape), _full(wout.shape),
               _full(lg.shape), _full(lb.shape), _full(rwh.shape), _full(rwl.shape), _full(rb.shape)]
    tiles_per_seq = SEQ // TM
    x1, info, tail = pl.pallas_call(
        _conv_prompt_kernel,
        grid=(BATCH, tiles_per_seq),
        in_specs=[pl.BlockSpec((TM, D_MODEL), lambda b, t: (b * tiles_per_seq + t, 0))] + w_specs,
        out_specs=[pl.BlockSpec((TM, D_MODEL), lambda b, t: (b * tiles_per_seq + t, 0)),
                   pl.BlockSpec((TM, LANES), lambda b, t: (b * tiles_per_seq + t, 0)),
                   pl.BlockSpec((1, STATE_PAD, D_MODEL), lambda b, t: (b, 0, 0))],
        out_shape=[jax.ShapeDtypeStruct((T_ALL, D_MODEL), F32),
                   jax.ShapeDtypeStruct((T_ALL, LANES), F32),
                   jax.ShapeDtypeStruct((BATCH, STATE_PAD, D_MODEL), F32)],
        scratch_shapes=[pltpu.VMEM((N_LANE_CHUNKS, 1, STATE_PAD + TM, LANES), F32),
                        pltpu.VMEM((N_LANE_CHUNKS, TM, LANES), F32)],
        compiler_params=pltpu.CompilerParams(dimension_semantics=("arbitrary", "arbitrary"),
                                             vmem_limit_bytes=VMEM_LIMIT),
        name="conv_mixer_prompt",
    )(x, *w)
    nb = TM // DEC_SEQ
    first = T_PROMPT // TM
    x1, info, u_s = pl.pallas_call(
        _conv_sample_kernel,
        grid=(T_SAMPLE // TM,),
        in_specs=[pl.BlockSpec((TM, D_MODEL), lambda i: (first + i, 0)),
                  pl.BlockSpec((nb, STATE_PAD, D_MODEL), lambda i: (i, 0, 0))] + w_specs
                 + [pl.BlockSpec(memory_space=pl.ANY), pl.BlockSpec(memory_space=pl.ANY)],
        out_specs=[pl.BlockSpec((TM, D_MODEL), lambda i: (first + i, 0)),
                   pl.BlockSpec((TM, LANES), lambda i: (first + i, 0)),
                   pl.BlockSpec((TM, D_MODEL), lambda i: (i, 0))],
        out_shape=[jax.ShapeDtypeStruct((T_ALL, D_MODEL), F32),
                   jax.ShapeDtypeStruct((T_ALL, LANES), F32),
                   jax.ShapeDtypeStruct((T_SAMPLE, D_MODEL), F32)],
        scratch_shapes=[pltpu.VMEM((N_LANE_CHUNKS, nb, STATE_PAD + DEC_SEQ, LANES), F32),
                        pltpu.VMEM((N_LANE_CHUNKS, TM, LANES), F32)],
        input_output_aliases={12: 0, 13: 1},
        compiler_params=pltpu.CompilerParams(dimension_semantics=("arbitrary",),
                                             vmem_limit_bytes=VMEM_LIMIT),
        name="conv_mixer_sample",
    )(x, state_pad, *w, x1, info)
    return x1, info, tail, u_s


def _rope_lanes(v, c, sa, sb):
    width = v.shape[-1]
    half = ROT_DIM // 2
    return v * c + pltpu.roll(v, width - half, 1) * sa + pltpu.roll(v, half, 1) * sb


def _attn_mixer_body(sink_ref, x_ref, wq_ref, c_ref, sa_ref, sb_ref, kd_ref, vd_ref, wo_ref, lg_ref, lb_ref,
                     rwh_ref, rwl_ref, rb_ref, x1_ref, info_ref, obuf, *, nunit, uq, prompt):
    x = x_ref[...]
    q = jnp.dot(x.astype(BF16), wq_ref[...], preferred_element_type=F32)
    rep = D_MODEL // LANES
    q = _rope_lanes(q, jnp.tile(c_ref[...], (1, rep)), jnp.tile(sa_ref[...], (1, rep)),
                    jnp.tile(sb_ref[...], (1, rep)))
    qb = (q * ATTN_SCALE).astype(BF16)
    pairs = N_HEADS // N_KV_HEADS // 2
    rows = pairs * uq
    lane_k = lax.broadcasted_iota(jnp.int32, (KEY_WIN, LANES), 1)
    col = lax.broadcasted_iota(jnp.int32, (rows, 2 * KEY_WIN), 1) & (KEY_WIN - 1)
    if prompt:
        chunk_shift = CHUNK.bit_length() - 1
        qchunk = (lax.broadcasted_iota(jnp.int32, (rows, 2 * KEY_WIN), 0) & (uq - 1)) >> chunk_shift
        kchunk = col >> chunk_shift
        band = (kchunk >= qchunk) & (kchunk <= qchunk + WINDOW // CHUNK)
    lane_o = lax.broadcasted_iota(jnp.int32, (uq, LANES), 1)
    for un in range(nunit):
        r0 = un * uq
        if prompt:
            kstart = pl.multiple_of(pl.program_id(1) * (nunit * uq) + r0, LANES)
            mask = band & (col + kstart >= WINDOW)
        else:
            mask = col < WINDOW + DEC_SEQ
        for kvh in range(N_KV_HEADS):
            if prompt:
                kw = kd_ref[0, kvh, pl.ds(kstart, KEY_WIN), :]
                vw = vd_ref[0, kvh, pl.ds(kstart, KEY_WIN), :]
            else:
                kw = kd_ref[un, kvh]
                vw = vd_ref[un, kvh]
            zero = jnp.zeros_like(kw)
            kbd = jnp.concatenate([jnp.where(lane_k < HEAD_DIM, kw, zero),
                                   jnp.where(lane_k >= HEAD_DIM, kw, zero)], axis=0)
            vbd = jnp.concatenate([jnp.where(lane_k < HEAD_DIM, vw, zero),
                                   jnp.where(lane_k >= HEAD_DIM, vw, zero)], axis=0)
            q4 = jnp.concatenate([qb[r0:r0 + uq, (kvh * pairs + p) * LANES:(kvh * pairs + p + 1) * LANES]
                                  for p in range(pairs)], axis=0)
            s = lax.dot_general(q4, kbd, (((1,), (1,)), ((), ())), preferred_element_type=F32)
            s = jnp.where(mask, s, NEG_INF)
            p_parts, inv_parts = [], []
            for p in range(pairs):
                halves, invs = [], []
                for hf in range(2):
                    sk = sink_ref[(kvh * pairs + p) * 2 + hf]
                    sh = s[p * uq:(p + 1) * uq, hf * KEY_WIN:(hf + 1) * KEY_WIN]
                    m = jnp.maximum(jnp.max(sh, axis=-1, keepdims=True), sk)
                    pe = jnp.exp(sh - m)
                    den = jnp.sum(pe, axis=-1, keepdims=True) + jnp.exp(sk - m)
                    halves.append(pe.astype(BF16))
                    invs.append(1.0 / den)
                p_parts.append(jnp.concatenate(halves, axis=1))
                inv_parts.append(invs)
            pm = jnp.concatenate(p_parts, axis=0)
            o4 = jnp.dot(pm, vbd, preferred_element_type=F32)
            for p in range(pairs):
                inv = jnp.where(lane_o < HEAD_DIM, inv_parts[p][0], inv_parts[p][1])
                hp = kvh * pairs + p
                obuf[r0:r0 + uq, hp * LANES:(hp + 1) * LANES] = (o4[p * uq:(p + 1) * uq, :] * inv).astype(BF16)
    mix = jnp.dot(obuf[...], wo_ref[...], preferred_element_type=F32)
    _post_mixer(x, mix, lg_ref, lb_ref, rwh_ref, rwl_ref, rb_ref, x1_ref, info_ref)


def _attn_prompt_kernel(*refs):
    _attn_mixer_body(*refs, nunit=TM // UQ_PROMPT, uq=UQ_PROMPT, prompt=True)


def _attn_sample_kernel(*refs):
    ins, (x1_any, info_any), outs = refs[:14], refs[14:16], refs[16:]
    del x1_any, info_any
    _attn_mixer_body(*ins, *outs, nunit=TM // DEC_SEQ, uq=DEC_SEQ, prompt=False)


def _attn_mixer(x, sinks, w, rope_p, rope_s, kv_p, kv_s):
    wq, wo, lg, lb, rwh, rwl, rb = w
    tiles_per_seq = SEQ // TM
    smem = pl.BlockSpec(memory_space=pltpu.SMEM)
    tail_specs = [_full(wo.shape), _full(lg.shape), _full(lb.shape), _full(rwh.shape), _full(rwl.shape),
                  _full(rb.shape)]
    kd, vd = kv_p
    tok = lambda b, t: (b * tiles_per_seq + t, 0)
    x1, info = pl.pallas_call(
        _attn_prompt_kernel,
        grid=(BATCH, tiles_per_seq),
        in_specs=[smem, pl.BlockSpec((TM, D_MODEL), tok), _full(wq.shape)]
                 + [pl.BlockSpec((TM, LANES), lambda b, t: (t, 0))] * 3
                 + [pl.BlockSpec((1,) + kd.shape[1:], lambda b, t: (b, 0, 0, 0))] * 2 + tail_specs,
        out_specs=[pl.BlockSpec((TM, D_MODEL), tok), pl.BlockSpec((TM, LANES), tok)],
        out_shape=[jax.ShapeDtypeStruct((T_ALL, D_MODEL), F32), jax.ShapeDtypeStruct((T_ALL, LANES), F32)],
        scratch_shapes=[pltpu.VMEM((TM, D_MODEL), BF16)],
        compiler_params=pltpu.CompilerParams(dimension_semantics=("arbitrary", "arbitrary"),
                                             vmem_limit_bytes=VMEM_LIMIT),
        name="attn_mixer_prompt",
    )(sinks, x, wq, *rope_p, kd, vd, wo, lg, lb, rwh, rwl, rb)
    nb = TM // DEC_SEQ
    first = T_PROMPT // TM
    kd, vd = kv_s
    x1, info = pl.pallas_call(
        _attn_sample_kernel,
        grid=(T_SAMPLE // TM,),
        in_specs=[smem, pl.BlockSpec((TM, D_MODEL), lambda i: (first + i, 0)), _full(wq.shape)]
                 + [_full((TM, LANES))] * 3
                 + [pl.BlockSpec((nb,) + kd.shape[1:], lambda i: (i, 0, 0, 0))] * 2 + tail_specs
                 + [pl.BlockSpec(memory_space=pl.ANY), pl.BlockSpec(memory_space=pl.ANY)],
        out_specs=[pl.BlockSpec((TM, D_MODEL), lambda i: (first + i, 0)),
                   pl.BlockSpec((TM, LANES), lambda i: (first + i, 0))],
        out_shape=[jax.ShapeDtypeStruct((T_ALL, D_MODEL), F32), jax.ShapeDtypeStruct((T_ALL, LANES), F32)],
        scratch_shapes=[pltpu.VMEM((TM, D_MODEL), BF16)],
        input_output_aliases={14: 0, 15: 1},
        compiler_params=pltpu.CompilerParams(dimension_semantics=("arbitrary",),
                                             vmem_limit_bytes=VMEM_LIMIT),
        name="attn_mixer_sample",
    )(sinks, x, wq, *rope_s, kd, vd, wo, lg, lb, rwh, rwl, rb, x1, info)
    return x1, info


def _kv_kernel(x_ref, wkv_ref, c_ref, sa_ref, sb_ref, k_ref, v_ref):
    kv = jnp.dot(x_ref[...].astype(BF16), wkv_ref[...], preferred_element_type=F32)
    k_ref[...] = _rope_lanes(kv[:, :KV_WIDTH], c_ref[...], sa_ref[...], sb_ref[...])
    v_ref[...] = kv[:, KV_WIDTH:]


def _kv_proj(x, wkv, rope_all):
    tok = lambda i: (i, 0)
    return pl.pallas_call(
        _kv_kernel,
        grid=(T_ALL // TM,),
        in_specs=[pl.BlockSpec((TM, D_MODEL), tok), _full(wkv.shape)] + [pl.BlockSpec((TM, LANES), tok)] * 3,
        out_specs=[pl.BlockSpec((TM, KV_WIDTH), tok)] * 2,
        out_shape=[jax.ShapeDtypeStruct((T_ALL, KV_WIDTH), F32)] * 2,
        compiler_params=pltpu.CompilerParams(dimension_semantics=("arbitrary",)),
        name="kv_proj",
    )(x, wkv, *rope_all)


def _moe_kernel(te_ref, nv_ref, xs_ref, wg_ref, wu_ref, wd_ref, ys_ref, wgu_bf, wd_bf):
    i = pl.program_id(0)

    @pl.when(i < nv_ref[0])
    def _():
        e = te_ref[i]
        prev = te_ref[jnp.maximum(i - 1, 0)]

        @pl.when((i == 0) | (e != prev))
        def _():
            wgu_bf[:, :EXPERT_FF] = wg_ref[0].astype(BF16)
            wgu_bf[:, EXPERT_FF:] = wu_ref[0].astype(BF16)
            wd_bf[...] = wd_ref[0].astype(BF16)

        hgu = jnp.dot(xs_ref[...], wgu_bf[...], preferred_element_type=F32)
        hg = hgu[:, :EXPERT_FF]
        h = hg * jax.nn.sigmoid(hg) * hgu[:, EXPERT_FF:]
        ys_ref[...] = jnp.dot(h.astype(BF16), wd_bf[...], preferred_element_type=F32)


def _moe_experts(xs, tile_expert, n_valid, wg, wu, wd, layer):
    base = layer * N_EXPERTS
    row = lambda i, te, nv: (jnp.minimum(i, nv[0] - 1), 0)
    wsel = lambda i, te, nv: (base + te[i], 0, 0)
    return pl.pallas_call(
        _moe_kernel,
        grid_spec=pltpu.PrefetchScalarGridSpec(
            num_scalar_prefetch=2,
            grid=(MOE_TILES,),
            in_specs=[pl.BlockSpec((MOE_TM, D_MODEL), row),
                      pl.BlockSpec((1, D_MODEL, EXPERT_FF), wsel),
                      pl.BlockSpec((1, D_MODEL, EXPERT_FF), wsel),
                      pl.BlockSpec((1, EXPERT_FF, D_MODEL), wsel)],
            out_specs=pl.BlockSpec((MOE_TM, D_MODEL), row),
            scratch_shapes=[pltpu.VMEM((D_MODEL, 2 * EXPERT_FF), BF16), pltpu.VMEM((EXPERT_FF, D_MODEL), BF16)],
        ),
        out_shape=jax.ShapeDtypeStruct((MOE_ROWS, D_MODEL), F32),
        compiler_params=pltpu.CompilerParams(dimension_semantics=("arbitrary",), vmem_limit_bytes=VMEM_LIMIT),
        name="moe_experts",
    )(tile_expert, n_valid, xs, wg, wu, wd)


def _ffn_ln_kernel(x1_ref, y1_ref, y2_ref, info_ref, g_ref, b_ref, o_ref):
    info = info_ref[...]
    f = info[:, 2:3] * y1_ref[...] + info[:, 3:4] * y2_ref[...]
    o_ref[...] = _layer_norm(DEEPNORM_ALPHA * x1_ref[...] + f, g_ref[...], b_ref[...])


def _ffn_ln(x1, y1, y2, info, g, b):
    tok = lambda i: (i, 0)
    return pl.pallas_call(
        _ffn_ln_kernel,
        grid=(T_ALL // TM,),
        in_specs=[pl.BlockSpec((TM, D_MODEL), tok)] * 3 + [pl.BlockSpec((TM, LANES), tok), _full(g.shape),
                                                          _full(b.shape)],
        out_specs=pl.BlockSpec((TM, D_MODEL), tok),
        out_shape=jax.ShapeDtypeStruct((T_ALL, D_MODEL), F32),
        compiler_params=pltpu.CompilerParams(dimension_semantics=("arbitrary",)),
        name="ffn_ln",
    )(x1, y1, y2, info, g, b)


def _moe_layer(x1, info, wg, wu, wd, layer, g, b):
    n = 2 * T_ALL
    ef = info[:, :2].astype(jnp.int32).reshape(n)
    order = jnp.argsort(ef, stable=True).astype(jnp.int32)
    es = ef[order]
    counts = jnp.sum(ef[:, None] == jnp.arange(N_EXPERTS, dtype=jnp.int32)[None, :], axis=0, dtype=jnp.int32)
    tiles_per = (counts + MOE_TM - 1) // MOE_TM
    tile_end = jnp.cumsum(tiles_per)
    pad_start = (tile_end - tiles_per) * MOE_TM
    seg_start = jnp.cumsum(counts) - counts
    dest = pad_start[es] + jnp.arange(n, dtype=jnp.int32) - seg_start[es]
    src_tok = jnp.zeros((MOE_ROWS,), jnp.int32).at[dest].set(order // 2)
    pos = jnp.zeros((n,), jnp.int32).at[order].set(dest).reshape(T_ALL, 2)
    n_valid = tile_end[-1:]
    tile_ids = jnp.minimum(jnp.arange(MOE_TILES, dtype=jnp.int32), n_valid[0] - 1)
    tile_expert = jnp.sum(tile_ids[:, None] >= tile_end[None, :], axis=1, dtype=jnp.int32)
    xs = x1.astype(BF16)[src_tok]
    ys = _moe_experts(xs, tile_expert, n_valid, wg, wu, wd, layer)
    return _ffn_ln(x1, ys[pos[:, 0]], ys[pos[:, 1]], info, g, b)


def _rope_tables(pos):
    half = ROT_DIM // 2
    inv_freq = ROPE_THETA ** (-jnp.arange(0, ROT_DIM, 2, dtype=F32) / ROT_DIM)
    ang = pos.astype(F32)[:, None] * inv_freq[None, :]
    cos, sin = jnp.cos(ang), jnp.sin(ang)
    n = pos.shape[0]
    ones = jnp.ones((n, HEAD_DIM - ROT_DIM), F32)
    zeros_h = jnp.zeros((n, half), F32)
    zeros_r = jnp.zeros((n, HEAD_DIM - ROT_DIM), F32)
    c = jnp.concatenate([cos, cos, ones], axis=1)
    sa = jnp.concatenate([-sin, zeros_h, zeros_r], axis=1)
    sb = jnp.concatenate([zeros_h, sin, zeros_r], axis=1)
    rep = LANES // HEAD_DIM
    return tuple(jnp.tile(t, (1, rep)) for t in (c, sa, sb))


def _dup_heads(kv):
    b, l, _ = kv.shape
    h = kv.reshape(b, l, N_KV_HEADS, HEAD_DIM).transpose(0, 2, 1, 3)
    return jnp.concatenate([h, h], axis=-1).astype(BF16)


def kernel(x_prompt, x_sample, state_conv, cache_k, cache_v, ln_mix_g, ln_mix_b, ln_ffn_g, ln_ffn_b, conv_w_in, conv_w_dw, conv_ln_g, conv_ln_b, conv_w_out, w_kv, attn_w_q, attn_sinks, attn_w_o, router_w_group, router_b_group, router_w_expert, router_b_expert, expert_w_gate, expert_w_up, expert_w_down):
    x = jnp.concatenate([x_prompt.reshape(T_PROMPT, D_MODEL), x_sample.reshape(T_SAMPLE, D_MODEL)], axis=0)
    wg = expert_w_gate.reshape(DEPTH * N_EXPERTS, D_MODEL, EXPERT_FF)
    wu = expert_w_up.reshape(DEPTH * N_EXPERTS, D_MODEL, EXPERT_FF)
    wd = expert_w_down.reshape(DEPTH * N_EXPERTS, EXPERT_FF, D_MODEL)
    row = lambda v: v.reshape(1, -1)

    def router_weights(layer):
        w = jnp.concatenate([router_w_group[layer], router_w_expert[layer]], axis=1)
        w = jnp.pad(w, ((0, 0), (0, LANES - w.shape[1])))
        w_hi = w.astype(BF16)
        w_lo = (w - w_hi.astype(F32)).astype(BF16)
        bias = jnp.concatenate([router_b_group[layer], router_b_expert[layer].reshape(-1)])
        return w_hi, w_lo, jnp.pad(bias, (0, LANES - bias.shape[0])).reshape(1, LANES)

    pos_p = jnp.arange(SEQ, dtype=jnp.int32)
    pos_s = PAST_LEN + jnp.arange(DEC_SEQ, dtype=jnp.int32)
    rope_p = _rope_tables(pos_p)
    rope_s1 = _rope_tables(pos_s)
    rope_s = tuple(jnp.tile(t, (TM // DEC_SEQ, 1)) for t in rope_s1)
    rope_all = tuple(jnp.concatenate([jnp.tile(p, (BATCH, 1)), jnp.tile(s, (DEC_BATCH, 1))], axis=0)
                     for p, s in zip(rope_p, rope_s1))

    tails, us = [], []
    kv_p = kv_s = None
    outs_kv = None
    for layer in range(DEPTH):
        lg, lb = row(ln_mix_g[layer]), row(ln_mix_b[layer])
        rwh, rwl, rb = router_weights(layer)
        if layer < N_A_LAYERS:
            st = jnp.pad(state_conv[layer], ((0, 0), (STATE_PAD - (CONV_WIDTH - 1), 0), (0, 0)))
            wdw = jnp.pad(conv_w_dw[layer], ((0, STATE_PAD - CONV_WIDTH), (0, 0)))
            wdw = wdw.reshape(STATE_PAD, N_LANE_CHUNKS, LANES).transpose(1, 0, 2)
            w = (conv_w_in[layer].astype(BF16), wdw, row(conv_ln_g[layer]), row(conv_ln_b[layer]),
                 conv_w_out[layer].astype(BF16), lg, lb, rwh, rwl, rb)
            x1, info, tail, u_s = _conv_mixer(x, st, w)
            tails.append(tail[:, STATE_PAD - (CONV_WIDTH - 1):, :])
            us.append(u_s.reshape(DEC_BATCH, DEC_SEQ, D_MODEL)[:, DEC_SEQ - (CONV_WIDTH - 1):, :])
        else:
            j = layer - N_A_LAYERS
            w = (attn_w_q[j].astype(BF16), attn_w_o[j].astype(BF16), lg, lb, rwh, rwl, rb)
            x1, info = _attn_mixer(x, attn_sinks[j], w, rope_p, rope_s, kv_p, kv_s)
        x = _moe_layer(x1, info, wg, wu, wd, layer, row(ln_ffn_g[layer]), row(ln_ffn_b[layer]))
        if layer == N_A_LAYERS - 1:
            k, v = _kv_proj(x, w_kv.astype(BF16), rope_all)
            k_p = k[:T_PROMPT].reshape(BATCH, SEQ, KV_WIDTH)
            v_p = v[:T_PROMPT].reshape(BATCH, SEQ, KV_WIDTH)
            k_s = jnp.concatenate([cache_k.reshape(DEC_BATCH, WINDOW, KV_WIDTH),
                                   k[T_PROMPT:].reshape(DEC_BATCH, DEC_SEQ, KV_WIDTH)], axis=1)
            v_s = jnp.concatenate([cache_v.reshape(DEC_BATCH, WINDOW, KV_WIDTH),
                                   v[T_PROMPT:].reshape(DEC_BATCH, DEC_SEQ, KV_WIDTH)], axis=1)
            cache_shape = (-1, WINDOW, N_KV_HEADS, HEAD_DIM)
            outs_kv = (k_p[:, -WINDOW:].reshape(cache_shape), v_p[:, -WINDOW:].reshape(cache_shape),
                       k_s[:, -WINDOW:].reshape(cache_shape), v_s[:, -WINDOW:].reshape(cache_shape))
            front = ((0, 0), (WINDOW, 0), (0, 0))
            back = ((0, 0), (0, KEY_WIN - WINDOW - DEC_SEQ), (0, 0))
            kv_p = (_dup_heads(jnp.pad(k_p, front)), _dup_heads(jnp.pad(v_p, front)))
            kv_s = (_dup_heads(jnp.pad(k_s, back)), _dup_heads(jnp.pad(v_s, back)))
    y_prompt = x[:T_PROMPT].reshape(BATCH, SEQ, D_MODEL)
    y_sample = x[T_PROMPT:].reshape(DEC_BATCH, DEC_SEQ, D_MODEL)
    return (y_prompt, y_sample, jnp.stack(tails, axis=0), jnp.stack(us, axis=0)) + outs_kv
```

```python
import functools

import jax
import jax.numpy as jnp
from jax import lax
from jax.experimental import pallas as pl
from jax.experimental.pallas import tpu as pltpu

D_MODEL = 1024
BATCH = 2
SEQ = 8192
DEPTH = 4
DEC_BATCH = 32
DEC_SEQ = 32
PAST_LEN = 2048
CHUNK = 64
N_A_LAYERS = DEPTH // 2
CONV_WIDTH = 31
N_HEADS = 16
N_KV_HEADS = 2
HEAD_DIM = 64
KV_WIDTH = N_KV_HEADS * HEAD_DIM
WINDOW = 128
ROT_DIM = HEAD_DIM // 4
ROPE_THETA = 500000.0
N_GROUPS = 4
EXPERTS_PER_GROUP = 8
N_EXPERTS = N_GROUPS * EXPERTS_PER_GROUP
EXPERT_FF = D_MODEL // 2
DEEPNORM_ALPHA = (2 * DEPTH) ** 0.25
LN_EPS = 1e-5
ATTN_SCALE = HEAD_DIM ** -0.5
NEG_INF = -1e30

LANES = 128
N_LANE_CHUNKS = D_MODEL // LANES
T_PROMPT = BATCH * SEQ
T_SAMPLE = DEC_BATCH * DEC_SEQ
T_ALL = T_PROMPT + T_SAMPLE
TM = 256
STATE_PAD = 32
KEY_WIN = 256
UQ_PROMPT = 2 * CHUNK
MOE_TM = 256
MOE_ROWS = 2 * T_ALL + N_EXPERTS * MOE_TM
MOE_TILES = MOE_ROWS // MOE_TM
VMEM_LIMIT = 56 * 1024 * 1024

F32 = jnp.float32
BF16 = jnp.bfloat16


def _layer_norm(x, g, b):
    mu = jnp.mean(x, axis=-1, keepdims=True)
    xc = x - mu
    var = jnp.mean(xc * xc, axis=-1, keepdims=True)
    return xc * lax.rsqrt(var + LN_EPS) * g + b


def _router(x1, rwh_ref, rwl_ref, rb_ref, cnt_ref):
    x_hi = x1.astype(BF16)
    x_lo = (x1 - x_hi.astype(F32)).astype(BF16)
    w_hi = rwh_ref[...]
    logits = (jnp.dot(x_hi, w_hi, preferred_element_type=F32)
              + jnp.dot(x_hi, rwl_ref[...], preferred_element_type=F32)
              + jnp.dot(x_lo, w_hi, preferred_element_type=F32)) + rb_ref[...]
    lane_i = lax.broadcasted_iota(jnp.int32, logits.shape, 1)
    lane = lane_i.astype(F32)
    neg = jnp.float32(-3.0e38)
    none = jnp.float32(LANES)
    gl = jnp.where(lane_i < N_GROUPS, logits, neg)
    gmax = jnp.max(gl, axis=-1, keepdims=True)
    g_idx = jnp.min(jnp.where(gl == gmax, lane, none), axis=-1, keepdims=True)
    gsum = jnp.sum(jnp.where(lane_i < N_GROUPS, jnp.exp(gl - gmax), 0.0), axis=-1, keepdims=True)
    p_grp = 1.0 / gsum
    lo = N_GROUPS + EXPERTS_PER_GROUP * g_idx
    el = jnp.where(lane >= lo, jnp.where(lane < lo + EXPERTS_PER_GROUP, logits, neg), neg)
    l1 = jnp.max(el, axis=-1, keepdims=True)
    i1 = jnp.min(jnp.where(el == l1, lane, none), axis=-1, keepdims=True)
    el2 = jnp.where(lane == i1, neg, el)
    l2 = jnp.max(el2, axis=-1, keepdims=True)
    i2 = jnp.min(jnp.where(el2 == l2, lane, none), axis=-1, keepdims=True)
    t = jnp.exp(l2 - l1)
    inv = p_grp / (1.0 + t)
    e1 = i1 - N_GROUPS
    e2 = i2 - N_GROUPS
    rows = logits.shape[0]
    onehot = jnp.where(lane == e1, 1.0, jnp.where(lane == e2, 1.0, 0.0))
    tri = (lax.broadcasted_iota(jnp.int32, (rows, rows), 0) > lax.broadcasted_iota(jnp.int32, (rows, rows), 1))
    before = jnp.dot(jnp.where(tri, 1.0, 0.0).astype(BF16), onehot.astype(BF16),
                     preferred_element_type=F32) + cnt_ref[...]
    r1 = jnp.sum(jnp.where(lane == e1, before, 0.0), axis=-1, keepdims=True)
    r2 = jnp.sum(jnp.where(lane == e2, before, 0.0), axis=-1, keepdims=True)
    cnt_ref[...] = cnt_ref[...] + jnp.sum(onehot, axis=0, keepdims=True)
    out = jnp.where(lane_i == 4, r1, jnp.where(lane_i == 5, r2, 0.0))
    out = jnp.where(lane_i == 2, inv, jnp.where(lane_i == 3, inv * t, out))
    return jnp.where(lane_i == 0, e1, jnp.where(lane_i == 1, e2, out))


def _post_mixer(x, mix, lg_ref, lb_ref, rwh_ref, rwl_ref, rb_ref, x1_ref, info_ref, cnt_out_ref, cnt_ref):
    x1 = _layer_norm(DEEPNORM_ALPHA * x + mix, lg_ref[...], lb_ref[...])
    x1_ref[...] = x1
    info_ref[...] = _router(x1, rwh_ref, rwl_ref, rb_ref, cnt_ref)
    cnt_out_ref[...] = cnt_ref[...]


def _init_counts(cnt_ref, first_step, cnt_in_ref):
    @pl.when(first_step)
    def _():
        cnt_ref[...] = jnp.zeros_like(cnt_ref) if cnt_in_ref is None else cnt_in_ref[...]


def _conv_mixer_body(x_ref, st_ref, win_ref, wdw_ref, cg_ref, cb_ref, wout_ref, post_w, cnt_in_ref,
                     x1_ref, info_ref, cnt_out_ref, u_ref, cnt_ref, ubuf, cbuf, *, nseg, seg):
    carry = st_ref is None
    first_step = (pl.program_id(0) == 0) & (pl.program_id(1) == 0) if carry else pl.program_id(0) == 0
    _init_counts(cnt_ref, first_step, cnt_in_ref)
    x = x_ref[...]
    h = jnp.dot(x.astype(BF16), win_ref[...], preferred_element_type=F32)
    u = h[:, :D_MODEL] * jax.nn.sigmoid(h[:, D_MODEL:])
    if carry:
        @pl.when(pl.program_id(1) == 0)
        def _():
            ubuf[:, 0, 0:STATE_PAD, :] = jnp.zeros((N_LANE_CHUNKS, STATE_PAD, LANES), F32)
        u_ref[0] = u[seg - STATE_PAD:, :]
    else:
        u_ref[...] = u
    for lc in range(N_LANE_CHUNKS):
        for s in range(nseg):
            ubuf[lc, s, STATE_PAD:STATE_PAD + seg, :] = u[s * seg:(s + 1) * seg, lc * LANES:(lc + 1) * LANES]
            if not carry:
                ubuf[lc, s, 0:STATE_PAD, :] = st_ref[s, :, lc * LANES:(lc + 1) * LANES]

    def conv_lane_chunk(lc, c):
        w = wdw_ref[lc]
        for s in range(nseg):
            acc = jnp.zeros((seg, LANES), F32)
            for k in range(CONV_WIDTH):
                off = k + STATE_PAD - (CONV_WIDTH - 1)
                acc = acc + w[k:k + 1, :] * ubuf[lc, s, off:off + seg, :]
            cbuf[lc, s * seg:(s + 1) * seg, :] = acc
        return c

    lax.fori_loop(0, N_LANE_CHUNKS, conv_lane_chunk, 0)
    if carry:
        for lc in range(N_LANE_CHUNKS):
            ubuf[lc, 0, 0:STATE_PAD, :] = u[seg - STATE_PAD:, lc * LANES:(lc + 1) * LANES]
    dw = jnp.concatenate([cbuf[lc] for lc in range(N_LANE_CHUNKS)], axis=1)
    sw = _layer_norm(dw, cg_ref[...], cb_ref[...])
    sw = sw * jax.nn.sigmoid(sw)
    mix = jnp.dot(sw.astype(BF16), wout_ref[...], preferred_element_type=F32)
    _post_mixer(x, mix, *post_w, x1_ref, info_ref, cnt_out_ref, cnt_ref)


N_POST_W = 5


def _conv_prompt_kernel(x_ref, win_ref, wdw_ref, cg_ref, cb_ref, wout_ref, *rest):
    post_w, rest = rest[:N_POST_W], rest[N_POST_W:]
    _conv_mixer_body(x_ref, None, win_ref, wdw_ref, cg_ref, cb_ref, wout_ref, post_w, None, *rest,
                     nseg=1, seg=TM)


def _conv_sample_kernel(x_ref, st_ref, win_ref, wdw_ref, cg_ref, cb_ref, wout_ref, *rest):
    post_w, cnt_in_ref, rest = rest[:N_POST_W], rest[N_POST_W], rest[N_POST_W + 3:]
    _conv_mixer_body(x_ref, st_ref, win_ref, wdw_ref, cg_ref, cb_ref, wout_ref, post_w, cnt_in_ref, *rest,
                     nseg=TM // DEC_SEQ, seg=DEC_SEQ)


def _full(shape):
    return pl.BlockSpec(shape, lambda *_: (0,) * len(shape))


ANY_SPEC = pl.BlockSpec(memory_space=pl.ANY)
CNT_SHAPE = jax.ShapeDtypeStruct((1, LANES), F32)
CNT_SCRATCH = pltpu.VMEM((1, LANES), F32)


def _conv_mixer(x, state_pad, w):
    w_specs = [_full(a.shape) for a in w]
    tiles_per_seq = SEQ // TM
    tok = lambda b, t: (b * tiles_per_seq + t, 0)
    x1, info, cnt, tail = pl.pallas_call(
        _conv_prompt_kernel,
        grid=(BATCH, tiles_per_seq),
        in_specs=[pl.BlockSpec((TM, D_MODEL), tok)] + w_specs,
        out_specs=[pl.BlockSpec((TM, D_MODEL), tok), pl.BlockSpec((TM, LANES), tok), _full(CNT_SHAPE.shape),
                   pl.BlockSpec((1, STATE_PAD, D_MODEL), lambda b, t: (b, 0, 0))],
        out_shape=[jax.ShapeDtypeStruct((T_ALL, D_MODEL), F32), jax.ShapeDtypeStruct((T_ALL, LANES), F32),
                   CNT_SHAPE, jax.ShapeDtypeStruct((BATCH, STATE_PAD, D_MODEL), F32)],
        scratch_shapes=[CNT_SCRATCH, pltpu.VMEM((N_LANE_CHUNKS, 1, STATE_PAD + TM, LANES), F32),
                        pltpu.VMEM((N_LANE_CHUNKS, TM, LANES), F32)],
        compiler_params=pltpu.CompilerParams(dimension_semantics=("arbitrary", "arbitrary"),
                                             vmem_limit_bytes=VMEM_LIMIT),
        name="conv_mixer_prompt",
    )(x, *w)
    nb = TM // DEC_SEQ
    first = T_PROMPT // TM
    stok = lambda i: (first + i, 0)
    n_in = 2 + len(w) + 1
    x1, info, cnt, u_s = pl.pallas_call(
        _conv_sample_kernel,
        grid=(T_SAMPLE // TM,),
        in_specs=[pl.BlockSpec((TM, D_MODEL), stok), pl.BlockSpec((nb, STATE_PAD, D_MODEL), lambda i: (i, 0, 0))]
                 + w_specs + [_full(CNT_SHAPE.shape), ANY_SPEC, ANY_SPEC],
        out_specs=[pl.BlockSpec((TM, D_MODEL), stok), pl.BlockSpec((TM, LANES), stok), _full(CNT_SHAPE.shape),
                   pl.BlockSpec((TM, D_MODEL), lambda i: (i, 0))],
        out_shape=[jax.ShapeDtypeStruct((T_ALL, D_MODEL), F32), jax.ShapeDtypeStruct((T_ALL, LANES), F32),
                   CNT_SHAPE, jax.ShapeDtypeStruct((T_SAMPLE, D_MODEL), F32)],
        scratch_shapes=[CNT_SCRATCH, pltpu.VMEM((N_LANE_CHUNKS, nb, STATE_PAD + DEC_SEQ, LANES), F32),
                        pltpu.VMEM((N_LANE_CHUNKS, TM, LANES), F32)],
        input_output_aliases={n_in: 0, n_in + 1: 1},
        compiler_params=pltpu.CompilerParams(dimension_semantics=("arbitrary",),
                                             vmem_limit_bytes=VMEM_LIMIT),
        name="conv_mixer_sample",
    )(x, state_pad, *w, cnt, x1, info)
    return x1, info, cnt, tail, u_s


def _rope_lanes(v, c, sa, sb):
    width = v.shape[-1]
    half = ROT_DIM // 2
    return v * c + pltpu.roll(v, width - half, 1) * sa + pltpu.roll(v, half, 1) * sb


def _attn_mixer_body(sink_ref, x_ref, wq_ref, c_ref, sa_ref, sb_ref, kd_ref, vd_ref, wo_ref, post_w, cnt_in_ref,
                     x1_ref, info_ref, cnt_out_ref, cnt_ref, obuf, *, nunit, uq, prompt):
    first_step = (pl.program_id(0) == 0) & (pl.program_id(1) == 0) if prompt else pl.program_id(0) == 0
    _init_counts(cnt_ref, first_step, cnt_in_ref)
    x = x_ref[...]
    q = jnp.dot(x.astype(BF16), wq_ref[...], preferred_element_type=F32)
    rep = D_MODEL // LANES
    q = _rope_lanes(q, jnp.tile(c_ref[...], (1, rep)), jnp.tile(sa_ref[...], (1, rep)),
                    jnp.tile(sb_ref[...], (1, rep)))
    qb = (q * ATTN_SCALE).astype(BF16)
    pairs = N_HEADS // N_KV_HEADS // 2
    rows = pairs * uq
    lane_k = lax.broadcasted_iota(jnp.int32, (KEY_WIN, LANES), 1)
    col = lax.broadcasted_iota(jnp.int32, (rows, 2 * KEY_WIN), 1) & (KEY_WIN - 1)
    if prompt:
        chunk_shift = CHUNK.bit_length() - 1
        qchunk = (lax.broadcasted_iota(jnp.int32, (rows, 2 * KEY_WIN), 0) & (uq - 1)) >> chunk_shift
        kchunk = col >> chunk_shift
        band = (kchunk >= qchunk) & (kchunk <= qchunk + WINDOW // CHUNK)
    lane_o = lax.broadcasted_iota(jnp.int32, (uq, LANES), 1)
    for un in range(nunit):
        r0 = un * uq
        if prompt:
            kstart = pl.multiple_of(pl.program_id(1) * (nunit * uq) + r0, LANES)
            mask = band & (col + kstart >= WINDOW)
        else:
            mask = col < WINDOW + DEC_SEQ
        for kvh in range(N_KV_HEADS):
            if prompt:
                kw = kd_ref[0, kvh, pl.ds(kstart, KEY_WIN), :]
                vw = vd_ref[0, kvh, pl.ds(kstart, KEY_WIN), :]
            else:
                kw = kd_ref[un, kvh]
                vw = vd_ref[un, kvh]
            zero = jnp.zeros_like(kw)
            kbd = jnp.concatenate([jnp.where(lane_k < HEAD_DIM, kw, zero),
                                   jnp.where(lane_k >= HEAD_DIM, kw, zero)], axis=0)
            vbd = jnp.concatenate([jnp.where(lane_k < HEAD_DIM, vw, zero),
                                   jnp.where(lane_k >= HEAD_DIM, vw, zero)], axis=0)
            q4 = jnp.concatenate([qb[r0:r0 + uq, (kvh * pairs + p) * LANES:(kvh * pairs + p + 1) * LANES]
                                  for p in range(pairs)], axis=0)
            s = lax.dot_general(q4, kbd, (((1,), (1,)), ((), ())), preferred_element_type=F32)
            s = jnp.where(mask, s, NEG_INF)
            p_parts, inv_parts = [], []
            for p in range(pairs):
                halves, invs = [], []
                for hf in range(2):
                    sk = sink_ref[(kvh * pairs + p) * 2 + hf]
                    sh = s[p * uq:(p + 1) * uq, hf * KEY_WIN:(hf + 1) * KEY_WIN]
                    m = jnp.maximum(jnp.max(sh, axis=-1, keepdims=True), sk)
                    pe = jnp.exp(sh - m)
                    den = jnp.sum(pe, axis=-1, keepdims=True) + jnp.exp(sk - m)
                    halves.append(pe.astype(BF16))
                    invs.append(1.0 / den)
                p_parts.append(jnp.concatenate(halves, axis=1))
                inv_parts.append(invs)
            pm = jnp.concatenate(p_parts, axis=0)
            o4 = jnp.dot(pm, vbd, preferred_element_type=F32)
            for p in range(pairs):
                inv = jnp.where(lane_o < HEAD_DIM, inv_parts[p][0], inv_parts[p][1])
                hp = kvh * pairs + p
                obuf[r0:r0 + uq, hp * LANES:(hp + 1) * LANES] = (o4[p * uq:(p + 1) * uq, :] * inv).astype(BF16)
    mix = jnp.dot(obuf[...], wo_ref[...], preferred_element_type=F32)
    _post_mixer(x, mix, *post_w, x1_ref, info_ref, cnt_out_ref, cnt_ref)


N_ATTN_IN = 9


def _attn_prompt_kernel(*refs):
    ins, post_w, rest = refs[:N_ATTN_IN], refs[N_ATTN_IN:N_ATTN_IN + N_POST_W], refs[N_ATTN_IN + N_POST_W:]
    _attn_mixer_body(*ins, post_w, None, *rest, nunit=TM // UQ_PROMPT, uq=UQ_PROMPT, prompt=True)


def _attn_sample_kernel(*refs):
    ins, post_w = refs[:N_ATTN_IN], refs[N_ATTN_IN:N_ATTN_IN + N_POST_W]
    cnt_in_ref, rest = refs[N_ATTN_IN + N_POST_W], refs[N_ATTN_IN + N_POST_W + 3:]
    _attn_mixer_body(*ins, post_w, cnt_in_ref, *rest, nunit=TM // DEC_SEQ, uq=DEC_SEQ, prompt=False)


def _attn_mixer(x, sinks, wq, wo, post_w, rope_p, rope_s, kv_p, kv_s):
    tiles_per_seq = SEQ // TM
    smem = pl.BlockSpec(memory_space=pltpu.SMEM)
    tail_specs = [_full(wo.shape)] + [_full(a.shape) for a in post_w]
    kd, vd = kv_p
    tok = lambda b, t: (b * tiles_per_seq + t, 0)
    x1, info, cnt = pl.pallas_call(
        _attn_prompt_kernel,
        grid=(BATCH, tiles_per_seq),
        in_specs=[smem, pl.BlockSpec((TM, D_MODEL), tok), _full(wq.shape)]
                 + [pl.BlockSpec((TM, LANES), lambda b, t: (t, 0))] * 3
                 + [pl.BlockSpec((1,) + kd.shape[1:], lambda b, t: (b, 0, 0, 0))] * 2 + tail_specs,
        out_specs=[pl.BlockSpec((TM, D_MODEL), tok), pl.BlockSpec((TM, LANES), tok), _full(CNT_SHAPE.shape)],
        out_shape=[jax.ShapeDtypeStruct((T_ALL, D_MODEL), F32), jax.ShapeDtypeStruct((T_ALL, LANES), F32),
                   CNT_SHAPE],
        scratch_shapes=[CNT_SCRATCH, pltpu.VMEM((TM, D_MODEL), BF16)],
        compiler_params=pltpu.CompilerParams(dimension_semantics=("arbitrary", "arbitrary"),
                                             vmem_limit_bytes=VMEM_LIMIT),
        name="attn_mixer_prompt",
    )(sinks, x, wq, *rope_p, kd, vd, wo, *post_w)
    nb = TM // DEC_SEQ
    first = T_PROMPT // TM
    stok = lambda i: (first + i, 0)
    kd, vd = kv_s
    n_in = N_ATTN_IN + N_POST_W + 1
    x1, info, cnt = pl.pallas_call(
        _attn_sample_kernel,
        grid=(T_SAMPLE // TM,),
        in_specs=[smem, pl.BlockSpec((TM, D_MODEL), stok), _full(wq.shape)] + [_full((TM, LANES))] * 3
                 + [pl.BlockSpec((nb,) + kd.shape[1:], lambda i: (i, 0, 0, 0))] * 2 + tail_specs
                 + [_full(CNT_SHAPE.shape), ANY_SPEC, ANY_SPEC],
        out_specs=[pl.BlockSpec((TM, D_MODEL), stok), pl.BlockSpec((TM, LANES), stok), _full(CNT_SHAPE.shape)],
        out_shape=[jax.ShapeDtypeStruct((T_ALL, D_MODEL), F32), jax.ShapeDtypeStruct((T_ALL, LANES), F32),
                   CNT_SHAPE],
        scratch_shapes=[CNT_SCRATCH, pltpu.VMEM((TM, D_MODEL), BF16)],
        input_output_aliases={n_in: 0, n_in + 1: 1},
        compiler_params=pltpu.CompilerParams(dimension_semantics=("arbitrary",),
                                             vmem_limit_bytes=VMEM_LIMIT),
        name="attn_mixer_sample",
    )(sinks, x, wq, *rope_s, kd, vd, wo, *post_w, cnt, x1, info)
    return x1, info, cnt


def _kv_kernel(x_ref, wkv_ref, c_ref, sa_ref, sb_ref, k_ref, v_ref):
    kv = jnp.dot(x_ref[...].astype(BF16), wkv_ref[...], preferred_element_type=F32)
    k_ref[...] = _rope_lanes(kv[:, :KV_WIDTH], c_ref[...], sa_ref[...], sb_ref[...])
    v_ref[...] = kv[:, KV_WIDTH:]


def _kv_proj(x, wkv, rope_all):
    tok = lambda i: (i, 0)
    return pl.pallas_call(
        _kv_kernel,
        grid=(T_ALL // TM,),
        in_specs=[pl.BlockSpec((TM, D_MODEL), tok), _full(wkv.shape)] + [pl.BlockSpec((TM, LANES), tok)] * 3,
        out_specs=[pl.BlockSpec((TM, KV_WIDTH), tok)] * 2,
        out_shape=[jax.ShapeDtypeStruct((T_ALL, KV_WIDTH), F32)] * 2,
        compiler_params=pltpu.CompilerParams(dimension_semantics=("arbitrary",)),
        name="kv_proj",
    )(x, wkv, *rope_all)


N_TILES = T_ALL // TM
ROW_GROUP = 8


def _row_copy_wait(src_rows, dst_rows, sem):
    pltpu.make_async_copy(src_rows, dst_rows, sem).wait()


def _dispatch_kernel(zrow_ref, zflag_ref, pos_ref, x1_hbm, xs_hbm, zbuf, sem, zsem):
    i = pl.program_id(0)

    @pl.when(i == 0)
    def _():
        zbuf[...] = jnp.zeros_like(zbuf)

        def zero_tile(e):
            return pltpu.make_async_copy(zbuf, xs_hbm.at[pl.ds(pl.multiple_of(zrow_ref[e], MOE_TM), MOE_TM)], zsem)

        for e in range(N_EXPERTS):
            @pl.when(zflag_ref[e] > 0)
            def _():
                zero_tile(e).start()
        for e in range(N_EXPERTS):
            @pl.when(zflag_ref[e] > 0)
            def _():
                zero_tile(e).wait()

    def issue(g, c):
        for r in range(ROW_GROUP):
            j = g * ROW_GROUP + r
            src = x1_hbm.at[pl.ds(i * TM + j, 1)]
            for slot in range(2):
                pltpu.make_async_copy(src, xs_hbm.at[pl.ds(pos_ref[0, 0, 2 * j + slot], 1)], sem).start()
        return c

    lax.fori_loop(0, TM // ROW_GROUP, issue, 0)

    @pl.when(i > 0)
    def _():
        _row_copy_wait(x1_hbm.at[pl.ds(0, 2 * TM)], xs_hbm.at[pl.ds(0, 2 * TM)], sem)

    @pl.when(i == N_TILES - 1)
    def _():
        _row_copy_wait(x1_hbm.at[pl.ds(0, 2 * TM)], xs_hbm.at[pl.ds(0, 2 * TM)], sem)


def _dispatch(x1, pos, zrow, zflag):
    return pl.pallas_call(
        _dispatch_kernel,
        grid_spec=pltpu.PrefetchScalarGridSpec(
            num_scalar_prefetch=2,
            grid=(N_TILES,),
            in_specs=[pl.BlockSpec((1, 1, 2 * TM), lambda i, zr, zf: (i, 0, 0), memory_space=pltpu.SMEM),
                      ANY_SPEC],
            out_specs=ANY_SPEC,
            scratch_shapes=[pltpu.VMEM((MOE_TM, D_MODEL), F32), pltpu.SemaphoreType.DMA,
                            pltpu.SemaphoreType.DMA],
        ),
        out_shape=jax.ShapeDtypeStruct((MOE_ROWS, D_MODEL), F32),
        compiler_params=pltpu.CompilerParams(dimension_semantics=("arbitrary",)),
        name="moe_dispatch",
    )(zrow, zflag, pos, x1)


def _moe_kernel(te_ref, nv_ref, xs_ref, wg_ref, wu_ref, wd_ref, ys_ref, wgu_bf, wd_bf):
    i = pl.program_id(0)

    @pl.when(i < nv_ref[0])
    def _():
        e = te_ref[i]
        prev = te_ref[jnp.maximum(i - 1, 0)]

        @pl.when((i == 0) | (e != prev))
        def _():
            wgu_bf[:, :EXPERT_FF] = wg_ref[0].astype(BF16)
            wgu_bf[:, EXPERT_FF:] = wu_ref[0].astype(BF16)
            wd_bf[...] = wd_ref[0].astype(BF16)

        hgu = jnp.dot(xs_ref[...].astype(BF16), wgu_bf[...], preferred_element_type=F32)
        hg = hgu[:, :EXPERT_FF]
        h = hg * jax.nn.sigmoid(hg) * hgu[:, EXPERT_FF:]
        ys_ref[...] = jnp.dot(h.astype(BF16), wd_bf[...], preferred_element_type=F32)


def _moe_experts(xs, tile_expert, n_valid, wg, wu, wd, layer):
    base = layer * N_EXPERTS
    row = lambda i, te, nv: (jnp.minimum(i, nv[0] - 1), 0)
    wsel = lambda i, te, nv: (base + te[i], 0, 0)
    return pl.pallas_call(
        _moe_kernel,
        grid_spec=pltpu.PrefetchScalarGridSpec(
            num_scalar_prefetch=2,
            grid=(MOE_TILES,),
            in_specs=[pl.BlockSpec((MOE_TM, D_MODEL), row),
                      pl.BlockSpec((1, D_MODEL, EXPERT_FF), wsel),
                      pl.BlockSpec((1, D_MODEL, EXPERT_FF), wsel),
                      pl.BlockSpec((1, EXPERT_FF, D_MODEL), wsel)],
            out_specs=pl.BlockSpec((MOE_TM, D_MODEL), row),
            scratch_shapes=[pltpu.VMEM((D_MODEL, 2 * EXPERT_FF), BF16), pltpu.VMEM((EXPERT_FF, D_MODEL), BF16)],
        ),
        out_shape=jax.ShapeDtypeStruct((MOE_ROWS, D_MODEL), F32),
        compiler_params=pltpu.CompilerParams(dimension_semantics=("arbitrary",), vmem_limit_bytes=VMEM_LIMIT),
        name="moe_experts",
    )(tile_expert, n_valid, xs, wg, wu, wd)


def _ffn_ln_kernel(pos_ref, pos_next_ref, x1_ref, info_ref, g_ref, b_ref, ys_hbm, o_ref, ybuf, sem):
    i = pl.program_id(0)

    def issue(p_ref, par):
        def body(g, c):
            for r in range(ROW_GROUP):
                j = g * ROW_GROUP + r
                for slot in range(2):
                    pltpu.make_async_copy(ys_hbm.at[pl.ds(p_ref[0, 0, 2 * j + slot], 1)],
                                          ybuf.at[par, slot, pl.ds(j, 1)], sem.at[par]).start()
            return c
        lax.fori_loop(0, TM // ROW_GROUP, body, 0)

    @pl.when(i == 0)
    def _():
        issue(pos_ref, 0)

    @pl.when(i + 1 < N_TILES)
    def _():
        issue(pos_next_ref, (i + 1) % 2)

    par = i % 2
    for slot in range(2):
        _row_copy_wait(ys_hbm.at[pl.ds(0, TM)], ybuf.at[par, slot], sem.at[par])
    info = info_ref[...]
    f = info[:, 2:3] * ybuf[par, 0] + info[:, 3:4] * ybuf[par, 1]
    o_ref[...] = _layer_norm(DEEPNORM_ALPHA * x1_ref[...] + f, g_ref[...], b_ref[...])


def _ffn_ln(x1, ys, pos, info, g, b):
    tok = lambda i: (i, 0)
    pos_spec = lambda fn: pl.BlockSpec((1, 1, 2 * TM), fn, memory_space=pltpu.SMEM)
    return pl.pallas_call(
        _ffn_ln_kernel,
        grid=(N_TILES,),
        in_specs=[pos_spec(lambda i: (i, 0, 0)), pos_spec(lambda i: (jnp.minimum(i + 1, N_TILES - 1), 0, 0)),
                  pl.BlockSpec((TM, D_MODEL), tok), pl.BlockSpec((TM, LANES), tok), _full(g.shape), _full(b.shape),
                  ANY_SPEC],
        out_specs=pl.BlockSpec((TM, D_MODEL), tok),
        out_shape=jax.ShapeDtypeStruct((T_ALL, D_MODEL), F32),
        scratch_shapes=[pltpu.VMEM((2, 2, TM, D_MODEL), F32), pltpu.SemaphoreType.DMA((2,))],
        compiler_params=pltpu.CompilerParams(dimension_semantics=("arbitrary",)),
        name="ffn_ln",
    )(pos, pos, x1, info, g, b, ys)


def _moe_layer(x1, info, cnt, wg, wu, wd, layer, g, b):
    counts = cnt[0, :N_EXPERTS].astype(jnp.int32)
    tiles_per = (counts + MOE_TM - 1) // MOE_TM
    tile_end = jnp.cumsum(tiles_per)
    pad_start = (tile_end - tiles_per) * MOE_TM
    expert = info[:, 0:2].astype(jnp.int32)
    rank = info[:, 4:6].astype(jnp.int32)
    ids = jnp.arange(N_EXPERTS, dtype=jnp.int32)
    start = jnp.sum(jnp.where(expert[:, :, None] == ids[None, None, :], pad_start[None, None, :], 0), axis=-1)
    pos = (start + rank).reshape(N_TILES, 1, 2 * TM)
    n_valid = tile_end[-1:]
    tile_ids = jnp.minimum(jnp.arange(MOE_TILES, dtype=jnp.int32), n_valid[0] - 1)
    tile_expert = jnp.sum(tile_ids[:, None] >= tile_end[None, :], axis=1, dtype=jnp.int32)
    zrow = jnp.maximum(tile_end - 1, 0) * MOE_TM
    xs = _dispatch(x1, pos, zrow, tiles_per)
    ys = _moe_experts(xs, tile_expert, n_valid, wg, wu, wd, layer)
    return _ffn_ln(x1, ys, pos, info, g, b)


def _rope_tables(pos):
    half = ROT_DIM // 2
    inv_freq = ROPE_THETA ** (-jnp.arange(0, ROT_DIM, 2, dtype=F32) / ROT_DIM)
    ang = pos.astype(F32)[:, None] * inv_freq[None, :]
    cos, sin = jnp.cos(ang), jnp.sin(ang)
    n = pos.shape[0]
    ones = jnp.ones((n, HEAD_DIM - ROT_DIM), F32)
    zeros_h = jnp.zeros((n, half), F32)
    zeros_r = jnp.zeros((n, HEAD_DIM - ROT_DIM), F32)
    c = jnp.concatenate([cos, cos, ones], axis=1)
    sa = jnp.concatenate([-sin, zeros_h, zeros_r], axis=1)
    sb = jnp.concatenate([zeros_h, sin, zeros_r], axis=1)
    rep = LANES // HEAD_DIM
    return tuple(jnp.tile(t, (1, rep)) for t in (c, sa, sb))


def _dup_heads(kv):
    b, l, _ = kv.shape
    h = kv.reshape(b, l, N_KV_HEADS, HEAD_DIM).transpose(0, 2, 1, 3)
    return jnp.concatenate([h, h], axis=-1).astype(BF16)


def kernel(x_prompt, x_sample, state_conv, cache_k, cache_v, ln_mix_g, ln_mix_b, ln_ffn_g, ln_ffn_b, conv_w_in, conv_w_dw, conv_ln_g, conv_ln_b, conv_w_out, w_kv, attn_w_q, attn_sinks, attn_w_o, router_w_group, router_b_group, router_w_expert, router_b_expert, expert_w_gate, expert_w_up, expert_w_down):
    x = jnp.concatenate([x_prompt.reshape(T_PROMPT, D_MODEL), x_sample.reshape(T_SAMPLE, D_MODEL)], axis=0)
    wg = expert_w_gate.reshape(DEPTH * N_EXPERTS, D_MODEL, EXPERT_FF)
    wu = expert_w_up.reshape(DEPTH * N_EXPERTS, D_MODEL, EXPERT_FF)
    wd = expert_w_down.reshape(DEPTH * N_EXPERTS, EXPERT_FF, D_MODEL)
    row = lambda v: v.reshape(1, -1)

    def router_weights(layer):
        w = jnp.concatenate([router_w_group[layer], router_w_expert[layer]], axis=1)
        w = jnp.pad(w, ((0, 0), (0, LANES - w.shape[1])))
        w_hi = w.astype(BF16)
        w_lo = (w - w_hi.astype(F32)).astype(BF16)
        bias = jnp.concatenate([router_b_group[layer], router_b_expert[layer].reshape(-1)])
        return w_hi, w_lo, jnp.pad(bias, (0, LANES - bias.shape[0])).reshape(1, LANES)

    pos_p = jnp.arange(SEQ, dtype=jnp.int32)
    pos_s = PAST_LEN + jnp.arange(DEC_SEQ, dtype=jnp.int32)
    rope_p = _rope_tables(pos_p)
    rope_s1 = _rope_tables(pos_s)
    rope_s = tuple(jnp.tile(t, (TM // DEC_SEQ, 1)) for t in rope_s1)
    rope_all = tuple(jnp.concatenate([jnp.tile(p, (BATCH, 1)), jnp.tile(s, (DEC_BATCH, 1))], axis=0)
                     for p, s in zip(rope_p, rope_s1))

    tails, us = [], []
    kv_p = kv_s = None
    outs_kv = None
    for layer in range(DEPTH):
        post_w = (row(ln_mix_g[layer]), row(ln_mix_b[layer])) + router_weights(layer)
        if layer < N_A_LAYERS:
            st = jnp.pad(state_conv[layer], ((0, 0), (STATE_PAD - (CONV_WIDTH - 1), 0), (0, 0)))
            wdw = jnp.pad(conv_w_dw[layer], ((0, STATE_PAD - CONV_WIDTH), (0, 0)))
            wdw = wdw.reshape(STATE_PAD, N_LANE_CHUNKS, LANES).transpose(1, 0, 2)
            w = (conv_w_in[layer].astype(BF16), wdw, row(conv_ln_g[layer]), row(conv_ln_b[layer]),
                 conv_w_out[layer].astype(BF16)) + post_w
            x1, info, cnt, tail, u_s = _conv_mixer(x, st, w)
            tails.append(tail[:, STATE_PAD - (CONV_WIDTH - 1):, :])
            us.append(u_s.reshape(DEC_BATCH, DEC_SEQ, D_MODEL)[:, DEC_SEQ - (CONV_WIDTH - 1):, :])
        else:
            j = layer - N_A_LAYERS
            x1, info, cnt = _attn_mixer(x, attn_sinks[j], attn_w_q[j].astype(BF16), attn_w_o[j].astype(BF16),
                                        post_w, rope_p, rope_s, kv_p, kv_s)
        x = _moe_layer(x1, info, cnt, wg, wu, wd, layer, row(ln_ffn_g[layer]), row(ln_ffn_b[layer]))
        if layer == N_A_LAYERS - 1:
            k, v = _kv_proj(x, w_kv.astype(BF16), rope_all)
            k_p = k[:T_PROMPT].reshape(BATCH, SEQ, KV_WIDTH)
            v_p = v[:T_PROMPT].reshape(BATCH, SEQ, KV_WIDTH)
            k_s = jnp.concatenate([cache_k.reshape(DEC_BATCH, WINDOW, KV_WIDTH),
                                   k[T_PROMPT:].reshape(DEC_BATCH, DEC_SEQ, KV_WIDTH)], axis=1)
            v_s = jnp.concatenate([cache_v.reshape(DEC_BATCH, WINDOW, KV_WIDTH),
                                   v[T_PROMPT:].reshape(DEC_BATCH, DEC_SEQ, KV_WIDTH)], axis=1)
            cache_shape = (-1, WINDOW, N_KV_HEADS, HEAD_DIM)
            outs_kv = (k_p[:, -WINDOW:].reshape(cache_shape), v_p[:, -WINDOW:].reshape(cache_shape),
                       k_s[:, -WINDOW:].reshape(cache_shape), v_s[:, -WINDOW:].reshape(cache_shape))
            front = ((0, 0), (WINDOW, 0), (0, 0))
            back = ((0, 0), (0, KEY_WIN - WINDOW - DEC_SEQ), (0, 0))
            kv_p = (_dup_heads(jnp.pad(k_p, front)), _dup_heads(jnp.pad(v_p, front)))
            kv_s = (_dup_heads(jnp.pad(k_s, back)), _dup_heads(jnp.pad(v_s, back)))
    y_prompt = x[:T_PROMPT].reshape(BATCH, SEQ, D_MODEL)
    y_sample = x[T_PROMPT:].reshape(DEC_BATCH, DEC_SEQ, D_MODEL)
    return (y_prompt, y_sample, jnp.stack(tails, axis=0), jnp.stack(us, axis=0)) + outs_kv
```

```python
import functools

import jax
import jax.numpy as jnp
from jax import lax
from jax.experimental import pallas as pl
from jax.experimental.pallas import tpu as pltpu

D_MODEL = 1024
BATCH = 2
SEQ = 8192
DEPTH = 4
DEC_BATCH = 32
DEC_SEQ = 32
PAST_LEN = 2048
CHUNK = 64
N_A_LAYERS = DEPTH // 2
CONV_WIDTH = 31
N_HEADS = 16
N_KV_HEADS = 2
HEAD_DIM = 64
KV_WIDTH = N_KV_HEADS * HEAD_DIM
WINDOW = 128
ROT_DIM = HEAD_DIM // 4
ROPE_THETA = 500000.0
N_GROUPS = 4
EXPERTS_PER_GROUP = 8
N_EXPERTS = N_GROUPS * EXPERTS_PER_GROUP
EXPERT_FF = D_MODEL // 2
DEEPNORM_ALPHA = (2 * DEPTH) ** 0.25
LN_EPS = 1e-5
ATTN_SCALE = HEAD_DIM ** -0.5
NEG_INF = -1e30

LANES = 128
N_LANE_CHUNKS = D_MODEL // LANES
T_PROMPT = BATCH * SEQ
T_SAMPLE = DEC_BATCH * DEC_SEQ
T_ALL = T_PROMPT + T_SAMPLE
TM = 256
STATE_PAD = 32
KEY_WIN = 256
UQ_PROMPT = 2 * CHUNK
MOE_TM = 256
MOE_ROWS = 2 * T_ALL + N_EXPERTS * MOE_TM
MOE_TILES = MOE_ROWS // MOE_TM
VMEM_LIMIT = 56 * 1024 * 1024

F32 = jnp.float32
BF16 = jnp.bfloat16


def _layer_norm(x, g, b):
    mu = jnp.mean(x, axis=-1, keepdims=True)
    xc = x - mu
    var = jnp.mean(xc * xc, axis=-1, keepdims=True)
    return xc * lax.rsqrt(var + LN_EPS) * g + b


def _router(x1, rwh_ref, rwl_ref, rb_ref, cnt_ref):
    x_hi = x1.astype(BF16)
    x_lo = (x1 - x_hi.astype(F32)).astype(BF16)
    w_hi = rwh_ref[...]
    logits = (jnp.dot(x_hi, w_hi, preferred_element_type=F32)
              + jnp.dot(x_hi, rwl_ref[...], preferred_element_type=F32)
              + jnp.dot(x_lo, w_hi, preferred_element_type=F32)) + rb_ref[...]
    lane_i = lax.broadcasted_iota(jnp.int32, logits.shape, 1)
    lane = lane_i.astype(F32)
    neg = jnp.float32(-3.0e38)
    none = jnp.float32(LANES)
    gl = jnp.where(lane_i < N_GROUPS, logits, neg)
    gmax = jnp.max(gl, axis=-1, keepdims=True)
    g_idx = jnp.min(jnp.where(gl == gmax, lane, none), axis=-1, keepdims=True)
    gsum = jnp.sum(jnp.where(lane_i < N_GROUPS, jnp.exp(gl - gmax), 0.0), axis=-1, keepdims=True)
    p_grp = 1.0 / gsum
    lo = N_GROUPS + EXPERTS_PER_GROUP * g_idx
    el = jnp.where(lane >= lo, jnp.where(lane < lo + EXPERTS_PER_GROUP, logits, neg), neg)
    l1 = jnp.max(el, axis=-1, keepdims=True)
    i1 = jnp.min(jnp.where(el == l1, lane, none), axis=-1, keepdims=True)
    el2 = jnp.where(lane == i1, neg, el)
    l2 = jnp.max(el2, axis=-1, keepdims=True)
    i2 = jnp.min(jnp.where(el2 == l2, lane, none), axis=-1, keepdims=True)
    t = jnp.exp(l2 - l1)
    inv = p_grp / (1.0 + t)
    e1 = i1 - N_GROUPS
    e2 = i2 - N_GROUPS
    rows = logits.shape[0]
    onehot = jnp.where(lane == e1, 1.0, jnp.where(lane == e2, 1.0, 0.0))
    tri = (lax.broadcasted_iota(jnp.int32, (rows, rows), 0) > lax.broadcasted_iota(jnp.int32, (rows, rows), 1))
    before = jnp.dot(jnp.where(tri, 1.0, 0.0).astype(BF16), onehot.astype(BF16),
                     preferred_element_type=F32) + cnt_ref[...]
    r1 = jnp.sum(jnp.where(lane == e1, before, 0.0), axis=-1, keepdims=True)
    r2 = jnp.sum(jnp.where(lane == e2, before, 0.0), axis=-1, keepdims=True)
    cnt_ref[...] = cnt_ref[...] + jnp.sum(onehot, axis=0, keepdims=True)
    out = jnp.where(lane_i == 4, r1, jnp.where(lane_i == 5, r2, 0.0))
    out = jnp.where(lane_i == 2, inv, jnp.where(lane_i == 3, inv * t, out))
    return jnp.where(lane_i == 0, e1, jnp.where(lane_i == 1, e2, out))


def _post_mixer(x, mix, lg_ref, lb_ref, rwh_ref, rwl_ref, rb_ref, x1_ref, info_ref, cnt_out_ref, cnt_ref):
    x1 = _layer_norm(DEEPNORM_ALPHA * x + mix, lg_ref[...], lb_ref[...])
    x1_ref[...] = x1
    info_ref[...] = _router(x1, rwh_ref, rwl_ref, rb_ref, cnt_ref)
    cnt_out_ref[...] = cnt_ref[...]


def _init_counts(cnt_ref, first_step, cnt_in_ref):
    @pl.when(first_step)
    def _():
        cnt_ref[...] = jnp.zeros_like(cnt_ref) if cnt_in_ref is None else cnt_in_ref[...]


def _conv_mixer_body(x_ref, st_ref, win_ref, wdw_ref, cg_ref, cb_ref, wout_ref, post_w, cnt_in_ref,
                     x1_ref, info_ref, cnt_out_ref, u_ref, cnt_ref, ubuf, cbuf, *, nseg, seg):
    carry = st_ref is None
    first_step = (pl.program_id(0) == 0) & (pl.program_id(1) == 0) if carry else pl.program_id(0) == 0
    _init_counts(cnt_ref, first_step, cnt_in_ref)
    x = x_ref[...]
    h = jnp.dot(x.astype(BF16), win_ref[...], preferred_element_type=F32)
    u = h[:, :D_MODEL] * jax.nn.sigmoid(h[:, D_MODEL:])
    if carry:
        @pl.when(pl.program_id(1) == 0)
        def _():
            ubuf[:, 0, 0:STATE_PAD, :] = jnp.zeros((N_LANE_CHUNKS, STATE_PAD, LANES), F32)
        u_ref[0] = u[seg - STATE_PAD:, :]
    else:
        u_ref[...] = u
    for lc in range(N_LANE_CHUNKS):
        for s in range(nseg):
            ubuf[lc, s, STATE_PAD:STATE_PAD + seg, :] = u[s * seg:(s + 1) * seg, lc * LANES:(lc + 1) * LANES]
            if not carry:
                ubuf[lc, s, 0:STATE_PAD, :] = st_ref[s, :, lc * LANES:(lc + 1) * LANES]

    def conv_lane_chunk(lc, c):
        w = wdw_ref[lc]
        for s in range(nseg):
            acc = jnp.zeros((seg, LANES), F32)
            for k in range(CONV_WIDTH):
                off = k + STATE_PAD - (CONV_WIDTH - 1)
                acc = acc + w[k:k + 1, :] * ubuf[lc, s, off:off + seg, :]
            cbuf[lc, s * seg:(s + 1) * seg, :] = acc
        return c

    lax.fori_loop(0, N_LANE_CHUNKS, conv_lane_chunk, 0)
    if carry:
        for lc in range(N_LANE_CHUNKS):
            ubuf[lc, 0, 0:STATE_PAD, :] = u[seg - STATE_PAD:, lc * LANES:(lc + 1) * LANES]
    dw = jnp.concatenate([cbuf[lc] for lc in range(N_LANE_CHUNKS)], axis=1)
    sw = _layer_norm(dw, cg_ref[...], cb_ref[...])
    sw = sw * jax.nn.sigmoid(sw)
    mix = jnp.dot(sw.astype(BF16), wout_ref[...], preferred_element_type=F32)
    _post_mixer(x, mix, *post_w, x1_ref, info_ref, cnt_out_ref, cnt_ref)


N_POST_W = 5


def _conv_prompt_kernel(x_ref, win_ref, wdw_ref, cg_ref, cb_ref, wout_ref, *rest):
    post_w, rest = rest[:N_POST_W], rest[N_POST_W:]
    _conv_mixer_body(x_ref, None, win_ref, wdw_ref, cg_ref, cb_ref, wout_ref, post_w, None, *rest,
                     nseg=1, seg=TM)


def _conv_sample_kernel(x_ref, st_ref, win_ref, wdw_ref, cg_ref, cb_ref, wout_ref, *rest):
    post_w, cnt_in_ref, rest = rest[:N_POST_W], rest[N_POST_W], rest[N_POST_W + 3:]
    _conv_mixer_body(x_ref, st_ref, win_ref, wdw_ref, cg_ref, cb_ref, wout_ref, post_w, cnt_in_ref, *rest,
                     nseg=TM // DEC_SEQ, seg=DEC_SEQ)


def _full(shape):
    return pl.BlockSpec(shape, lambda *_: (0,) * len(shape))


ANY_SPEC = pl.BlockSpec(memory_space=pl.ANY)
CNT_SHAPE = jax.ShapeDtypeStruct((1, LANES), F32)
CNT_SCRATCH = pltpu.VMEM((1, LANES), F32)


def _conv_mixer(x_p, x_s, s_first, state_pad, w):
    w_specs = [_full(a.shape) for a in w]
    tiles_per_seq = SEQ // TM
    tok = lambda b, t: (b * tiles_per_seq + t, 0)
    x1, info, cnt, tail = pl.pallas_call(
        _conv_prompt_kernel,
        grid=(BATCH, tiles_per_seq),
        in_specs=[pl.BlockSpec((TM, D_MODEL), tok)] + w_specs,
        out_specs=[pl.BlockSpec((TM, D_MODEL), tok), pl.BlockSpec((TM, LANES), tok), _full(CNT_SHAPE.shape),
                   pl.BlockSpec((1, STATE_PAD, D_MODEL), lambda b, t: (b, 0, 0))],
        out_shape=[jax.ShapeDtypeStruct((T_ALL, D_MODEL), F32), jax.ShapeDtypeStruct((T_ALL, LANES), F32),
                   CNT_SHAPE, jax.ShapeDtypeStruct((BATCH, STATE_PAD, D_MODEL), F32)],
        scratch_shapes=[CNT_SCRATCH, pltpu.VMEM((N_LANE_CHUNKS, 1, STATE_PAD + TM, LANES), F32),
                        pltpu.VMEM((N_LANE_CHUNKS, TM, LANES), F32)],
        compiler_params=pltpu.CompilerParams(dimension_semantics=("arbitrary", "arbitrary"),
                                             vmem_limit_bytes=VMEM_LIMIT),
        name="conv_mixer_prompt",
    )(x_p, *w)
    nb = TM // DEC_SEQ
    first = T_PROMPT // TM
    stok = lambda i: (first + i, 0)
    n_in = 2 + len(w) + 1
    x1, info, cnt, u_s = pl.pallas_call(
        _conv_sample_kernel,
        grid=(T_SAMPLE // TM,),
        in_specs=[pl.BlockSpec((TM, D_MODEL), lambda i: (s_first + i, 0)),
                  pl.BlockSpec((nb, STATE_PAD, D_MODEL), lambda i: (i, 0, 0))]
                 + w_specs + [_full(CNT_SHAPE.shape), ANY_SPEC, ANY_SPEC],
        out_specs=[pl.BlockSpec((TM, D_MODEL), stok), pl.BlockSpec((TM, LANES), stok), _full(CNT_SHAPE.shape),
                   pl.BlockSpec((TM, D_MODEL), lambda i: (i, 0))],
        out_shape=[jax.ShapeDtypeStruct((T_ALL, D_MODEL), F32), jax.ShapeDtypeStruct((T_ALL, LANES), F32),
                   CNT_SHAPE, jax.ShapeDtypeStruct((T_SAMPLE, D_MODEL), F32)],
        scratch_shapes=[CNT_SCRATCH, pltpu.VMEM((N_LANE_CHUNKS, nb, STATE_PAD + DEC_SEQ, LANES), F32),
                        pltpu.VMEM((N_LANE_CHUNKS, TM, LANES), F32)],
        input_output_aliases={n_in: 0, n_in + 1: 1},
        compiler_params=pltpu.CompilerParams(dimension_semantics=("arbitrary",),
                                             vmem_limit_bytes=VMEM_LIMIT),
        name="conv_mixer_sample",
    )(x_s, state_pad, *w, cnt, x1, info)
    return x1, info, cnt, tail, u_s


def _rope_lanes(v, c, sa, sb):
    width = v.shape[-1]
    half = ROT_DIM // 2
    return v * c + pltpu.roll(v, width - half, 1) * sa + pltpu.roll(v, half, 1) * sb


def _attn_mixer_body(sink_ref, x_ref, wq_ref, c_ref, sa_ref, sb_ref, kd_ref, vd_ref, wo_ref, post_w, cnt_in_ref,
                     x1_ref, info_ref, cnt_out_ref, cnt_ref, obuf, *, nunit, uq, prompt):
    first_step = (pl.program_id(0) == 0) & (pl.program_id(1) == 0) if prompt else pl.program_id(0) == 0
    _init_counts(cnt_ref, first_step, cnt_in_ref)
    x = x_ref[...]
    q = jnp.dot(x.astype(BF16), wq_ref[...], preferred_element_type=F32)
    rep = D_MODEL // LANES
    q = _rope_lanes(q, jnp.tile(c_ref[...], (1, rep)), jnp.tile(sa_ref[...], (1, rep)),
                    jnp.tile(sb_ref[...], (1, rep)))
    qb = (q * ATTN_SCALE).astype(BF16)
    pairs = N_HEADS // N_KV_HEADS // 2
    rows = pairs * uq
    lane_k = lax.broadcasted_iota(jnp.int32, (KEY_WIN, LANES), 1)
    col = lax.broadcasted_iota(jnp.int32, (rows, 2 * KEY_WIN), 1) & (KEY_WIN - 1)
    if prompt:
        chunk_shift = CHUNK.bit_length() - 1
        qchunk = (lax.broadcasted_iota(jnp.int32, (rows, 2 * KEY_WIN), 0) & (uq - 1)) >> chunk_shift
        kchunk = col >> chunk_shift
        band = (kchunk >= qchunk) & (kchunk <= qchunk + WINDOW // CHUNK)
    lane_o = lax.broadcasted_iota(jnp.int32, (uq, LANES), 1)
    for un in range(nunit):
        r0 = un * uq
        if prompt:
            kstart = pl.multiple_of(pl.program_id(1) * (nunit * uq) + r0, LANES)
            mask = band & (col + kstart >= WINDOW)
        else:
            mask = col < WINDOW + DEC_SEQ
        for kvh in range(N_KV_HEADS):
            if prompt:
                kw = kd_ref[0, kvh, pl.ds(kstart, KEY_WIN), :]
                vw = vd_ref[0, kvh, pl.ds(kstart, KEY_WIN), :]
            else:
                kw = kd_ref[un, kvh]
                vw = vd_ref[un, kvh]
            zero = jnp.zeros_like(kw)
            kbd = jnp.concatenate([jnp.where(lane_k < HEAD_DIM, kw, zero),
                                   jnp.where(lane_k >= HEAD_DIM, kw, zero)], axis=0)
            vbd = jnp.concatenate([jnp.where(lane_k < HEAD_DIM, vw, zero),
                                   jnp.where(lane_k >= HEAD_DIM, vw, zero)], axis=0)
            q4 = jnp.concatenate([qb[r0:r0 + uq, (kvh * pairs + p) * LANES:(kvh * pairs + p + 1) * LANES]
                                  for p in range(pairs)], axis=0)
            s = lax.dot_general(q4, kbd, (((1,), (1,)), ((), ())), preferred_element_type=F32)
            s = jnp.where(mask, s, NEG_INF)
            p_parts, inv_parts = [], []
            for p in range(pairs):
                halves, invs = [], []
                for hf in range(2):
                    sk = sink_ref[(kvh * pairs + p) * 2 + hf]
                    sh = s[p * uq:(p + 1) * uq, hf * KEY_WIN:(hf + 1) * KEY_WIN]
                    m = jnp.maximum(jnp.max(sh, axis=-1, keepdims=True), sk)
                    pe = jnp.exp(sh - m)
                    den = jnp.sum(pe, axis=-1, keepdims=True) + jnp.exp(sk - m)
                    halves.append(pe.astype(BF16))
                    invs.append(1.0 / den)
                p_parts.append(jnp.concatenate(halves, axis=1))
                inv_parts.append(invs)
            pm = jnp.concatenate(p_parts, axis=0)
            o4 = jnp.dot(pm, vbd, preferred_element_type=F32)
            for p in range(pairs):
                inv = jnp.where(lane_o < HEAD_DIM, inv_parts[p][0], inv_parts[p][1])
                hp = kvh * pairs + p
                obuf[r0:r0 + uq, hp * LANES:(hp + 1) * LANES] = (o4[p * uq:(p + 1) * uq, :] * inv).astype(BF16)
    mix = jnp.dot(obuf[...], wo_ref[...], preferred_element_type=F32)
    _post_mixer(x, mix, *post_w, x1_ref, info_ref, cnt_out_ref, cnt_ref)


N_ATTN_IN = 9


def _attn_prompt_kernel(*refs):
    ins, post_w, rest = refs[:N_ATTN_IN], refs[N_ATTN_IN:N_ATTN_IN + N_POST_W], refs[N_ATTN_IN + N_POST_W:]
    _attn_mixer_body(*ins, post_w, None, *rest, nunit=TM // UQ_PROMPT, uq=UQ_PROMPT, prompt=True)


def _attn_sample_kernel(*refs):
    ins, post_w = refs[:N_ATTN_IN], refs[N_ATTN_IN:N_ATTN_IN + N_POST_W]
    cnt_in_ref, rest = refs[N_ATTN_IN + N_POST_W], refs[N_ATTN_IN + N_POST_W + 3:]
    _attn_mixer_body(*ins, post_w, cnt_in_ref, *rest, nunit=TM // DEC_SEQ, uq=DEC_SEQ, prompt=False)


def _attn_mixer(x, sinks, wq, wo, post_w, rope_p, rope_s, kv_p, kv_s):
    tiles_per_seq = SEQ // TM
    smem = pl.BlockSpec(memory_space=pltpu.SMEM)
    tail_specs = [_full(wo.shape)] + [_full(a.shape) for a in post_w]
    kd, vd = kv_p
    tok = lambda b, t: (b * tiles_per_seq + t, 0)
    x1, info, cnt = pl.pallas_call(
        _attn_prompt_kernel,
        grid=(BATCH, tiles_per_seq),
        in_specs=[smem, pl.BlockSpec((TM, D_MODEL), tok), _full(wq.shape)]
                 + [pl.BlockSpec((TM, LANES), lambda b, t: (t, 0))] * 3
                 + [pl.BlockSpec((1,) + kd.shape[1:], lambda b, t: (b, 0, 0, 0))] * 2 + tail_specs,
        out_specs=[pl.BlockSpec((TM, D_MODEL), tok), pl.BlockSpec((TM, LANES), tok), _full(CNT_SHAPE.shape)],
        out_shape=[jax.ShapeDtypeStruct((T_ALL, D_MODEL), F32), jax.ShapeDtypeStruct((T_ALL, LANES), F32),
                   CNT_SHAPE],
        scratch_shapes=[CNT_SCRATCH, pltpu.VMEM((TM, D_MODEL), BF16)],
        compiler_params=pltpu.CompilerParams(dimension_semantics=("arbitrary", "arbitrary"),
                                             vmem_limit_bytes=VMEM_LIMIT),
        name="attn_mixer_prompt",
    )(sinks, x, wq, *rope_p, kd, vd, wo, *post_w)
    nb = TM // DEC_SEQ
    first = T_PROMPT // TM
    stok = lambda i: (first + i, 0)
    kd, vd = kv_s
    n_in = N_ATTN_IN + N_POST_W + 1
    x1, info, cnt = pl.pallas_call(
        _attn_sample_kernel,
        grid=(T_SAMPLE // TM,),
        in_specs=[smem, pl.BlockSpec((TM, D_MODEL), stok), _full(wq.shape)] + [_full((TM, LANES))] * 3
                 + [pl.BlockSpec((nb,) + kd.shape[1:], lambda i: (i, 0, 0, 0))] * 2 + tail_specs
                 + [_full(CNT_SHAPE.shape), ANY_SPEC, ANY_SPEC],
        out_specs=[pl.BlockSpec((TM, D_MODEL), stok), pl.BlockSpec((TM, LANES), stok), _full(CNT_SHAPE.shape)],
        out_shape=[jax.ShapeDtypeStruct((T_ALL, D_MODEL), F32), jax.ShapeDtypeStruct((T_ALL, LANES), F32),
                   CNT_SHAPE],
        scratch_shapes=[CNT_SCRATCH, pltpu.VMEM((TM, D_MODEL), BF16)],
        input_output_aliases={n_in: 0, n_in + 1: 1},
        compiler_params=pltpu.CompilerParams(dimension_semantics=("arbitrary",),
                                             vmem_limit_bytes=VMEM_LIMIT),
        name="attn_mixer_sample",
    )(sinks, x, wq, *rope_s, kd, vd, wo, *post_w, cnt, x1, info)
    return x1, info, cnt


def _kv_kernel(x_ref, wkv_ref, c_ref, sa_ref, sb_ref, k_ref, v_ref):
    kv = jnp.dot(x_ref[...].astype(BF16), wkv_ref[...], preferred_element_type=F32)
    k_ref[...] = _rope_lanes(kv[:, :KV_WIDTH], c_ref[...], sa_ref[...], sb_ref[...])
    v_ref[...] = kv[:, KV_WIDTH:]


def _kv_proj(x, wkv, rope_all):
    tok = lambda i: (i, 0)
    return pl.pallas_call(
        _kv_kernel,
        grid=(T_ALL // TM,),
        in_specs=[pl.BlockSpec((TM, D_MODEL), tok), _full(wkv.shape)] + [pl.BlockSpec((TM, LANES), tok)] * 3,
        out_specs=[pl.BlockSpec((TM, KV_WIDTH), tok)] * 2,
        out_shape=[jax.ShapeDtypeStruct((T_ALL, KV_WIDTH), F32)] * 2,
        compiler_params=pltpu.CompilerParams(dimension_semantics=("arbitrary",)),
        name="kv_proj",
    )(x, wkv, *rope_all)


N_TILES = T_ALL // TM


def _row_copy_wait(src_rows, dst_rows, sem):
    pltpu.make_async_copy(src_rows, dst_rows, sem).wait()


def _dispatch_kernel(zrow_ref, zflag_ref, pos_ref, x1_ref, xs_hbm, zbuf, sem, zsem):
    i = pl.program_id(0)

    @pl.when(i == 0)
    def _():
        zbuf[...] = jnp.zeros_like(zbuf)

        def zero_tile(e):
            return pltpu.make_async_copy(zbuf, xs_hbm.at[pl.ds(pl.multiple_of(zrow_ref[e], MOE_TM), MOE_TM)], zsem)

        for e in range(N_EXPERTS):
            @pl.when(zflag_ref[e] > 0)
            def _():
                zero_tile(e).start()
        for e in range(N_EXPERTS):
            @pl.when(zflag_ref[e] > 0)
            def _():
                zero_tile(e).wait()

    for j in range(TM):
        for slot in range(2):
            pltpu.make_async_copy(x1_ref.at[pl.ds(j, 1)], xs_hbm.at[pl.ds(pos_ref[0, 0, 2 * j + slot], 1)],
                                  sem).start()
    for slot in range(2):
        _row_copy_wait(x1_ref, xs_hbm.at[pl.ds(0, TM)], sem)


def _dispatch(x1, pos, zrow, zflag):
    return pl.pallas_call(
        _dispatch_kernel,
        grid_spec=pltpu.PrefetchScalarGridSpec(
            num_scalar_prefetch=2,
            grid=(N_TILES,),
            in_specs=[pl.BlockSpec((1, 1, 2 * TM), lambda i, zr, zf: (i, 0, 0), memory_space=pltpu.SMEM),
                      pl.BlockSpec((TM, D_MODEL), lambda i, zr, zf: (i, 0))],
            out_specs=ANY_SPEC,
            scratch_shapes=[pltpu.VMEM((MOE_TM, D_MODEL), F32), pltpu.SemaphoreType.DMA,
                            pltpu.SemaphoreType.DMA],
        ),
        out_shape=jax.ShapeDtypeStruct((MOE_ROWS, D_MODEL), F32),
        compiler_params=pltpu.CompilerParams(dimension_semantics=("arbitrary",)),
        name="moe_dispatch",
    )(zrow, zflag, pos, x1)


def _moe_kernel(te_ref, slot_ref, next_ref, nv_ref, xs_ref, wg_hbm, wu_hbm, wd_hbm, ys_ref,
                wg_st, wu_st, wd_st, wgu_bf, wd_bf, sem, *, base):
    i = pl.program_id(0)

    def fetch(e, slot):
        return (pltpu.make_async_copy(wg_hbm.at[base + e], wg_st.at[slot], sem.at[slot, 0]),
                pltpu.make_async_copy(wu_hbm.at[base + e], wu_st.at[slot], sem.at[slot, 1]),
                pltpu.make_async_copy(wd_hbm.at[base + e], wd_st.at[slot], sem.at[slot, 2]))

    @pl.when(i == 0)
    def _():
        for c in fetch(te_ref[0], slot_ref[0]):
            c.start()

    @pl.when(i < nv_ref[0])
    def _():
        e = te_ref[i]
        slot = slot_ref[i]

        @pl.when((i == 0) | (e != te_ref[jnp.maximum(i - 1, 0)]))
        def _():
            for c in fetch(e, slot):
                c.wait()
            wgu_bf[:, :EXPERT_FF] = wg_st[slot].astype(BF16)
            wgu_bf[:, EXPERT_FF:] = wu_st[slot].astype(BF16)
            wd_bf[...] = wd_st[slot].astype(BF16)

            @pl.when(next_ref[i] >= 0)
            def _():
                for c in fetch(next_ref[i], 1 - slot):
                    c.start()

        hgu = jnp.dot(xs_ref[...].astype(BF16), wgu_bf[...], preferred_element_type=F32)
        hg = hgu[:, :EXPERT_FF]
        h = hg * jax.nn.sigmoid(hg) * hgu[:, EXPERT_FF:]
        ys_ref[...] = jnp.dot(h.astype(BF16), wd_bf[...], preferred_element_type=F32)


def _moe_experts(xs, tile_expert, tile_slot, tile_next, n_valid, wg, wu, wd, layer):
    row = lambda i, te, sl, nx, nv: (jnp.minimum(i, nv[0] - 1), 0)
    return pl.pallas_call(
        functools.partial(_moe_kernel, base=layer * N_EXPERTS),
        grid_spec=pltpu.PrefetchScalarGridSpec(
            num_scalar_prefetch=4,
            grid=(MOE_TILES,),
            in_specs=[pl.BlockSpec((MOE_TM, D_MODEL), row), ANY_SPEC, ANY_SPEC, ANY_SPEC],
            out_specs=pl.BlockSpec((MOE_TM, D_MODEL), row),
            scratch_shapes=[pltpu.VMEM((2, D_MODEL, EXPERT_FF), F32), pltpu.VMEM((2, D_MODEL, EXPERT_FF), F32),
                            pltpu.VMEM((2, EXPERT_FF, D_MODEL), F32),
                            pltpu.VMEM((D_MODEL, 2 * EXPERT_FF), BF16), pltpu.VMEM((EXPERT_FF, D_MODEL), BF16),
                            pltpu.SemaphoreType.DMA((2, 3))],
        ),
        out_shape=jax.ShapeDtypeStruct((MOE_ROWS, D_MODEL), F32),
        compiler_params=pltpu.CompilerParams(dimension_semantics=("arbitrary",), vmem_limit_bytes=VMEM_LIMIT),
        name="moe_experts",
    )(tile_expert, tile_slot, tile_next, n_valid, xs, wg, wu, wd)


PROMPT_TILES = T_PROMPT // TM


def _ffn_ln_kernel(pos_ref, pos_next_ref, x1_ref, info_ref, g_ref, b_ref, ys_hbm, *rest, split):
    out_refs, (ybuf, sem) = rest[:-2], rest[-2:]
    i = pl.program_id(0)

    def issue(p_ref, par):
        for j in range(TM):
            for slot in range(2):
                pltpu.make_async_copy(ys_hbm.at[pl.ds(p_ref[0, 0, 2 * j + slot], 1)],
                                      ybuf.at[par, slot, pl.ds(j, 1)], sem.at[par]).start()

    @pl.when(i == 0)
    def _():
        issue(pos_ref, 0)

    @pl.when(i + 1 < N_TILES)
    def _():
        issue(pos_next_ref, (i + 1) % 2)

    par = i % 2
    for slot in range(2):
        _row_copy_wait(ys_hbm.at[pl.ds(0, TM)], ybuf.at[par, slot], sem.at[par])
    info = info_ref[...]
    f = info[:, 2:3] * ybuf[par, 0] + info[:, 3:4] * ybuf[par, 1]
    x2 = _layer_norm(DEEPNORM_ALPHA * x1_ref[...] + f, g_ref[...], b_ref[...])
    if split:
        @pl.when(i < PROMPT_TILES)
        def _():
            out_refs[0][...] = x2

        @pl.when(i >= PROMPT_TILES)
        def _():
            out_refs[1][...] = x2
    else:
        out_refs[0][...] = x2


def _ffn_ln(x1, ys, pos, info, g, b, split):
    tok = lambda i: (i, 0)
    pos_spec = lambda fn: pl.BlockSpec((1, 1, 2 * TM), fn, memory_space=pltpu.SMEM)
    if split:
        out_specs = [pl.BlockSpec((TM, D_MODEL), lambda i: (jnp.minimum(i, PROMPT_TILES - 1), 0)),
                     pl.BlockSpec((TM, D_MODEL), lambda i: (jnp.maximum(i - PROMPT_TILES, 0), 0))]
        out_shape = [jax.ShapeDtypeStruct((T_PROMPT, D_MODEL), F32), jax.ShapeDtypeStruct((T_SAMPLE, D_MODEL), F32)]
    else:
        out_specs = pl.BlockSpec((TM, D_MODEL), tok)
        out_shape = jax.ShapeDtypeStruct((T_ALL, D_MODEL), F32)
    return pl.pallas_call(
        functools.partial(_ffn_ln_kernel, split=split),
        grid=(N_TILES,),
        in_specs=[pos_spec(lambda i: (i, 0, 0)), pos_spec(lambda i: (jnp.minimum(i + 1, N_TILES - 1), 0, 0)),
                  pl.BlockSpec((TM, D_MODEL), tok), pl.BlockSpec((TM, LANES), tok), _full(g.shape), _full(b.shape),
                  ANY_SPEC],
        out_specs=out_specs,
        out_shape=out_shape,
        scratch_shapes=[pltpu.VMEM((2, 2, TM, D_MODEL), F32), pltpu.SemaphoreType.DMA((2,))],
        compiler_params=pltpu.CompilerParams(dimension_semantics=("arbitrary",)),
        name="ffn_ln",
    )(pos, pos, x1, info, g, b, ys)


def _moe_layer(x1, info, cnt, wg, wu, wd, layer, g, b, split):
    counts = cnt[0, :N_EXPERTS].astype(jnp.int32)
    tiles_per = (counts + MOE_TM - 1) // MOE_TM
    tile_end = jnp.cumsum(tiles_per)
    pad_start = (tile_end - tiles_per) * MOE_TM
    expert = info[:, 0:2].astype(jnp.int32)
    rank = info[:, 4:6].astype(jnp.int32)
    ids = jnp.arange(N_EXPERTS, dtype=jnp.int32)
    start = jnp.sum(jnp.where(expert[:, :, None] == ids[None, None, :], pad_start[None, None, :], 0), axis=-1)
    pos = (start + rank).reshape(N_TILES, 1, 2 * TM)
    n_valid = tile_end[-1:]
    tile_ids = jnp.minimum(jnp.arange(MOE_TILES, dtype=jnp.int32), n_valid[0] - 1)
    tile_expert = jnp.sum(tile_ids[:, None] >= tile_end[None, :], axis=1, dtype=jnp.int32)
    nonempty = tiles_per > 0
    slot_of = (jnp.cumsum(nonempty.astype(jnp.int32)) - 1) % 2
    later = jnp.where(nonempty[None, :] & (ids[None, :] > ids[:, None]), ids[None, :], N_EXPERTS)
    next_of = jnp.min(later, axis=1)
    next_of = jnp.where(next_of < N_EXPERTS, next_of, -1)
    zrow = jnp.maximum(tile_end - 1, 0) * MOE_TM
    xs = _dispatch(x1, pos, zrow, tiles_per)
    ys = _moe_experts(xs, tile_expert, slot_of[tile_expert], next_of[tile_expert], n_valid, wg, wu, wd, layer)
    return _ffn_ln(x1, ys, pos, info, g, b, split)


def _rope_tables(pos):
    half = ROT_DIM // 2
    inv_freq = ROPE_THETA ** (-jnp.arange(0, ROT_DIM, 2, dtype=F32) / ROT_DIM)
    ang = pos.astype(F32)[:, None] * inv_freq[None, :]
    cos, sin = jnp.cos(ang), jnp.sin(ang)
    n = pos.shape[0]
    ones = jnp.ones((n, HEAD_DIM - ROT_DIM), F32)
    zeros_h = jnp.zeros((n, half), F32)
    zeros_r = jnp.zeros((n, HEAD_DIM - ROT_DIM), F32)
    c = jnp.concatenate([cos, cos, ones], axis=1)
    sa = jnp.concatenate([-sin, zeros_h, zeros_r], axis=1)
    sb = jnp.concatenate([zeros_h, sin, zeros_r], axis=1)
    rep = LANES // HEAD_DIM
    return tuple(jnp.tile(t, (1, rep)) for t in (c, sa, sb))


def _dup_heads(kv):
    b, l, _ = kv.shape
    h = kv.reshape(b, l, N_KV_HEADS, HEAD_DIM).transpose(0, 2, 1, 3)
    return jnp.concatenate([h, h], axis=-1).astype(BF16)


def kernel(x_prompt, x_sample, state_conv, cache_k, cache_v, ln_mix_g, ln_mix_b, ln_ffn_g, ln_ffn_b, conv_w_in, conv_w_dw, conv_ln_g, conv_ln_b, conv_w_out, w_kv, attn_w_q, attn_sinks, attn_w_o, router_w_group, router_b_group, router_w_expert, router_b_expert, expert_w_gate, expert_w_up, expert_w_down):
    x = None
    wg = expert_w_gate.reshape(DEPTH * N_EXPERTS, D_MODEL, EXPERT_FF)
    wu = expert_w_up.reshape(DEPTH * N_EXPERTS, D_MODEL, EXPERT_FF)
    wd = expert_w_down.reshape(DEPTH * N_EXPERTS, EXPERT_FF, D_MODEL)
    row = lambda v: v.reshape(1, -1)

    def router_weights(layer):
        w = jnp.concatenate([router_w_group[layer], router_w_expert[layer]], axis=1)
        w = jnp.pad(w, ((0, 0), (0, LANES - w.shape[1])))
        w_hi = w.astype(BF16)
        w_lo = (w - w_hi.astype(F32)).astype(BF16)
        bias = jnp.concatenate([router_b_group[layer], router_b_expert[layer].reshape(-1)])
        return w_hi, w_lo, jnp.pad(bias, (0, LANES - bias.shape[0])).reshape(1, LANES)

    pos_p = jnp.arange(SEQ, dtype=jnp.int32)
    pos_s = PAST_LEN + jnp.arange(DEC_SEQ, dtype=jnp.int32)
    rope_p = _rope_tables(pos_p)
    rope_s1 = _rope_tables(pos_s)
    rope_s = tuple(jnp.tile(t, (TM // DEC_SEQ, 1)) for t in rope_s1)
    rope_all = tuple(jnp.concatenate([jnp.tile(p, (BATCH, 1)), jnp.tile(s, (DEC_BATCH, 1))], axis=0)
                     for p, s in zip(rope_p, rope_s1))

    tails, us = [], []
    kv_p = kv_s = None
    outs_kv = None
    for layer in range(DEPTH):
        post_w = (row(ln_mix_g[layer]), row(ln_mix_b[layer])) + router_weights(layer)
        if layer < N_A_LAYERS:
            st = jnp.pad(state_conv[layer], ((0, 0), (STATE_PAD - (CONV_WIDTH - 1), 0), (0, 0)))
            wdw = jnp.pad(conv_w_dw[layer], ((0, STATE_PAD - CONV_WIDTH), (0, 0)))
            wdw = wdw.reshape(STATE_PAD, N_LANE_CHUNKS, LANES).transpose(1, 0, 2)
            w = (conv_w_in[layer].astype(BF16), wdw, row(conv_ln_g[layer]), row(conv_ln_b[layer]),
                 conv_w_out[layer].astype(BF16)) + post_w
            if layer == 0:
                x_in = (x_prompt.reshape(T_PROMPT, D_MODEL), x_sample.reshape(T_SAMPLE, D_MODEL), 0)
            else:
                x_in = (x, x, PROMPT_TILES)
            x1, info, cnt, tail, u_s = _conv_mixer(*x_in, st, w)
            tails.append(tail[:, STATE_PAD - (CONV_WIDTH - 1):, :])
            us.append(u_s.reshape(DEC_BATCH, DEC_SEQ, D_MODEL)[:, DEC_SEQ - (CONV_WIDTH - 1):, :])
        else:
            j = layer - N_A_LAYERS
            x1, info, cnt = _attn_mixer(x, attn_sinks[j], attn_w_q[j].astype(BF16), attn_w_o[j].astype(BF16),
                                        post_w, rope_p, rope_s, kv_p, kv_s)
        x = _moe_layer(x1, info, cnt, wg, wu, wd, layer, row(ln_ffn_g[layer]), row(ln_ffn_b[layer]),
                       split=layer == DEPTH - 1)
        if layer == N_A_LAYERS - 1:
            k, v = _kv_proj(x, w_kv.astype(BF16), rope_all)
            k_p = k[:T_PROMPT].reshape(BATCH, SEQ, KV_WIDTH)
            v_p = v[:T_PROMPT].reshape(BATCH, SEQ, KV_WIDTH)
            k_s = jnp.concatenate([cache_k.reshape(DEC_BATCH, WINDOW, KV_WIDTH),
                                   k[T_PROMPT:].reshape(DEC_BATCH, DEC_SEQ, KV_WIDTH)], axis=1)
            v_s = jnp.concatenate([cache_v.reshape(DEC_BATCH, WINDOW, KV_WIDTH),
                                   v[T_PROMPT:].reshape(DEC_BATCH, DEC_SEQ, KV_WIDTH)], axis=1)
            cache_shape = (-1, WINDOW, N_KV_HEADS, HEAD_DIM)
            outs_kv = (k_p[:, -WINDOW:].reshape(cache_shape), v_p[:, -WINDOW:].reshape(cache_shape),
                       k_s[:, -WINDOW:].reshape(cache_shape), v_s[:, -WINDOW:].reshape(cache_shape))
            front = ((0, 0), (WINDOW, 0), (0, 0))
            back = ((0, 0), (0, KEY_WIN - WINDOW - DEC_SEQ), (0, 0))
            kv_p = (_dup_heads(jnp.pad(k_p, front)), _dup_heads(jnp.pad(v_p, front)))
            kv_s = (_dup_heads(jnp.pad(k_s, back)), _dup_heads(jnp.pad(v_s, back)))
    y_prompt = x[0].reshape(BATCH, SEQ, D_MODEL)
    y_sample = x[1].reshape(DEC_BATCH, DEC_SEQ, D_MODEL)
    return (y_prompt, y_sample, jnp.stack(tails, axis=0), jnp.stack(us, axis=0)) + outs_kv
```

```python
import functools

import jax
import jax.numpy as jnp
from jax import lax
from jax.experimental import pallas as pl
from jax.experimental.pallas import tpu as pltpu

D_MODEL = 1024
BATCH = 2
SEQ = 8192
DEPTH = 4
DEC_BATCH = 32
DEC_SEQ = 32
PAST_LEN = 2048
CHUNK = 64
N_A_LAYERS = DEPTH // 2
CONV_WIDTH = 31
N_HEADS = 16
N_KV_HEADS = 2
HEAD_DIM = 64
KV_WIDTH = N_KV_HEADS * HEAD_DIM
WINDOW = 128
ROT_DIM = HEAD_DIM // 4
ROPE_THETA = 500000.0
N_GROUPS = 4
EXPERTS_PER_GROUP = 8
N_EXPERTS = N_GROUPS * EXPERTS_PER_GROUP
EXPERT_FF = D_MODEL // 2
DEEPNORM_ALPHA = (2 * DEPTH) ** 0.25
LN_EPS = 1e-5
ATTN_SCALE = HEAD_DIM ** -0.5
NEG_INF = -1e30

LANES = 128
N_LANE_CHUNKS = D_MODEL // LANES
T_PROMPT = BATCH * SEQ
T_SAMPLE = DEC_BATCH * DEC_SEQ
T_ALL = T_PROMPT + T_SAMPLE
TM = 256
STATE_PAD = 32
KEY_WIN = 256
UQ_PROMPT = 2 * CHUNK
MOE_TM = 256
MOE_ROWS = 2 * T_ALL + N_EXPERTS * MOE_TM
MOE_TILES = MOE_ROWS // MOE_TM
VMEM_LIMIT = 56 * 1024 * 1024

F32 = jnp.float32
BF16 = jnp.bfloat16


def _layer_norm(x, g, b):
    mu = jnp.mean(x, axis=-1, keepdims=True)
    xc = x - mu
    var = jnp.mean(xc * xc, axis=-1, keepdims=True)
    return xc * lax.rsqrt(var + LN_EPS) * g + b


def _router(x1, rwh_ref, rwl_ref, rb_ref, cnt_ref):
    x_hi = x1.astype(BF16)
    x_lo = (x1 - x_hi.astype(F32)).astype(BF16)
    w_hi = rwh_ref[...]
    logits = (jnp.dot(x_hi, w_hi, preferred_element_type=F32)
              + jnp.dot(x_hi, rwl_ref[...], preferred_element_type=F32)
              + jnp.dot(x_lo, w_hi, preferred_element_type=F32)) + rb_ref[...]
    lane_i = lax.broadcasted_iota(jnp.int32, logits.shape, 1)
    lane = lane_i.astype(F32)
    neg = jnp.float32(-3.0e38)
    none = jnp.float32(LANES)
    gl = jnp.where(lane_i < N_GROUPS, logits, neg)
    gmax = jnp.max(gl, axis=-1, keepdims=True)
    g_idx = jnp.min(jnp.where(gl == gmax, lane, none), axis=-1, keepdims=True)
    gsum = jnp.sum(jnp.where(lane_i < N_GROUPS, jnp.exp(gl - gmax), 0.0), axis=-1, keepdims=True)
    p_grp = 1.0 / gsum
    lo = N_GROUPS + EXPERTS_PER_GROUP * g_idx
    el = jnp.where(lane >= lo, jnp.where(lane < lo + EXPERTS_PER_GROUP, logits, neg), neg)
    l1 = jnp.max(el, axis=-1, keepdims=True)
    i1 = jnp.min(jnp.where(el == l1, lane, none), axis=-1, keepdims=True)
    el2 = jnp.where(lane == i1, neg, el)
    l2 = jnp.max(el2, axis=-1, keepdims=True)
    i2 = jnp.min(jnp.where(el2 == l2, lane, none), axis=-1, keepdims=True)
    t = jnp.exp(l2 - l1)
    inv = p_grp / (1.0 + t)
    e1 = i1 - N_GROUPS
    e2 = i2 - N_GROUPS
    rows = logits.shape[0]
    onehot = jnp.where(lane == e1, 1.0, jnp.where(lane == e2, 1.0, 0.0))
    tri = (lax.broadcasted_iota(jnp.int32, (rows, rows), 0) > lax.broadcasted_iota(jnp.int32, (rows, rows), 1))
    before = jnp.dot(jnp.where(tri, 1.0, 0.0).astype(BF16), onehot.astype(BF16),
                     preferred_element_type=F32) + cnt_ref[...]
    r1 = jnp.sum(jnp.where(lane == e1, before, 0.0), axis=-1, keepdims=True)
    r2 = jnp.sum(jnp.where(lane == e2, before, 0.0), axis=-1, keepdims=True)
    cnt_ref[...] = cnt_ref[...] + jnp.sum(onehot, axis=0, keepdims=True)
    out = jnp.where(lane_i == 4, r1, jnp.where(lane_i == 5, r2, 0.0))
    out = jnp.where(lane_i == 2, inv, jnp.where(lane_i == 3, inv * t, out))
    return jnp.where(lane_i == 0, e1, jnp.where(lane_i == 1, e2, out))


ER_ROWS = 8


def _post_mixer(x, mix, lg_ref, lb_ref, rwh_ref, rwl_ref, rb_ref, x1_ref, info_ref, er_ref, cnt_out_ref, cnt_ref):
    x1 = _layer_norm(DEEPNORM_ALPHA * x + mix, lg_ref[...], lb_ref[...])
    x1_ref[...] = x1
    info = _router(x1, rwh_ref, rwl_ref, rb_ref, cnt_ref)
    info_ref[...] = info
    er_ref[0] = jnp.transpose(info)[0:ER_ROWS, :].astype(jnp.int32)
    cnt_out_ref[...] = cnt_ref[...]


def _init_counts(cnt_ref, first_step, cnt_in_ref):
    @pl.when(first_step)
    def _():
        cnt_ref[...] = jnp.zeros_like(cnt_ref) if cnt_in_ref is None else cnt_in_ref[...]


def _conv_mixer_body(x_ref, st_ref, win_ref, wdw_ref, cg_ref, cb_ref, wout_ref, post_w, cnt_in_ref,
                     x1_ref, info_ref, er_ref, cnt_out_ref, u_ref, cnt_ref, ubuf, cbuf, *, nseg, seg):
    carry = st_ref is None
    first_step = (pl.program_id(0) == 0) & (pl.program_id(1) == 0) if carry else pl.program_id(0) == 0
    _init_counts(cnt_ref, first_step, cnt_in_ref)
    x = x_ref[...]
    h = jnp.dot(x.astype(BF16), win_ref[...], preferred_element_type=F32)
    u = h[:, :D_MODEL] * jax.nn.sigmoid(h[:, D_MODEL:])
    if carry:
        @pl.when(pl.program_id(1) == 0)
        def _():
            ubuf[:, 0, 0:STATE_PAD, :] = jnp.zeros((N_LANE_CHUNKS, STATE_PAD, LANES), F32)
        u_ref[0] = u[seg - STATE_PAD:, :]
    else:
        u_ref[...] = u
    for lc in range(N_LANE_CHUNKS):
        for s in range(nseg):
            ubuf[lc, s, STATE_PAD:STATE_PAD + seg, :] = u[s * seg:(s + 1) * seg, lc * LANES:(lc + 1) * LANES]
            if not carry:
                ubuf[lc, s, 0:STATE_PAD, :] = st_ref[s, :, lc * LANES:(lc + 1) * LANES]

    def conv_lane_chunk(lc, c):
        w = wdw_ref[lc]
        for s in range(nseg):
            acc = jnp.zeros((seg, LANES), F32)
            for k in range(CONV_WIDTH):
                off = k + STATE_PAD - (CONV_WIDTH - 1)
                acc = acc + w[k:k + 1, :] * ubuf[lc, s, off:off + seg, :]
            cbuf[lc, s * seg:(s + 1) * seg, :] = acc
        return c

    lax.fori_loop(0, N_LANE_CHUNKS, conv_lane_chunk, 0)
    if carry:
        for lc in range(N_LANE_CHUNKS):
            ubuf[lc, 0, 0:STATE_PAD, :] = u[seg - STATE_PAD:, lc * LANES:(lc + 1) * LANES]
    dw = jnp.concatenate([cbuf[lc] for lc in range(N_LANE_CHUNKS)], axis=1)
    sw = _layer_norm(dw, cg_ref[...], cb_ref[...])
    sw = sw * jax.nn.sigmoid(sw)
    mix = jnp.dot(sw.astype(BF16), wout_ref[...], preferred_element_type=F32)
    _post_mixer(x, mix, *post_w, x1_ref, info_ref, er_ref, cnt_out_ref, cnt_ref)


N_POST_W = 5


def _conv_prompt_kernel(x_ref, win_ref, wdw_ref, cg_ref, cb_ref, wout_ref, *rest):
    post_w, rest = rest[:N_POST_W], rest[N_POST_W:]
    _conv_mixer_body(x_ref, None, win_ref, wdw_ref, cg_ref, cb_ref, wout_ref, post_w, None, *rest,
                     nseg=1, seg=TM)


def _conv_sample_kernel(x_ref, st_ref, win_ref, wdw_ref, cg_ref, cb_ref, wout_ref, *rest):
    post_w, cnt_in_ref, rest = rest[:N_POST_W], rest[N_POST_W], rest[N_POST_W + 1 + N_ALIASED:]
    _conv_mixer_body(x_ref, st_ref, win_ref, wdw_ref, cg_ref, cb_ref, wout_ref, post_w, cnt_in_ref, *rest,
                     nseg=TM // DEC_SEQ, seg=DEC_SEQ)


def _full(shape):
    return pl.BlockSpec(shape, lambda *_: (0,) * len(shape))


ANY_SPEC = pl.BlockSpec(memory_space=pl.ANY)
CNT_SHAPE = jax.ShapeDtypeStruct((1, LANES), F32)
CNT_SCRATCH = pltpu.VMEM((1, LANES), F32)
PROMPT_TILES = T_PROMPT // TM
N_ALIASED = 3
TOKEN_OUT_SHAPES = [jax.ShapeDtypeStruct((T_ALL, D_MODEL), F32), jax.ShapeDtypeStruct((T_ALL, LANES), F32),
                    jax.ShapeDtypeStruct((T_ALL // TM, ER_ROWS, TM), jnp.int32)]


def _token_out_specs(tile):
    return [pl.BlockSpec((TM, D_MODEL), lambda *g: (tile(*g), 0)), pl.BlockSpec((TM, LANES), lambda *g: (tile(*g), 0)),
            pl.BlockSpec((1, ER_ROWS, TM), lambda *g: (tile(*g), 0, 0))]


def _conv_mixer(x_p, x_s, s_first, state_pad, w):
    w_specs = [_full(a.shape) for a in w]
    tiles_per_seq = SEQ // TM
    ptile = lambda b, t: b * tiles_per_seq + t
    x1, info, er, cnt, tail = pl.pallas_call(
        _conv_prompt_kernel,
        grid=(BATCH, tiles_per_seq),
        in_specs=[pl.BlockSpec((TM, D_MODEL), lambda b, t: (ptile(b, t), 0))] + w_specs,
        out_specs=_token_out_specs(ptile) + [_full(CNT_SHAPE.shape),
                                             pl.BlockSpec((1, STATE_PAD, D_MODEL), lambda b, t: (b, 0, 0))],
        out_shape=TOKEN_OUT_SHAPES + [CNT_SHAPE, jax.ShapeDtypeStruct((BATCH, STATE_PAD, D_MODEL), F32)],
        scratch_shapes=[CNT_SCRATCH, pltpu.VMEM((N_LANE_CHUNKS, 1, STATE_PAD + TM, LANES), F32),
                        pltpu.VMEM((N_LANE_CHUNKS, TM, LANES), F32)],
        compiler_params=pltpu.CompilerParams(dimension_semantics=("arbitrary", "arbitrary"),
                                             vmem_limit_bytes=VMEM_LIMIT),
        name="conv_mixer_prompt",
    )(x_p, *w)
    nb = TM // DEC_SEQ
    n_in = 2 + len(w) + 1
    x1, info, er, cnt, u_s = pl.pallas_call(
        _conv_sample_kernel,
        grid=(T_SAMPLE // TM,),
        in_specs=[pl.BlockSpec((TM, D_MODEL), lambda i: (s_first + i, 0)),
                  pl.BlockSpec((nb, STATE_PAD, D_MODEL), lambda i: (i, 0, 0))]
                 + w_specs + [_full(CNT_SHAPE.shape)] + [ANY_SPEC] * N_ALIASED,
        out_specs=_token_out_specs(lambda i: PROMPT_TILES + i)
                  + [_full(CNT_SHAPE.shape), pl.BlockSpec((TM, D_MODEL), lambda i: (i, 0))],
        out_shape=TOKEN_OUT_SHAPES + [CNT_SHAPE, jax.ShapeDtypeStruct((T_SAMPLE, D_MODEL), F32)],
        scratch_shapes=[CNT_SCRATCH, pltpu.VMEM((N_LANE_CHUNKS, nb, STATE_PAD + DEC_SEQ, LANES), F32),
                        pltpu.VMEM((N_LANE_CHUNKS, TM, LANES), F32)],
        input_output_aliases={n_in + k: k for k in range(N_ALIASED)},
        compiler_params=pltpu.CompilerParams(dimension_semantics=("arbitrary",),
                                             vmem_limit_bytes=VMEM_LIMIT),
        name="conv_mixer_sample",
    )(x_s, state_pad, *w, cnt, x1, info, er)
    return x1, info, er, cnt, tail, u_s


def _rope_lanes(v, c, sa, sb):
    width = v.shape[-1]
    half = ROT_DIM // 2
    return v * c + pltpu.roll(v, width - half, 1) * sa + pltpu.roll(v, half, 1) * sb


def _attn_mixer_body(sink_ref, x_ref, wq_ref, c_ref, sa_ref, sb_ref, kd_ref, vd_ref, wo_ref, post_w, cnt_in_ref,
                     x1_ref, info_ref, er_ref, cnt_out_ref, cnt_ref, obuf, bias_ref, *, nunit, uq, prompt):
    first_step = (pl.program_id(0) == 0) & (pl.program_id(1) == 0) if prompt else pl.program_id(0) == 0
    _init_counts(cnt_ref, first_step, cnt_in_ref)

    @pl.when(first_step)
    def _():
        shape = bias_ref.shape[1:]
        col = lax.broadcasted_iota(jnp.int32, shape, 1) & (KEY_WIN - 1)
        if prompt:
            chunk_shift = CHUNK.bit_length() - 1
            qchunk = (lax.broadcasted_iota(jnp.int32, shape, 0) & (uq - 1)) >> chunk_shift
            kchunk = col >> chunk_shift
            band = jnp.where(kchunk >= qchunk, jnp.where(kchunk <= qchunk + WINDOW // CHUNK, 0.0, NEG_INF), NEG_INF)
            bias_ref[0] = band
            bias_ref[1] = jnp.where(col >= WINDOW, band, NEG_INF)
        else:
            bias_ref[0] = jnp.where(col < WINDOW + DEC_SEQ, 0.0, NEG_INF)

    x = x_ref[...]
    q = jnp.dot(x.astype(BF16), wq_ref[...], preferred_element_type=F32)
    rep = D_MODEL // LANES
    q = _rope_lanes(q, jnp.tile(c_ref[...], (1, rep)), jnp.tile(sa_ref[...], (1, rep)),
                    jnp.tile(sb_ref[...], (1, rep)))
    qb = (q * ATTN_SCALE).astype(BF16)
    pairs = N_HEADS // N_KV_HEADS // 2
    lane_k = lax.broadcasted_iota(jnp.int32, (KEY_WIN, LANES), 1)
    lane_o = lax.broadcasted_iota(jnp.int32, (uq, LANES), 1)
    for un in range(nunit):
        r0 = un * uq
        if prompt:
            kstart = pl.multiple_of(pl.program_id(1) * (nunit * uq) + r0, LANES)
            bias = bias_ref[jnp.where(kstart == 0, 1, 0)]
        else:
            bias = bias_ref[0]
        for kvh in range(N_KV_HEADS):
            if prompt:
                kw = kd_ref[0, kvh, pl.ds(kstart, KEY_WIN), :]
                vw = vd_ref[0, kvh, pl.ds(kstart, KEY_WIN), :]
            else:
                kw = kd_ref[un, kvh]
                vw = vd_ref[un, kvh]
            zero = jnp.zeros_like(kw)
            kbd = jnp.concatenate([jnp.where(lane_k < HEAD_DIM, kw, zero),
                                   jnp.where(lane_k >= HEAD_DIM, kw, zero)], axis=0)
            vbd = jnp.concatenate([jnp.where(lane_k < HEAD_DIM, vw, zero),
                                   jnp.where(lane_k >= HEAD_DIM, vw, zero)], axis=0)
            q4 = jnp.concatenate([qb[r0:r0 + uq, (kvh * pairs + p) * LANES:(kvh * pairs + p + 1) * LANES]
                                  for p in range(pairs)], axis=0)
            s = lax.dot_general(q4, kbd, (((1,), (1,)), ((), ())), preferred_element_type=F32)
            s = s + bias
            p_parts, inv_parts = [], []
            for p in range(pairs):
                halves, invs = [], []
                for hf in range(2):
                    sk = sink_ref[(kvh * pairs + p) * 2 + hf]
                    sh = s[p * uq:(p + 1) * uq, hf * KEY_WIN:(hf + 1) * KEY_WIN]
                    m = jnp.maximum(jnp.max(sh, axis=-1, keepdims=True), sk)
                    pe = jnp.exp(sh - m)
                    den = jnp.sum(pe, axis=-1, keepdims=True) + jnp.exp(sk - m)
                    halves.append(pe.astype(BF16))
                    invs.append(1.0 / den)
                p_parts.append(jnp.concatenate(halves, axis=1))
                inv_parts.append(invs)
            pm = jnp.concatenate(p_parts, axis=0)
            o4 = jnp.dot(pm, vbd, preferred_element_type=F32)
            for p in range(pairs):
                inv = jnp.where(lane_o < HEAD_DIM, inv_parts[p][0], inv_parts[p][1])
                hp = kvh * pairs + p
                obuf[r0:r0 + uq, hp * LANES:(hp + 1) * LANES] = (o4[p * uq:(p + 1) * uq, :] * inv).astype(BF16)
    mix = jnp.dot(obuf[...], wo_ref[...], preferred_element_type=F32)
    _post_mixer(x, mix, *post_w, x1_ref, info_ref, er_ref, cnt_out_ref, cnt_ref)


N_ATTN_IN = 9


def _attn_prompt_kernel(*refs):
    ins, post_w, rest = refs[:N_ATTN_IN], refs[N_ATTN_IN:N_ATTN_IN + N_POST_W], refs[N_ATTN_IN + N_POST_W:]
    _attn_mixer_body(*ins, post_w, None, *rest, nunit=TM // UQ_PROMPT, uq=UQ_PROMPT, prompt=True)


def _attn_sample_kernel(*refs):
    ins, post_w = refs[:N_ATTN_IN], refs[N_ATTN_IN:N_ATTN_IN + N_POST_W]
    cnt_in_ref, rest = refs[N_ATTN_IN + N_POST_W], refs[N_ATTN_IN + N_POST_W + 1 + N_ALIASED:]
    _attn_mixer_body(*ins, post_w, cnt_in_ref, *rest, nunit=TM // DEC_SEQ, uq=DEC_SEQ, prompt=False)


def _attn_mixer(x, sinks, wq, wo, post_w, rope_p, rope_s, kv_p, kv_s):
    tiles_per_seq = SEQ // TM
    smem = pl.BlockSpec(memory_space=pltpu.SMEM)
    tail_specs = [_full(wo.shape)] + [_full(a.shape) for a in post_w]
    kd, vd = kv_p
    ptile = lambda b, t: b * tiles_per_seq + t
    pair_rows = N_HEADS // N_KV_HEADS // 2
    x1, info, er, cnt = pl.pallas_call(
        _attn_prompt_kernel,
        grid=(BATCH, tiles_per_seq),
        in_specs=[smem, pl.BlockSpec((TM, D_MODEL), lambda b, t: (ptile(b, t), 0)), _full(wq.shape)]
                 + [pl.BlockSpec((TM, LANES), lambda b, t: (t, 0))] * 3
                 + [pl.BlockSpec((1,) + kd.shape[1:], lambda b, t: (b, 0, 0, 0))] * 2 + tail_specs,
        out_specs=_token_out_specs(ptile) + [_full(CNT_SHAPE.shape)],
        out_shape=TOKEN_OUT_SHAPES + [CNT_SHAPE],
        scratch_shapes=[CNT_SCRATCH, pltpu.VMEM((TM, D_MODEL), BF16),
                        pltpu.VMEM((2, pair_rows * UQ_PROMPT, 2 * KEY_WIN), F32)],
        compiler_params=pltpu.CompilerParams(dimension_semantics=("arbitrary", "arbitrary"),
                                             vmem_limit_bytes=VMEM_LIMIT),
        name="attn_mixer_prompt",
    )(sinks, x, wq, *rope_p, kd, vd, wo, *post_w)
    nb = TM // DEC_SEQ
    kd, vd = kv_s
    n_in = N_ATTN_IN + N_POST_W + 1
    x1, info, er, cnt = pl.pallas_call(
        _attn_sample_kernel,
        grid=(T_SAMPLE // TM,),
        in_specs=[smem, pl.BlockSpec((TM, D_MODEL), lambda i: (PROMPT_TILES + i, 0)), _full(wq.shape)]
                 + [_full((TM, LANES))] * 3
                 + [pl.BlockSpec((nb,) + kd.shape[1:], lambda i: (i, 0, 0, 0))] * 2 + tail_specs
                 + [_full(CNT_SHAPE.shape)] + [ANY_SPEC] * N_ALIASED,
        out_specs=_token_out_specs(lambda i: PROMPT_TILES + i) + [_full(CNT_SHAPE.shape)],
        out_shape=TOKEN_OUT_SHAPES + [CNT_SHAPE],
        scratch_shapes=[CNT_SCRATCH, pltpu.VMEM((TM, D_MODEL), BF16),
                        pltpu.VMEM((1, pair_rows * DEC_SEQ, 2 * KEY_WIN), F32)],
        input_output_aliases={n_in + k: k for k in range(N_ALIASED)},
        compiler_params=pltpu.CompilerParams(dimension_semantics=("arbitrary",),
                                             vmem_limit_bytes=VMEM_LIMIT),
        name="attn_mixer_sample",
    )(sinks, x, wq, *rope_s, kd, vd, wo, *post_w, cnt, x1, info, er)
    return x1, info, er, cnt


def _kv_kernel(x_ref, wkv_ref, c_ref, sa_ref, sb_ref, k_ref, v_ref):
    kv = jnp.dot(x_ref[...].astype(BF16), wkv_ref[...], preferred_element_type=F32)
    k_ref[...] = _rope_lanes(kv[:, :KV_WIDTH], c_ref[...], sa_ref[...], sb_ref[...])
    v_ref[...] = kv[:, KV_WIDTH:]


def _kv_proj(x, wkv, rope_all):
    tok = lambda i: (i, 0)
    return pl.pallas_call(
        _kv_kernel,
        grid=(T_ALL // TM,),
        in_specs=[pl.BlockSpec((TM, D_MODEL), tok), _full(wkv.shape)] + [pl.BlockSpec((TM, LANES), tok)] * 3,
        out_specs=[pl.BlockSpec((TM, KV_WIDTH), tok)] * 2,
        out_shape=[jax.ShapeDtypeStruct((T_ALL, KV_WIDTH), F32)] * 2,
        compiler_params=pltpu.CompilerParams(dimension_semantics=("arbitrary",)),
        name="kv_proj",
    )(x, wkv, *rope_all)


N_TILES = T_ALL // TM


def _row_copy_wait(src_rows, dst_rows, sem):
    pltpu.make_async_copy(src_rows, dst_rows, sem).wait()


POS_BLOCK = (1, 2, TM)


def _dispatch_kernel(zrow_ref, zflag_ref, pos_ref, x1_ref, xs_hbm, zbuf, sem, zsem):
    i = pl.program_id(0)

    @pl.when(i == 0)
    def _():
        zbuf[...] = jnp.zeros_like(zbuf)

        def zero_tile(e):
            return pltpu.make_async_copy(zbuf, xs_hbm.at[pl.ds(pl.multiple_of(zrow_ref[e], MOE_TM), MOE_TM)], zsem)

        for e in range(N_EXPERTS):
            @pl.when(zflag_ref[e] > 0)
            def _():
                zero_tile(e).start()
        for e in range(N_EXPERTS):
            @pl.when(zflag_ref[e] > 0)
            def _():
                zero_tile(e).wait()

    for j in range(TM):
        for slot in range(2):
            pltpu.make_async_copy(x1_ref.at[pl.ds(j, 1)],
                                  xs_hbm.at[pl.ds(pos_ref[0, slot, j], 1)], sem).start()
    for slot in range(2):
        _row_copy_wait(x1_ref, xs_hbm.at[pl.ds(0, TM)], sem)


def _dispatch(x1, pos, zrow, zflag):
    return pl.pallas_call(
        _dispatch_kernel,
        grid_spec=pltpu.PrefetchScalarGridSpec(
            num_scalar_prefetch=2,
            grid=(N_TILES,),
            in_specs=[pl.BlockSpec(POS_BLOCK, lambda i, *_: (i, 0, 0), memory_space=pltpu.SMEM),
                      pl.BlockSpec((TM, D_MODEL), lambda i, *_: (i, 0))],
            out_specs=ANY_SPEC,
            scratch_shapes=[pltpu.VMEM((MOE_TM, D_MODEL), F32), pltpu.SemaphoreType.DMA,
                            pltpu.SemaphoreType.DMA],
        ),
        out_shape=jax.ShapeDtypeStruct((MOE_ROWS, D_MODEL), F32),
        compiler_params=pltpu.CompilerParams(dimension_semantics=("arbitrary",)),
        name="moe_dispatch",
    )(zrow, zflag, pos, x1)


def _moe_kernel(te_ref, slot_ref, next_ref, nv_ref, xs_ref, wg_hbm, wu_hbm, wd_hbm, ys_ref,
                wg_st, wu_st, wd_st, wgu_bf, wd_bf, sem, *, base):
    i = pl.program_id(0)

    def fetch(e, slot):
        return (pltpu.make_async_copy(wg_hbm.at[base + e], wg_st.at[slot], sem.at[slot, 0]),
                pltpu.make_async_copy(wu_hbm.at[base + e], wu_st.at[slot], sem.at[slot, 1]),
                pltpu.make_async_copy(wd_hbm.at[base + e], wd_st.at[slot], sem.at[slot, 2]))

    @pl.when(i == 0)
    def _():
        for c in fetch(te_ref[0], slot_ref[0]):
            c.start()

    @pl.when(i < nv_ref[0])
    def _():
        e = te_ref[i]
        slot = slot_ref[i]

        @pl.when((i == 0) | (e != te_ref[jnp.maximum(i - 1, 0)]))
        def _():
            for c in fetch(e, slot):
                c.wait()
            wgu_bf[:, :EXPERT_FF] = wg_st[slot].astype(BF16)
            wgu_bf[:, EXPERT_FF:] = wu_st[slot].astype(BF16)
            wd_bf[...] = wd_st[slot].astype(BF16)

            @pl.when(next_ref[i] >= 0)
            def _():
                for c in fetch(next_ref[i], 1 - slot):
                    c.start()

        hgu = jnp.dot(xs_ref[...].astype(BF16), wgu_bf[...], preferred_element_type=F32)
        hg = hgu[:, :EXPERT_FF]
        h = hg * jax.nn.sigmoid(hg) * hgu[:, EXPERT_FF:]
        ys_ref[...] = jnp.dot(h.astype(BF16), wd_bf[...], preferred_element_type=F32)


def _moe_experts(xs, tile_expert, tile_slot, tile_next, n_valid, wg, wu, wd, layer):
    row = lambda i, te, sl, nx, nv: (jnp.minimum(i, nv[0] - 1), 0)
    return pl.pallas_call(
        functools.partial(_moe_kernel, base=layer * N_EXPERTS),
        grid_spec=pltpu.PrefetchScalarGridSpec(
            num_scalar_prefetch=4,
            grid=(MOE_TILES,),
            in_specs=[pl.BlockSpec((MOE_TM, D_MODEL), row), ANY_SPEC, ANY_SPEC, ANY_SPEC],
            out_specs=pl.BlockSpec((MOE_TM, D_MODEL), row),
            scratch_shapes=[pltpu.VMEM((2, D_MODEL, EXPERT_FF), F32), pltpu.VMEM((2, D_MODEL, EXPERT_FF), F32),
                            pltpu.VMEM((2, EXPERT_FF, D_MODEL), F32),
                            pltpu.VMEM((D_MODEL, 2 * EXPERT_FF), BF16), pltpu.VMEM((EXPERT_FF, D_MODEL), BF16),
                            pltpu.SemaphoreType.DMA((2, 3))],
        ),
        out_shape=jax.ShapeDtypeStruct((MOE_ROWS, D_MODEL), F32),
        compiler_params=pltpu.CompilerParams(dimension_semantics=("arbitrary",), vmem_limit_bytes=VMEM_LIMIT),
        name="moe_experts",
    )(tile_expert, tile_slot, tile_next, n_valid, xs, wg, wu, wd)


def _ffn_ln_kernel(pos_ref, pos_next_ref, x1_ref, info_ref, g_ref, b_ref, ys_hbm, *rest, split):
    out_refs, (ybuf, sem) = rest[:-2], rest[-2:]
    i = pl.program_id(0)

    def issue(p_ref, par):
        for j in range(TM):
            for slot in range(2):
                pltpu.make_async_copy(ys_hbm.at[pl.ds(p_ref[0, slot, j], 1)],
                                      ybuf.at[par, slot, pl.ds(j, 1)], sem.at[par]).start()

    @pl.when(i == 0)
    def _():
        issue(pos_ref, 0)

    @pl.when(i + 1 < N_TILES)
    def _():
        issue(pos_next_ref, (i + 1) % 2)

    par = i % 2
    for slot in range(2):
        _row_copy_wait(ys_hbm.at[pl.ds(0, TM)], ybuf.at[par, slot], sem.at[par])
    info = info_ref[...]
    f = info[:, 2:3] * ybuf[par, 0] + info[:, 3:4] * ybuf[par, 1]
    x2 = _layer_norm(DEEPNORM_ALPHA * x1_ref[...] + f, g_ref[...], b_ref[...])
    if split:
        @pl.when(i < PROMPT_TILES)
        def _():
            out_refs[0][...] = x2

        @pl.when(i >= PROMPT_TILES)
        def _():
            out_refs[1][...] = x2
    else:
        out_refs[0][...] = x2


def _ffn_ln(x1, ys, pos, info, g, b, split):
    tok = lambda i: (i, 0)
    pos_spec = lambda fn: pl.BlockSpec(POS_BLOCK, fn, memory_space=pltpu.SMEM)
    if split:
        out_specs = [pl.BlockSpec((TM, D_MODEL), lambda i: (jnp.minimum(i, PROMPT_TILES - 1), 0)),
                     pl.BlockSpec((TM, D_MODEL), lambda i: (jnp.maximum(i - PROMPT_TILES, 0), 0))]
        out_shape = [jax.ShapeDtypeStruct((T_PROMPT, D_MODEL), F32), jax.ShapeDtypeStruct((T_SAMPLE, D_MODEL), F32)]
    else:
        out_specs = pl.BlockSpec((TM, D_MODEL), tok)
        out_shape = jax.ShapeDtypeStruct((T_ALL, D_MODEL), F32)
    return pl.pallas_call(
        functools.partial(_ffn_ln_kernel, split=split),
        grid=(N_TILES,),
        in_specs=[pos_spec(lambda i: (i, 0, 0)), pos_spec(lambda i: (jnp.minimum(i + 1, N_TILES - 1), 0, 0)),
                  pl.BlockSpec((TM, D_MODEL), tok), pl.BlockSpec((TM, LANES), tok), _full(g.shape), _full(b.shape),
                  ANY_SPEC],
        out_specs=out_specs,
        out_shape=out_shape,
        scratch_shapes=[pltpu.VMEM((2, 2, TM, D_MODEL), F32), pltpu.SemaphoreType.DMA((2,))],
        compiler_params=pltpu.CompilerParams(dimension_semantics=("arbitrary",)),
        name="ffn_ln",
    )(pos, pos, x1, info, g, b, ys)


def _moe_layer(x1, info, er, cnt, wg, wu, wd, layer, g, b, split):
    counts = cnt[0, :N_EXPERTS].astype(jnp.int32)
    tiles_per = (counts + MOE_TM - 1) // MOE_TM
    tile_end = jnp.cumsum(tiles_per)
    pad_start = (tile_end - tiles_per) * MOE_TM
    ids = jnp.arange(N_EXPERTS, dtype=jnp.int32)
    half = ER_ROWS // 2
    hit = er[None, :, 0:2, :] == ids[:, None, None, None]
    pos = jnp.sum(jnp.where(hit, pad_start[:, None, None, None], 0), axis=0) + er[:, half:half + 2, :]
    n_valid = tile_end[-1:]
    tile_ids = jnp.minimum(jnp.arange(MOE_TILES, dtype=jnp.int32), n_valid[0] - 1)
    tile_expert = jnp.sum(tile_ids[:, None] >= tile_end[None, :], axis=1, dtype=jnp.int32)
    nonempty = tiles_per > 0
    slot_of = (jnp.cumsum(nonempty.astype(jnp.int32)) - 1) % 2
    later = jnp.where(nonempty[None, :] & (ids[None, :] > ids[:, None]), ids[None, :], N_EXPERTS)
    next_of = jnp.min(later, axis=1)
    next_of = jnp.where(next_of < N_EXPERTS, next_of, -1)
    zrow = jnp.maximum(tile_end - 1, 0) * MOE_TM
    xs = _dispatch(x1, pos, zrow, tiles_per)
    ys = _moe_experts(xs, tile_expert, slot_of[tile_expert], next_of[tile_expert], n_valid, wg, wu, wd, layer)
    return _ffn_ln(x1, ys, pos, info, g, b, split)


def _rope_tables(pos):
    half = ROT_DIM // 2
    inv_freq = ROPE_THETA ** (-jnp.arange(0, ROT_DIM, 2, dtype=F32) / ROT_DIM)
    ang = pos.astype(F32)[:, None] * inv_freq[None, :]
    cos, sin = jnp.cos(ang), jnp.sin(ang)
    n = pos.shape[0]
    ones = jnp.ones((n, HEAD_DIM - ROT_DIM), F32)
    zeros_h = jnp.zeros((n, half), F32)
    zeros_r = jnp.zeros((n, HEAD_DIM - ROT_DIM), F32)
    c = jnp.concatenate([cos, cos, ones], axis=1)
    sa = jnp.concatenate([-sin, zeros_h, zeros_r], axis=1)
    sb = jnp.concatenate([zeros_h, sin, zeros_r], axis=1)
    rep = LANES // HEAD_DIM
    return tuple(jnp.tile(t, (1, rep)) for t in (c, sa, sb))


def _dup_heads(kv):
    b, l, _ = kv.shape
    h = kv.reshape(b, l, N_KV_HEADS, HEAD_DIM).transpose(0, 2, 1, 3)
    return jnp.concatenate([h, h], axis=-1).astype(BF16)


def kernel(x_prompt, x_sample, state_conv, cache_k, cache_v, ln_mix_g, ln_mix_b, ln_ffn_g, ln_ffn_b, conv_w_in, conv_w_dw, conv_ln_g, conv_ln_b, conv_w_out, w_kv, attn_w_q, attn_sinks, attn_w_o, router_w_group, router_b_group, router_w_expert, router_b_expert, expert_w_gate, expert_w_up, expert_w_down):
    x = None
    wg = expert_w_gate.reshape(DEPTH * N_EXPERTS, D_MODEL, EXPERT_FF)
    wu = expert_w_up.reshape(DEPTH * N_EXPERTS, D_MODEL, EXPERT_FF)
    wd = expert_w_down.reshape(DEPTH * N_EXPERTS, EXPERT_FF, D_MODEL)
    row = lambda v: v.reshape(1, -1)

    def router_weights(layer):
        w = jnp.concatenate([router_w_group[layer], router_w_expert[layer]], axis=1)
        w = jnp.pad(w, ((0, 0), (0, LANES - w.shape[1])))
        w_hi = w.astype(BF16)
        w_lo = (w - w_hi.astype(F32)).astype(BF16)
        bias = jnp.concatenate([router_b_group[layer], router_b_expert[layer].reshape(-1)])
        return w_hi, w_lo, jnp.pad(bias, (0, LANES - bias.shape[0])).reshape(1, LANES)

    pos_p = jnp.arange(SEQ, dtype=jnp.int32)
    pos_s = PAST_LEN + jnp.arange(DEC_SEQ, dtype=jnp.int32)
    rope_p = _rope_tables(pos_p)
    rope_s1 = _rope_tables(pos_s)
    rope_s = tuple(jnp.tile(t, (TM // DEC_SEQ, 1)) for t in rope_s1)
    rope_all = tuple(jnp.concatenate([jnp.tile(p, (BATCH, 1)), jnp.tile(s, (DEC_BATCH, 1))], axis=0)
                     for p, s in zip(rope_p, rope_s1))

    tails, us = [], []
    kv_p = kv_s = None
    outs_kv = None
    for layer in range(DEPTH):
        post_w = (row(ln_mix_g[layer]), row(ln_mix_b[layer])) + router_weights(layer)
        if layer < N_A_LAYERS:
            st = jnp.pad(state_conv[layer], ((0, 0), (STATE_PAD - (CONV_WIDTH - 1), 0), (0, 0)))
            wdw = jnp.pad(conv_w_dw[layer], ((0, STATE_PAD - CONV_WIDTH), (0, 0)))
            wdw = wdw.reshape(STATE_PAD, N_LANE_CHUNKS, LANES).transpose(1, 0, 2)
            w = (conv_w_in[layer].astype(BF16), wdw, row(conv_ln_g[layer]), row(conv_ln_b[layer]),
                 conv_w_out[layer].astype(BF16)) + post_w
            if layer == 0:
                x_in = (x_prompt.reshape(T_PROMPT, D_MODEL), x_sample.reshape(T_SAMPLE, D_MODEL), 0)
            else:
                x_in = (x, x, PROMPT_TILES)
            x1, info, er, cnt, tail, u_s = _conv_mixer(*x_in, st, w)
            tails.append(tail[:, STATE_PAD - (CONV_WIDTH - 1):, :])
            us.append(u_s.reshape(DEC_BATCH, DEC_SEQ, D_MODEL)[:, DEC_SEQ - (CONV_WIDTH - 1):, :])
        else:
            j = layer - N_A_LAYERS
            x1, info, er, cnt = _attn_mixer(x, attn_sinks[j], attn_w_q[j].astype(BF16), attn_w_o[j].astype(BF16),
                                        post_w, rope_p, rope_s, kv_p, kv_s)
        x = _moe_layer(x1, info, er, cnt, wg, wu, wd, layer, row(ln_ffn_g[layer]), row(ln_ffn_b[layer]),
                       split=layer == DEPTH - 1)
        if layer == N_A_LAYERS - 1:
            k, v = _kv_proj(x, w_kv.astype(BF16), rope_all)
            k_p = k[:T_PROMPT].reshape(BATCH, SEQ, KV_WIDTH)
            v_p = v[:T_PROMPT].reshape(BATCH, SEQ, KV_WIDTH)
            k_s = jnp.concatenate([cache_k.reshape(DEC_BATCH, WINDOW, KV_WIDTH),
                                   k[T_PROMPT:].reshape(DEC_BATCH, DEC_SEQ, KV_WIDTH)], axis=1)
            v_s = jnp.concatenate([cache_v.reshape(DEC_BATCH, WINDOW, KV_WIDTH),
                                   v[T_PROMPT:].reshape(DEC_BATCH, DEC_SEQ, KV_WIDTH)], axis=1)
            cache_shape = (-1, WINDOW, N_KV_HEADS, HEAD_DIM)
            outs_kv = (k_p[:, -WINDOW:].reshape(cache_shape), v_p[:, -WINDOW:].reshape(cache_shape),
                       k_s[:, -WINDOW:].reshape(cache_shape), v_s[:, -WINDOW:].reshape(cache_shape))
            front = ((0, 0), (WINDOW, 0), (0, 0))
            back = ((0, 0), (0, KEY_WIN - WINDOW - DEC_SEQ), (0, 0))
            kv_p = (_dup_heads(jnp.pad(k_p, front)), _dup_heads(jnp.pad(v_p, front)))
            kv_s = (_dup_heads(jnp.pad(k_s, back)), _dup_heads(jnp.pad(v_s, back)))
    y_prompt = x[0].reshape(BATCH, SEQ, D_MODEL)
    y_sample = x[1].reshape(DEC_BATCH, DEC_SEQ, D_MODEL)
    return (y_prompt, y_sample, jnp.stack(tails, axis=0), jnp.stack(us, axis=0)) + outs_kv
```

```python
import functools

import jax
import jax.numpy as jnp
from jax import lax
from jax.experimental import pallas as pl
from jax.experimental.pallas import tpu as pltpu

D_MODEL = 1024
BATCH = 2
SEQ = 8192
DEPTH = 4
DEC_BATCH = 32
DEC_SEQ = 32
PAST_LEN = 2048
CHUNK = 64
N_A_LAYERS = DEPTH // 2
CONV_WIDTH = 31
N_HEADS = 16
N_KV_HEADS = 2
HEAD_DIM = 64
KV_WIDTH = N_KV_HEADS * HEAD_DIM
WINDOW = 128
ROT_DIM = HEAD_DIM // 4
ROPE_THETA = 500000.0
N_GROUPS = 4
EXPERTS_PER_GROUP = 8
N_EXPERTS = N_GROUPS * EXPERTS_PER_GROUP
EXPERT_FF = D_MODEL // 2
DEEPNORM_ALPHA = (2 * DEPTH) ** 0.25
LN_EPS = 1e-5
ATTN_SCALE = HEAD_DIM ** -0.5
NEG_INF = -1e30

LANES = 128
N_LANE_CHUNKS = D_MODEL // LANES
T_PROMPT = BATCH * SEQ
T_SAMPLE = DEC_BATCH * DEC_SEQ
T_ALL = T_PROMPT + T_SAMPLE
TM = 256
STATE_PAD = 32
KEY_WIN = 256
UQ_PROMPT = 2 * CHUNK
MOE_TM = 256
MOE_ROWS = 2 * T_ALL + N_EXPERTS * MOE_TM
MOE_TILES = MOE_ROWS // MOE_TM
VMEM_LIMIT = 56 * 1024 * 1024

F32 = jnp.float32
BF16 = jnp.bfloat16


def _layer_norm(x, g, b):
    mu = jnp.mean(x, axis=-1, keepdims=True)
    xc = x - mu
    var = jnp.mean(xc * xc, axis=-1, keepdims=True)
    return xc * lax.rsqrt(var + LN_EPS) * g + b


def _router(x1, rwh_ref, rwl_ref, rb_ref, cnt_ref):
    x_hi = x1.astype(BF16)
    x_lo = (x1 - x_hi.astype(F32)).astype(BF16)
    w_hi = rwh_ref[...]
    logits = (jnp.dot(x_hi, w_hi, preferred_element_type=F32)
              + jnp.dot(x_hi, rwl_ref[...], preferred_element_type=F32)
              + jnp.dot(x_lo, w_hi, preferred_element_type=F32)) + rb_ref[...]
    lane_i = lax.broadcasted_iota(jnp.int32, logits.shape, 1)
    lane = lane_i.astype(F32)
    neg = jnp.float32(-3.0e38)
    none = jnp.float32(LANES)
    gl = jnp.where(lane_i < N_GROUPS, logits, neg)
    gmax = jnp.max(gl, axis=-1, keepdims=True)
    g_idx = jnp.min(jnp.where(gl == gmax, lane, none), axis=-1, keepdims=True)
    gsum = jnp.sum(jnp.where(lane_i < N_GROUPS, jnp.exp(gl - gmax), 0.0), axis=-1, keepdims=True)
    p_grp = 1.0 / gsum
    lo = N_GROUPS + EXPERTS_PER_GROUP * g_idx
    el = jnp.where(lane >= lo, jnp.where(lane < lo + EXPERTS_PER_GROUP, logits, neg), neg)
    l1 = jnp.max(el, axis=-1, keepdims=True)
    i1 = jnp.min(jnp.where(el == l1, lane, none), axis=-1, keepdims=True)
    el2 = jnp.where(lane == i1, neg, el)
    l2 = jnp.max(el2, axis=-1, keepdims=True)
    i2 = jnp.min(jnp.where(el2 == l2, lane, none), axis=-1, keepdims=True)
    t = jnp.exp(l2 - l1)
    inv = p_grp / (1.0 + t)
    e1 = i1 - N_GROUPS
    e2 = i2 - N_GROUPS
    rows = logits.shape[0]
    onehot = jnp.where(lane == e1, 1.0, jnp.where(lane == e2, 1.0, 0.0))
    tri = (lax.broadcasted_iota(jnp.int32, (rows, rows), 0) > lax.broadcasted_iota(jnp.int32, (rows, rows), 1))
    before = jnp.dot(jnp.where(tri, 1.0, 0.0).astype(BF16), onehot.astype(BF16),
                     preferred_element_type=F32) + cnt_ref[...]
    r1 = jnp.sum(jnp.where(lane == e1, before, 0.0), axis=-1, keepdims=True)
    r2 = jnp.sum(jnp.where(lane == e2, before, 0.0), axis=-1, keepdims=True)
    cnt_ref[...] = cnt_ref[...] + jnp.sum(onehot, axis=0, keepdims=True)
    out = jnp.where(lane_i == 4, r1, jnp.where(lane_i == 5, r2, 0.0))
    out = jnp.where(lane_i == 2, inv, jnp.where(lane_i == 3, inv * t, out))
    return jnp.where(lane_i == 0, e1, jnp.where(lane_i == 1, e2, out))


ER_ROWS = 8


def _post_mixer(x, mix, lg_ref, lb_ref, rwh_ref, rwl_ref, rb_ref, x1_ref, info_ref, er_ref, cnt_out_ref, cnt_ref):
    x1 = _layer_norm(DEEPNORM_ALPHA * x + mix, lg_ref[...], lb_ref[...])
    x1_ref[...] = x1
    info = _router(x1, rwh_ref, rwl_ref, rb_ref, cnt_ref)
    info_ref[...] = info
    er_ref[0] = jnp.transpose(info)[0:ER_ROWS, :].astype(jnp.int32)
    cnt_out_ref[...] = cnt_ref[...]


def _init_counts(cnt_ref, first_step, cnt_in_ref):
    @pl.when(first_step)
    def _():
        cnt_ref[...] = jnp.zeros_like(cnt_ref) if cnt_in_ref is None else cnt_in_ref[...]


def _conv_mixer_body(x_ref, st_ref, win_ref, wdw_ref, cg_ref, cb_ref, wout_ref, post_w, cnt_in_ref,
                     x1_ref, info_ref, er_ref, cnt_out_ref, u_ref, cnt_ref, ubuf, cbuf, *, nseg, seg):
    carry = st_ref is None
    first_step = (pl.program_id(0) == 0) & (pl.program_id(1) == 0) if carry else pl.program_id(0) == 0
    _init_counts(cnt_ref, first_step, cnt_in_ref)
    x = x_ref[...]
    h = jnp.dot(x.astype(BF16), win_ref[...], preferred_element_type=F32)
    u = h[:, :D_MODEL] * jax.nn.sigmoid(h[:, D_MODEL:])
    if carry:
        @pl.when(pl.program_id(1) == 0)
        def _():
            ubuf[:, 0, 0:STATE_PAD, :] = jnp.zeros((N_LANE_CHUNKS, STATE_PAD, LANES), F32)
        u_ref[0] = u[seg - STATE_PAD:, :]
    else:
        u_ref[...] = u
    for lc in range(N_LANE_CHUNKS):
        for s in range(nseg):
            ubuf[lc, s, STATE_PAD:STATE_PAD + seg, :] = u[s * seg:(s + 1) * seg, lc * LANES:(lc + 1) * LANES]
            if not carry:
                ubuf[lc, s, 0:STATE_PAD, :] = st_ref[s, :, lc * LANES:(lc + 1) * LANES]

    def conv_lane_chunk(lc, c):
        w = wdw_ref[lc]
        for s in range(nseg):
            acc = jnp.zeros((seg, LANES), F32)
            for k in range(CONV_WIDTH):
                off = k + STATE_PAD - (CONV_WIDTH - 1)
                acc = acc + w[k:k + 1, :] * ubuf[lc, s, off:off + seg, :]
            cbuf[lc, s * seg:(s + 1) * seg, :] = acc
        return c

    lax.fori_loop(0, N_LANE_CHUNKS, conv_lane_chunk, 0)
    if carry:
        for lc in range(N_LANE_CHUNKS):
            ubuf[lc, 0, 0:STATE_PAD, :] = u[seg - STATE_PAD:, lc * LANES:(lc + 1) * LANES]
    dw = jnp.concatenate([cbuf[lc] for lc in range(N_LANE_CHUNKS)], axis=1)
    sw = _layer_norm(dw, cg_ref[...], cb_ref[...])
    sw = sw * jax.nn.sigmoid(sw)
    mix = jnp.dot(sw.astype(BF16), wout_ref[...], preferred_element_type=F32)
    _post_mixer(x, mix, *post_w, x1_ref, info_ref, er_ref, cnt_out_ref, cnt_ref)


N_POST_W = 5


def _conv_prompt_kernel(x_ref, win_ref, wdw_ref, cg_ref, cb_ref, wout_ref, *rest):
    post_w, rest = rest[:N_POST_W], rest[N_POST_W:]
    _conv_mixer_body(x_ref, None, win_ref, wdw_ref, cg_ref, cb_ref, wout_ref, post_w, None, *rest,
                     nseg=1, seg=TM)


def _conv_sample_kernel(x_ref, st_ref, win_ref, wdw_ref, cg_ref, cb_ref, wout_ref, *rest):
    post_w, cnt_in_ref, rest = rest[:N_POST_W], rest[N_POST_W], rest[N_POST_W + 1 + N_ALIASED:]
    _conv_mixer_body(x_ref, st_ref, win_ref, wdw_ref, cg_ref, cb_ref, wout_ref, post_w, cnt_in_ref, *rest,
                     nseg=TM // DEC_SEQ, seg=DEC_SEQ)


def _full(shape):
    return pl.BlockSpec(shape, lambda *_: (0,) * len(shape))


ANY_SPEC = pl.BlockSpec(memory_space=pl.ANY)
CNT_SHAPE = jax.ShapeDtypeStruct((1, LANES), F32)
CNT_SCRATCH = pltpu.VMEM((1, LANES), F32)
PROMPT_TILES = T_PROMPT // TM
N_ALIASED = 3
TOKEN_OUT_SHAPES = [jax.ShapeDtypeStruct((T_ALL, D_MODEL), F32), jax.ShapeDtypeStruct((T_ALL, LANES), F32),
                    jax.ShapeDtypeStruct((T_ALL // TM, ER_ROWS, TM), jnp.int32)]


def _token_out_specs(tile):
    return [pl.BlockSpec((TM, D_MODEL), lambda *g: (tile(*g), 0)), pl.BlockSpec((TM, LANES), lambda *g: (tile(*g), 0)),
            pl.BlockSpec((1, ER_ROWS, TM), lambda *g: (tile(*g), 0, 0))]


def _conv_mixer(x_p, x_s, s_first, state_pad, w):
    w_specs = [_full(a.shape) for a in w]
    tiles_per_seq = SEQ // TM
    ptile = lambda b, t: b * tiles_per_seq + t
    x1, info, er, cnt, tail = pl.pallas_call(
        _conv_prompt_kernel,
        grid=(BATCH, tiles_per_seq),
        in_specs=[pl.BlockSpec((TM, D_MODEL), lambda b, t: (ptile(b, t), 0))] + w_specs,
        out_specs=_token_out_specs(ptile) + [_full(CNT_SHAPE.shape),
                                             pl.BlockSpec((1, STATE_PAD, D_MODEL), lambda b, t: (b, 0, 0))],
        out_shape=TOKEN_OUT_SHAPES + [CNT_SHAPE, jax.ShapeDtypeStruct((BATCH, STATE_PAD, D_MODEL), F32)],
        scratch_shapes=[CNT_SCRATCH, pltpu.VMEM((N_LANE_CHUNKS, 1, STATE_PAD + TM, LANES), F32),
                        pltpu.VMEM((N_LANE_CHUNKS, TM, LANES), F32)],
        compiler_params=pltpu.CompilerParams(dimension_semantics=("arbitrary", "arbitrary"),
                                             vmem_limit_bytes=VMEM_LIMIT),
        name="conv_mixer_prompt",
    )(x_p, *w)
    nb = TM // DEC_SEQ
    n_in = 2 + len(w) + 1
    x1, info, er, cnt, u_s = pl.pallas_call(
        _conv_sample_kernel,
        grid=(T_SAMPLE // TM,),
        in_specs=[pl.BlockSpec((TM, D_MODEL), lambda i: (s_first + i, 0)),
                  pl.BlockSpec((nb, STATE_PAD, D_MODEL), lambda i: (i, 0, 0))]
                 + w_specs + [_full(CNT_SHAPE.shape)] + [ANY_SPEC] * N_ALIASED,
        out_specs=_token_out_specs(lambda i: PROMPT_TILES + i)
                  + [_full(CNT_SHAPE.shape), pl.BlockSpec((TM, D_MODEL), lambda i: (i, 0))],
        out_shape=TOKEN_OUT_SHAPES + [CNT_SHAPE, jax.ShapeDtypeStruct((T_SAMPLE, D_MODEL), F32)],
        scratch_shapes=[CNT_SCRATCH, pltpu.VMEM((N_LANE_CHUNKS, nb, STATE_PAD + DEC_SEQ, LANES), F32),
                        pltpu.VMEM((N_LANE_CHUNKS, TM, LANES), F32)],
        input_output_aliases={n_in + k: k for k in range(N_ALIASED)},
        compiler_params=pltpu.CompilerParams(dimension_semantics=("arbitrary",),
                                             vmem_limit_bytes=VMEM_LIMIT),
        name="conv_mixer_sample",
    )(x_s, state_pad, *w, cnt, x1, info, er)
    return x1, info, er, cnt, tail, u_s


def _rope_lanes(v, c, sa, sb):
    width = v.shape[-1]
    half = ROT_DIM // 2
    return v * c + pltpu.roll(v, width - half, 1) * sa + pltpu.roll(v, half, 1) * sb


def _attn_mixer_body(sink_ref, x_ref, wq_ref, c_ref, sa_ref, sb_ref, kd_ref, vd_ref, wo_ref, post_w, cnt_in_ref,
                     x1_ref, info_ref, er_ref, cnt_out_ref, cnt_ref, obuf, bias_ref, *, nunit, uq, prompt):
    first_step = (pl.program_id(0) == 0) & (pl.program_id(1) == 0) if prompt else pl.program_id(0) == 0
    _init_counts(cnt_ref, first_step, cnt_in_ref)

    @pl.when(first_step)
    def _():
        shape = bias_ref.shape[1:]
        col = lax.broadcasted_iota(jnp.int32, shape, 1) & (KEY_WIN - 1)
        if prompt:
            chunk_shift = CHUNK.bit_length() - 1
            qchunk = (lax.broadcasted_iota(jnp.int32, shape, 0) & (uq - 1)) >> chunk_shift
            kchunk = col >> chunk_shift
            band = jnp.where(kchunk >= qchunk, jnp.where(kchunk <= qchunk + WINDOW // CHUNK, 0.0, NEG_INF), NEG_INF)
            bias_ref[0] = band
            bias_ref[1] = jnp.where(col >= WINDOW, band, NEG_INF)
        else:
            bias_ref[0] = jnp.where(col < WINDOW + DEC_SEQ, 0.0, NEG_INF)

    x = x_ref[...]
    q = jnp.dot(x.astype(BF16), wq_ref[...], preferred_element_type=F32)
    rep = D_MODEL // LANES
    q = _rope_lanes(q, jnp.tile(c_ref[...], (1, rep)), jnp.tile(sa_ref[...], (1, rep)),
                    jnp.tile(sb_ref[...], (1, rep)))
    qb = (q * ATTN_SCALE).astype(BF16)
    pairs = N_HEADS // N_KV_HEADS // 2
    lane_k = lax.broadcasted_iota(jnp.int32, (KEY_WIN, LANES), 1)
    lane_o = lax.broadcasted_iota(jnp.int32, (uq, LANES), 1)
    for un in range(nunit):
        r0 = un * uq
        if prompt:
            kstart = pl.multiple_of(pl.program_id(1) * (nunit * uq) + r0, LANES)
            bias = bias_ref[jnp.where(kstart == 0, 1, 0)]
        else:
            bias = bias_ref[0]
        for kvh in range(N_KV_HEADS):
            if prompt:
                kw = kd_ref[0, kvh, pl.ds(kstart, KEY_WIN), :]
                vw = vd_ref[0, kvh, pl.ds(kstart, KEY_WIN), :]
            else:
                kw = kd_ref[un, kvh]
                vw = vd_ref[un, kvh]
            zero = jnp.zeros_like(kw)
            kbd = jnp.concatenate([jnp.where(lane_k < HEAD_DIM, kw, zero),
                                   jnp.where(lane_k >= HEAD_DIM, kw, zero)], axis=0)
            vbd = jnp.concatenate([jnp.where(lane_k < HEAD_DIM, vw, zero),
                                   jnp.where(lane_k >= HEAD_DIM, vw, zero)], axis=0)
            q4 = jnp.concatenate([qb[r0:r0 + uq, (kvh * pairs + p) * LANES:(kvh * pairs + p + 1) * LANES]
                                  for p in range(pairs)], axis=0)
            s = lax.dot_general(q4, kbd, (((1,), (1,)), ((), ())), preferred_element_type=F32)
            s = s + bias
            p_parts, inv_parts = [], []
            for p in range(pairs):
                halves, invs = [], []
                for hf in range(2):
                    sk = sink_ref[(kvh * pairs + p) * 2 + hf]
                    sh = s[p * uq:(p + 1) * uq, hf * KEY_WIN:(hf + 1) * KEY_WIN]
                    m = jnp.maximum(jnp.max(sh, axis=-1, keepdims=True), sk)
                    pe = jnp.exp(sh - m)
                    den = jnp.sum(pe, axis=-1, keepdims=True) + jnp.exp(sk - m)
                    halves.append(pe.astype(BF16))
                    invs.append(1.0 / den)
                p_parts.append(jnp.concatenate(halves, axis=1))
                inv_parts.append(invs)
            pm = jnp.concatenate(p_parts, axis=0)
            o4 = jnp.dot(pm, vbd, preferred_element_type=F32)
            for p in range(pairs):
                inv = jnp.where(lane_o < HEAD_DIM, inv_parts[p][0], inv_parts[p][1])
                hp = kvh * pairs + p
                obuf[r0:r0 + uq, hp * LANES:(hp + 1) * LANES] = (o4[p * uq:(p + 1) * uq, :] * inv).astype(BF16)
    mix = jnp.dot(obuf[...], wo_ref[...], preferred_element_type=F32)
    _post_mixer(x, mix, *post_w, x1_ref, info_ref, er_ref, cnt_out_ref, cnt_ref)


N_ATTN_IN = 9


def _attn_prompt_kernel(*refs):
    ins, post_w, rest = refs[:N_ATTN_IN], refs[N_ATTN_IN:N_ATTN_IN + N_POST_W], refs[N_ATTN_IN + N_POST_W:]
    _attn_mixer_body(*ins, post_w, None, *rest, nunit=TM // UQ_PROMPT, uq=UQ_PROMPT, prompt=True)


def _attn_sample_kernel(*refs):
    ins, post_w = refs[:N_ATTN_IN], refs[N_ATTN_IN:N_ATTN_IN + N_POST_W]
    cnt_in_ref, rest = refs[N_ATTN_IN + N_POST_W], refs[N_ATTN_IN + N_POST_W + 1 + N_ALIASED:]
    _attn_mixer_body(*ins, post_w, cnt_in_ref, *rest, nunit=TM // DEC_SEQ, uq=DEC_SEQ, prompt=False)


def _attn_mixer(x, sinks, wq, wo, post_w, rope_p, rope_s, kv_p, kv_s):
    tiles_per_seq = SEQ // TM
    smem = pl.BlockSpec(memory_space=pltpu.SMEM)
    tail_specs = [_full(wo.shape)] + [_full(a.shape) for a in post_w]
    kd, vd = kv_p
    ptile = lambda b, t: b * tiles_per_seq + t
    pair_rows = N_HEADS // N_KV_HEADS // 2
    x1, info, er, cnt = pl.pallas_call(
        _attn_prompt_kernel,
        grid=(BATCH, tiles_per_seq),
        in_specs=[smem, pl.BlockSpec((TM, D_MODEL), lambda b, t: (ptile(b, t), 0)), _full(wq.shape)]
                 + [pl.BlockSpec((TM, LANES), lambda b, t: (t, 0))] * 3
                 + [pl.BlockSpec((1,) + kd.shape[1:], lambda b, t: (b, 0, 0, 0))] * 2 + tail_specs,
        out_specs=_token_out_specs(ptile) + [_full(CNT_SHAPE.shape)],
        out_shape=TOKEN_OUT_SHAPES + [CNT_SHAPE],
        scratch_shapes=[CNT_SCRATCH, pltpu.VMEM((TM, D_MODEL), BF16),
                        pltpu.VMEM((2, pair_rows * UQ_PROMPT, 2 * KEY_WIN), F32)],
        compiler_params=pltpu.CompilerParams(dimension_semantics=("arbitrary", "arbitrary"),
                                             vmem_limit_bytes=VMEM_LIMIT),
        name="attn_mixer_prompt",
    )(sinks, x, wq, *rope_p, kd, vd, wo, *post_w)
    nb = TM // DEC_SEQ
    kd, vd = kv_s
    n_in = N_ATTN_IN + N_POST_W + 1
    x1, info, er, cnt = pl.pallas_call(
        _attn_sample_kernel,
        grid=(T_SAMPLE // TM,),
        in_specs=[smem, pl.BlockSpec((TM, D_MODEL), lambda i: (PROMPT_TILES + i, 0)), _full(wq.shape)]
                 + [_full((TM, LANES))] * 3
                 + [pl.BlockSpec((nb,) + kd.shape[1:], lambda i: (i, 0, 0, 0))] * 2 + tail_specs
                 + [_full(CNT_SHAPE.shape)] + [ANY_SPEC] * N_ALIASED,
        out_specs=_token_out_specs(lambda i: PROMPT_TILES + i) + [_full(CNT_SHAPE.shape)],
        out_shape=TOKEN_OUT_SHAPES + [CNT_SHAPE],
        scratch_shapes=[CNT_SCRATCH, pltpu.VMEM((TM, D_MODEL), BF16),
                        pltpu.VMEM((1, pair_rows * DEC_SEQ, 2 * KEY_WIN), F32)],
        input_output_aliases={n_in + k: k for k in range(N_ALIASED)},
        compiler_params=pltpu.CompilerParams(dimension_semantics=("arbitrary",),
                                             vmem_limit_bytes=VMEM_LIMIT),
        name="attn_mixer_sample",
    )(sinks, x, wq, *rope_s, kd, vd, wo, *post_w, cnt, x1, info, er)
    return x1, info, er, cnt


N_TILES = T_ALL // TM


def _row_copy_wait(src_rows, dst_rows, sem):
    pltpu.make_async_copy(src_rows, dst_rows, sem).wait()


POS_BLOCK = (1, 2, TM)


def _dispatch_kernel(zrow_ref, zflag_ref, pos_ref, x1_ref, xs_hbm, zbuf, sem, zsem):
    i = pl.program_id(0)

    @pl.when(i == 0)
    def _():
        zbuf[...] = jnp.zeros_like(zbuf)

        def zero_tile(e):
            return pltpu.make_async_copy(zbuf, xs_hbm.at[pl.ds(pl.multiple_of(zrow_ref[e], MOE_TM), MOE_TM)], zsem)

        for e in range(N_EXPERTS):
            @pl.when(zflag_ref[e] > 0)
            def _():
                zero_tile(e).start()
        for e in range(N_EXPERTS):
            @pl.when(zflag_ref[e] > 0)
            def _():
                zero_tile(e).wait()

    for j in range(TM):
        for slot in range(2):
            pltpu.make_async_copy(x1_ref.at[pl.ds(j, 1)],
                                  xs_hbm.at[pl.ds(pos_ref[0, slot, j], 1)], sem).start()
    for slot in range(2):
        _row_copy_wait(x1_ref, xs_hbm.at[pl.ds(0, TM)], sem)


def _dispatch(x1, pos, zrow, zflag):
    return pl.pallas_call(
        _dispatch_kernel,
        grid_spec=pltpu.PrefetchScalarGridSpec(
            num_scalar_prefetch=2,
            grid=(N_TILES,),
            in_specs=[pl.BlockSpec(POS_BLOCK, lambda i, *_: (i, 0, 0), memory_space=pltpu.SMEM),
                      pl.BlockSpec((TM, D_MODEL), lambda i, *_: (i, 0))],
            out_specs=ANY_SPEC,
            scratch_shapes=[pltpu.VMEM((MOE_TM, D_MODEL), F32), pltpu.SemaphoreType.DMA,
                            pltpu.SemaphoreType.DMA],
        ),
        out_shape=jax.ShapeDtypeStruct((MOE_ROWS, D_MODEL), F32),
        compiler_params=pltpu.CompilerParams(dimension_semantics=("arbitrary",)),
        name="moe_dispatch",
    )(zrow, zflag, pos, x1)


def _moe_kernel(te_ref, slot_ref, next_ref, nv_ref, xs_ref, wg_hbm, wu_hbm, wd_hbm, ys_ref,
                wg_st, wu_st, wd_st, wgu_bf, wd_bf, sem, *, base):
    i = pl.program_id(0)

    def fetch(e, slot):
        return (pltpu.make_async_copy(wg_hbm.at[base + e], wg_st.at[slot], sem.at[slot, 0]),
                pltpu.make_async_copy(wu_hbm.at[base + e], wu_st.at[slot], sem.at[slot, 1]),
                pltpu.make_async_copy(wd_hbm.at[base + e], wd_st.at[slot], sem.at[slot, 2]))

    @pl.when(i == 0)
    def _():
        for c in fetch(te_ref[0], slot_ref[0]):
            c.start()

    @pl.when(i < nv_ref[0])
    def _():
        e = te_ref[i]
        slot = slot_ref[i]

        @pl.when((i == 0) | (e != te_ref[jnp.maximum(i - 1, 0)]))
        def _():
            for c in fetch(e, slot):
                c.wait()
            wgu_bf[:, :EXPERT_FF] = wg_st[slot].astype(BF16)
            wgu_bf[:, EXPERT_FF:] = wu_st[slot].astype(BF16)
            wd_bf[...] = wd_st[slot].astype(BF16)

            @pl.when(next_ref[i] >= 0)
            def _():
                for c in fetch(next_ref[i], 1 - slot):
                    c.start()

        hgu = jnp.dot(xs_ref[...].astype(BF16), wgu_bf[...], preferred_element_type=F32)
        hg = hgu[:, :EXPERT_FF]
        h = hg * jax.nn.sigmoid(hg) * hgu[:, EXPERT_FF:]
        ys_ref[...] = jnp.dot(h.astype(BF16), wd_bf[...], preferred_element_type=F32)


def _moe_experts(xs, tile_expert, tile_slot, tile_next, n_valid, wg, wu, wd, layer):
    row = lambda i, te, sl, nx, nv: (jnp.minimum(i, nv[0] - 1), 0)
    return pl.pallas_call(
        functools.partial(_moe_kernel, base=layer * N_EXPERTS),
        grid_spec=pltpu.PrefetchScalarGridSpec(
            num_scalar_prefetch=4,
            grid=(MOE_TILES,),
            in_specs=[pl.BlockSpec((MOE_TM, D_MODEL), row), ANY_SPEC, ANY_SPEC, ANY_SPEC],
            out_specs=pl.BlockSpec((MOE_TM, D_MODEL), row),
            scratch_shapes=[pltpu.VMEM((2, D_MODEL, EXPERT_FF), F32), pltpu.VMEM((2, D_MODEL, EXPERT_FF), F32),
                            pltpu.VMEM((2, EXPERT_FF, D_MODEL), F32),
                            pltpu.VMEM((D_MODEL, 2 * EXPERT_FF), BF16), pltpu.VMEM((EXPERT_FF, D_MODEL), BF16),
                            pltpu.SemaphoreType.DMA((2, 3))],
        ),
        out_shape=jax.ShapeDtypeStruct((MOE_ROWS, D_MODEL), F32),
        compiler_params=pltpu.CompilerParams(dimension_semantics=("arbitrary",), vmem_limit_bytes=VMEM_LIMIT),
        name="moe_experts",
    )(tile_expert, tile_slot, tile_next, n_valid, xs, wg, wu, wd)


def _ffn_ln_kernel(pos_ref, pos_next_ref, x1_ref, info_ref, g_ref, b_ref, *rest, split, with_kv):
    kv_in, rest = (rest[:4], rest[4:]) if with_kv else ((), rest)
    ys_hbm, out_refs, (ybuf, sem) = rest[0], rest[1:-2], rest[-2:]
    i = pl.program_id(0)

    def issue(p_ref, par):
        for j in range(TM):
            for slot in range(2):
                pltpu.make_async_copy(ys_hbm.at[pl.ds(p_ref[0, slot, j], 1)],
                                      ybuf.at[par, slot, pl.ds(j, 1)], sem.at[par]).start()

    @pl.when(i == 0)
    def _():
        issue(pos_ref, 0)

    @pl.when(i + 1 < N_TILES)
    def _():
        issue(pos_next_ref, (i + 1) % 2)

    par = i % 2
    for slot in range(2):
        _row_copy_wait(ys_hbm.at[pl.ds(0, TM)], ybuf.at[par, slot], sem.at[par])
    info = info_ref[...]
    f = info[:, 2:3] * ybuf[par, 0] + info[:, 3:4] * ybuf[par, 1]
    x2 = _layer_norm(DEEPNORM_ALPHA * x1_ref[...] + f, g_ref[...], b_ref[...])
    if with_kv:
        wkv_ref, c_ref, sa_ref, sb_ref = kv_in
        k_ref, v_ref = out_refs[-2:]
        kv = jnp.dot(x2.astype(BF16), wkv_ref[...], preferred_element_type=F32)
        k_ref[...] = _rope_lanes(kv[:, :KV_WIDTH], c_ref[...], sa_ref[...], sb_ref[...])
        v_ref[...] = kv[:, KV_WIDTH:]
    if split:
        @pl.when(i < PROMPT_TILES)
        def _():
            out_refs[0][...] = x2

        @pl.when(i >= PROMPT_TILES)
        def _():
            out_refs[1][...] = x2
    else:
        out_refs[0][...] = x2


def _ffn_ln(x1, ys, pos, info, g, b, split, kv_w=None):
    tok = lambda i: (i, 0)
    pos_spec = lambda fn: pl.BlockSpec(POS_BLOCK, fn, memory_space=pltpu.SMEM)
    if split:
        out_specs = [pl.BlockSpec((TM, D_MODEL), lambda i: (jnp.minimum(i, PROMPT_TILES - 1), 0)),
                     pl.BlockSpec((TM, D_MODEL), lambda i: (jnp.maximum(i - PROMPT_TILES, 0), 0))]
        out_shape = [jax.ShapeDtypeStruct((T_PROMPT, D_MODEL), F32), jax.ShapeDtypeStruct((T_SAMPLE, D_MODEL), F32)]
    else:
        out_specs = [pl.BlockSpec((TM, D_MODEL), tok)]
        out_shape = [jax.ShapeDtypeStruct((T_ALL, D_MODEL), F32)]
    kv_args, kv_specs = (), []
    if kv_w is not None:
        wkv, rope_all = kv_w
        kv_args = (wkv,) + tuple(rope_all)
        kv_specs = [_full(wkv.shape)] + [pl.BlockSpec((TM, LANES), tok)] * 3
        out_specs = out_specs + [pl.BlockSpec((TM, KV_WIDTH), tok)] * 2
        out_shape = out_shape + [jax.ShapeDtypeStruct((T_ALL, KV_WIDTH), F32)] * 2
    return pl.pallas_call(
        functools.partial(_ffn_ln_kernel, split=split, with_kv=kv_w is not None),
        grid=(N_TILES,),
        in_specs=[pos_spec(lambda i: (i, 0, 0)), pos_spec(lambda i: (jnp.minimum(i + 1, N_TILES - 1), 0, 0)),
                  pl.BlockSpec((TM, D_MODEL), tok), pl.BlockSpec((TM, LANES), tok), _full(g.shape), _full(b.shape)]
                 + kv_specs + [ANY_SPEC],
        out_specs=out_specs,
        out_shape=out_shape,
        scratch_shapes=[pltpu.VMEM((2, 2, TM, D_MODEL), F32), pltpu.SemaphoreType.DMA((2,))],
        compiler_params=pltpu.CompilerParams(dimension_semantics=("arbitrary",)),
        name="ffn_ln",
    )(pos, pos, x1, info, g, b, *kv_args, ys)


def _moe_layer(x1, info, er, cnt, wg, wu, wd, layer, g, b, split, kv_w):
    counts = cnt[0, :N_EXPERTS].astype(jnp.int32)
    tiles_per = (counts + MOE_TM - 1) // MOE_TM
    tile_end = jnp.cumsum(tiles_per)
    pad_start = (tile_end - tiles_per) * MOE_TM
    ids = jnp.arange(N_EXPERTS, dtype=jnp.int32)
    half = ER_ROWS // 2
    hit = er[None, :, 0:2, :] == ids[:, None, None, None]
    pos = jnp.sum(jnp.where(hit, pad_start[:, None, None, None], 0), axis=0) + er[:, half:half + 2, :]
    n_valid = tile_end[-1:]
    tile_ids = jnp.minimum(jnp.arange(MOE_TILES, dtype=jnp.int32), n_valid[0] - 1)
    tile_expert = jnp.sum(tile_ids[:, None] >= tile_end[None, :], axis=1, dtype=jnp.int32)
    nonempty = tiles_per > 0
    slot_of = (jnp.cumsum(nonempty.astype(jnp.int32)) - 1) % 2
    later = jnp.where(nonempty[None, :] & (ids[None, :] > ids[:, None]), ids[None, :], N_EXPERTS)
    next_of = jnp.min(later, axis=1)
    next_of = jnp.where(next_of < N_EXPERTS, next_of, -1)
    zrow = jnp.maximum(tile_end - 1, 0) * MOE_TM
    mine = tile_expert[:, None] == ids[None, :]
    tile_slot = jnp.sum(jnp.where(mine, slot_of[None, :], 0), axis=1)
    tile_next = jnp.sum(jnp.where(mine, next_of[None, :], 0), axis=1)
    xs = _dispatch(x1, pos, zrow, tiles_per)
    ys = _moe_experts(xs, tile_expert, tile_slot, tile_next, n_valid, wg, wu, wd, layer)
    return _ffn_ln(x1, ys, pos, info, g, b, split, kv_w)


def _rope_tables(pos):
    half = ROT_DIM // 2
    inv_freq = ROPE_THETA ** (-jnp.arange(0, ROT_DIM, 2, dtype=F32) / ROT_DIM)
    ang = pos.astype(F32)[:, None] * inv_freq[None, :]
    cos, sin = jnp.cos(ang), jnp.sin(ang)
    n = pos.shape[0]
    ones = jnp.ones((n, HEAD_DIM - ROT_DIM), F32)
    zeros_h = jnp.zeros((n, half), F32)
    zeros_r = jnp.zeros((n, HEAD_DIM - ROT_DIM), F32)
    c = jnp.concatenate([cos, cos, ones], axis=1)
    sa = jnp.concatenate([-sin, zeros_h, zeros_r], axis=1)
    sb = jnp.concatenate([zeros_h, sin, zeros_r], axis=1)
    rep = LANES // HEAD_DIM
    return tuple(jnp.tile(t, (1, rep)) for t in (c, sa, sb))


def _dup_heads(kv):
    b, l, _ = kv.shape
    h = kv.reshape(b, l, N_KV_HEADS, HEAD_DIM).transpose(0, 2, 1, 3)
    return jnp.concatenate([h, h], axis=-1).astype(BF16)


def kernel(x_prompt, x_sample, state_conv, cache_k, cache_v, ln_mix_g, ln_mix_b, ln_ffn_g, ln_ffn_b, conv_w_in, conv_w_dw, conv_ln_g, conv_ln_b, conv_w_out, w_kv, attn_w_q, attn_sinks, attn_w_o, router_w_group, router_b_group, router_w_expert, router_b_expert, expert_w_gate, expert_w_up, expert_w_down):
    x = None
    wg = expert_w_gate.reshape(DEPTH * N_EXPERTS, D_MODEL, EXPERT_FF)
    wu = expert_w_up.reshape(DEPTH * N_EXPERTS, D_MODEL, EXPERT_FF)
    wd = expert_w_down.reshape(DEPTH * N_EXPERTS, EXPERT_FF, D_MODEL)
    row = lambda v: v.reshape(1, -1)

    def router_weights(layer):
        w = jnp.concatenate([router_w_group[layer], router_w_expert[layer]], axis=1)
        w = jnp.pad(w, ((0, 0), (0, LANES - w.shape[1])))
        w_hi = w.astype(BF16)
        w_lo = (w - w_hi.astype(F32)).astype(BF16)
        bias = jnp.concatenate([router_b_group[layer], router_b_expert[layer].reshape(-1)])
        return w_hi, w_lo, jnp.pad(bias, (0, LANES - bias.shape[0])).reshape(1, LANES)

    pos_p = jnp.arange(SEQ, dtype=jnp.int32)
    pos_s = PAST_LEN + jnp.arange(DEC_SEQ, dtype=jnp.int32)
    rope_p = _rope_tables(pos_p)
    rope_s1 = _rope_tables(pos_s)
    rope_s = tuple(jnp.tile(t, (TM // DEC_SEQ, 1)) for t in rope_s1)
    rope_all = tuple(jnp.concatenate([jnp.tile(p, (BATCH, 1)), jnp.tile(s, (DEC_BATCH, 1))], axis=0)
                     for p, s in zip(rope_p, rope_s1))

    tails, us = [], []
    kv_p = kv_s = None
    outs_kv = None
    for layer in range(DEPTH):
        post_w = (row(ln_mix_g[layer]), row(ln_mix_b[layer])) + router_weights(layer)
        if layer < N_A_LAYERS:
            st = jnp.pad(state_conv[layer], ((0, 0), (STATE_PAD - (CONV_WIDTH - 1), 0), (0, 0)))
            wdw = jnp.pad(conv_w_dw[layer], ((0, STATE_PAD - CONV_WIDTH), (0, 0)))
            wdw = wdw.reshape(STATE_PAD, N_LANE_CHUNKS, LANES).transpose(1, 0, 2)
            w = (conv_w_in[layer].astype(BF16), wdw, row(conv_ln_g[layer]), row(conv_ln_b[layer]),
                 conv_w_out[layer].astype(BF16)) + post_w
            if layer == 0:
                x_in = (x_prompt.reshape(T_PROMPT, D_MODEL), x_sample.reshape(T_SAMPLE, D_MODEL), 0)
            else:
                x_in = (x, x, PROMPT_TILES)
            x1, info, er, cnt, tail, u_s = _conv_mixer(*x_in, st, w)
            tails.append(tail[:, STATE_PAD - (CONV_WIDTH - 1):, :])
            us.append(u_s.reshape(DEC_BATCH, DEC_SEQ, D_MODEL)[:, DEC_SEQ - (CONV_WIDTH - 1):, :])
        else:
            j = layer - N_A_LAYERS
            x1, info, er, cnt = _attn_mixer(x, attn_sinks[j], attn_w_q[j].astype(BF16), attn_w_o[j].astype(BF16),
                                        post_w, rope_p, rope_s, kv_p, kv_s)
        last_a = layer == N_A_LAYERS - 1
        outs = _moe_layer(x1, info, er, cnt, wg, wu, wd, layer, row(ln_ffn_g[layer]), row(ln_ffn_b[layer]),
                          split=layer == DEPTH - 1, kv_w=(w_kv.astype(BF16), rope_all) if last_a else None)
        x = outs if layer == DEPTH - 1 else outs[0]
        if last_a:
            k, v = outs[1:]
            k_p = k[:T_PROMPT].reshape(BATCH, SEQ, KV_WIDTH)
            v_p = v[:T_PROMPT].reshape(BATCH, SEQ, KV_WIDTH)
            k_s = jnp.concatenate([cache_k.reshape(DEC_BATCH, WINDOW, KV_WIDTH),
                                   k[T_PROMPT:].reshape(DEC_BATCH, DEC_SEQ, KV_WIDTH)], axis=1)
            v_s = jnp.concatenate([cache_v.reshape(DEC_BATCH, WINDOW, KV_WIDTH),
                                   v[T_PROMPT:].reshape(DEC_BATCH, DEC_SEQ, KV_WIDTH)], axis=1)
            cache_shape = (-1, WINDOW, N_KV_HEADS, HEAD_DIM)
            outs_kv = (k_p[:, -WINDOW:].reshape(cache_shape), v_p[:, -WINDOW:].reshape(cache_shape),
                       k_s[:, -WINDOW:].reshape(cache_shape), v_s[:, -WINDOW:].reshape(cache_shape))
            front = ((0, 0), (WINDOW, 0), (0, 0))
            back = ((0, 0), (0, KEY_WIN - WINDOW - DEC_SEQ), (0, 0))
            kv_p = (_dup_heads(jnp.pad(k_p, front)), _dup_heads(jnp.pad(v_p, front)))
            kv_s = (_dup_heads(jnp.pad(k_s, back)), _dup_heads(jnp.pad(v_s, back)))
    y_prompt = x[0].reshape(BATCH, SEQ, D_MODEL)
    y_sample = x[1].reshape(DEC_BATCH, DEC_SEQ, D_MODEL)
    return (y_prompt, y_sample, jnp.stack(tails, axis=0), jnp.stack(us, axis=0)) + outs_kv
```

```python
import functools

import jax
import jax.numpy as jnp
from jax import lax
from jax.experimental import pallas as pl
from jax.experimental.pallas import tpu as pltpu

D_MODEL = 1024
BATCH = 2
SEQ = 8192
DEPTH = 4
DEC_BATCH = 32
DEC_SEQ = 32
PAST_LEN = 2048
CHUNK = 64
N_A_LAYERS = DEPTH // 2
CONV_WIDTH = 31
N_HEADS = 16
N_KV_HEADS = 2
HEAD_DIM = 64
KV_WIDTH = N_KV_HEADS * HEAD_DIM
WINDOW = 128
ROT_DIM = HEAD_DIM // 4
ROPE_THETA = 500000.0
N_GROUPS = 4
EXPERTS_PER_GROUP = 8
N_EXPERTS = N_GROUPS * EXPERTS_PER_GROUP
EXPERT_FF = D_MODEL // 2
DEEPNORM_ALPHA = (2 * DEPTH) ** 0.25
LN_EPS = 1e-5
ATTN_SCALE = HEAD_DIM ** -0.5
NEG_INF = -1e30

LANES = 128
N_LANE_CHUNKS = D_MODEL // LANES
T_PROMPT = BATCH * SEQ
T_SAMPLE = DEC_BATCH * DEC_SEQ
T_ALL = T_PROMPT + T_SAMPLE
TM = 256
STATE_PAD = 32
KEY_WIN = 256
UQ_PROMPT = 2 * CHUNK
MOE_TM = 256
MOE_ROWS = 2 * T_ALL + N_EXPERTS * MOE_TM
MOE_TILES = MOE_ROWS // MOE_TM
VMEM_LIMIT = 56 * 1024 * 1024

F32 = jnp.float32
BF16 = jnp.bfloat16


def _layer_norm(x, g, b):
    mu = jnp.mean(x, axis=-1, keepdims=True)
    xc = x - mu
    var = jnp.mean(xc * xc, axis=-1, keepdims=True)
    return xc * lax.rsqrt(var + LN_EPS) * g + b


def _router(x1, rwh_ref, rwl_ref, rb_ref, cnt_ref):
    x_hi = x1.astype(BF16)
    x_lo = (x1 - x_hi.astype(F32)).astype(BF16)
    w_hi = rwh_ref[...]
    logits = (jnp.dot(x_hi, w_hi, preferred_element_type=F32)
              + jnp.dot(x_hi, rwl_ref[...], preferred_element_type=F32)
              + jnp.dot(x_lo, w_hi, preferred_element_type=F32)) + rb_ref[...]
    lane_i = lax.broadcasted_iota(jnp.int32, logits.shape, 1)
    lane = lane_i.astype(F32)
    neg = jnp.float32(-3.0e38)
    none = jnp.float32(LANES)
    gl = jnp.where(lane_i < N_GROUPS, logits, neg)
    gmax = jnp.max(gl, axis=-1, keepdims=True)
    g_idx = jnp.min(jnp.where(gl == gmax, lane, none), axis=-1, keepdims=True)
    gsum = jnp.sum(jnp.where(lane_i < N_GROUPS, jnp.exp(gl - gmax), 0.0), axis=-1, keepdims=True)
    p_grp = 1.0 / gsum
    lo = N_GROUPS + EXPERTS_PER_GROUP * g_idx
    el = jnp.where(lane >= lo, jnp.where(lane < lo + EXPERTS_PER_GROUP, logits, neg), neg)
    l1 = jnp.max(el, axis=-1, keepdims=True)
    i1 = jnp.min(jnp.where(el == l1, lane, none), axis=-1, keepdims=True)
    el2 = jnp.where(lane == i1, neg, el)
    l2 = jnp.max(el2, axis=-1, keepdims=True)
    i2 = jnp.min(jnp.where(el2 == l2, lane, none), axis=-1, keepdims=True)
    t = jnp.exp(l2 - l1)
    inv = p_grp / (1.0 + t)
    e1 = i1 - N_GROUPS
    e2 = i2 - N_GROUPS
    rows = logits.shape[0]
    onehot = jnp.where(lane == e1, 1.0, jnp.where(lane == e2, 1.0, 0.0))
    tri = (lax.broadcasted_iota(jnp.int32, (rows, rows), 0) > lax.broadcasted_iota(jnp.int32, (rows, rows), 1))
    before = jnp.dot(jnp.where(tri, 1.0, 0.0).astype(BF16), onehot.astype(BF16),
                     preferred_element_type=F32) + cnt_ref[...]
    r1 = jnp.sum(jnp.where(lane == e1, before, 0.0), axis=-1, keepdims=True)
    r2 = jnp.sum(jnp.where(lane == e2, before, 0.0), axis=-1, keepdims=True)
    cnt_ref[...] = cnt_ref[...] + jnp.sum(onehot, axis=0, keepdims=True)
    out = jnp.where(lane_i == 4, r1, jnp.where(lane_i == 5, r2, 0.0))
    out = jnp.where(lane_i == 2, inv, jnp.where(lane_i == 3, inv * t, out))
    return jnp.where(lane_i == 0, e1, jnp.where(lane_i == 1, e2, out))


ER_ROWS = 8


def _post_mixer(x, mix, lg_ref, lb_ref, rwh_ref, rwl_ref, rb_ref, x1_ref, info_ref, er_ref, cnt_out_ref, cnt_ref):
    x1 = _layer_norm(DEEPNORM_ALPHA * x + mix, lg_ref[...], lb_ref[...])
    x1_ref[...] = x1
    info = _router(x1, rwh_ref, rwl_ref, rb_ref, cnt_ref)
    info_ref[...] = info
    er_ref[0] = jnp.transpose(info)[0:ER_ROWS, :].astype(jnp.int32)
    cnt_out_ref[...] = cnt_ref[...]


def _init_counts(cnt_ref, first_step, cnt_in_ref):
    @pl.when(first_step)
    def _():
        cnt_ref[...] = jnp.zeros_like(cnt_ref) if cnt_in_ref is None else cnt_in_ref[...]


def _conv_mixer_body(x_ref, st_ref, win_ref, wdw_ref, cg_ref, cb_ref, wout_ref, post_w, cnt_in_ref,
                     x1_ref, info_ref, er_ref, cnt_out_ref, u_ref, cnt_ref, ubuf, cbuf, *, nseg, seg):
    carry = st_ref is None
    first_step = (pl.program_id(0) == 0) & (pl.program_id(1) == 0) if carry else pl.program_id(0) == 0
    _init_counts(cnt_ref, first_step, cnt_in_ref)
    x = x_ref[...]
    h = jnp.dot(x.astype(BF16), win_ref[...], preferred_element_type=F32)
    u = h[:, :D_MODEL] * jax.nn.sigmoid(h[:, D_MODEL:])
    if carry:
        @pl.when(pl.program_id(1) == 0)
        def _():
            ubuf[:, 0, 0:STATE_PAD, :] = jnp.zeros((N_LANE_CHUNKS, STATE_PAD, LANES), F32)
        u_ref[0] = u[seg - STATE_PAD:, :]
    else:
        u_ref[...] = u
    for lc in range(N_LANE_CHUNKS):
        for s in range(nseg):
            ubuf[lc, s, STATE_PAD:STATE_PAD + seg, :] = u[s * seg:(s + 1) * seg, lc * LANES:(lc + 1) * LANES]
            if not carry:
                ubuf[lc, s, 0:STATE_PAD, :] = st_ref[s, :, lc * LANES:(lc + 1) * LANES]

    def conv_lane_chunk(lc, c):
        w = wdw_ref[lc]
        for s in range(nseg):
            acc = jnp.zeros((seg, LANES), F32)
            for k in range(CONV_WIDTH):
                off = k + STATE_PAD - (CONV_WIDTH - 1)
                acc = acc + w[k:k + 1, :] * ubuf[lc, s, off:off + seg, :]
            cbuf[lc, s * seg:(s + 1) * seg, :] = acc
        return c

    lax.fori_loop(0, N_LANE_CHUNKS, conv_lane_chunk, 0)
    if carry:
        for lc in range(N_LANE_CHUNKS):
            ubuf[lc, 0, 0:STATE_PAD, :] = u[seg - STATE_PAD:, lc * LANES:(lc + 1) * LANES]
    dw = jnp.concatenate([cbuf[lc] for lc in range(N_LANE_CHUNKS)], axis=1)
    sw = _layer_norm(dw, cg_ref[...], cb_ref[...])
    sw = sw * jax.nn.sigmoid(sw)
    mix = jnp.dot(sw.astype(BF16), wout_ref[...], preferred_element_type=F32)
    _post_mixer(x, mix, *post_w, x1_ref, info_ref, er_ref, cnt_out_ref, cnt_ref)


N_POST_W = 5


def _conv_prompt_kernel(x_ref, win_ref, wdw_ref, cg_ref, cb_ref, wout_ref, *rest):
    post_w, rest = rest[:N_POST_W], rest[N_POST_W:]
    _conv_mixer_body(x_ref, None, win_ref, wdw_ref, cg_ref, cb_ref, wout_ref, post_w, None, *rest,
                     nseg=1, seg=TM)


def _conv_sample_kernel(x_ref, st_ref, win_ref, wdw_ref, cg_ref, cb_ref, wout_ref, *rest):
    post_w, cnt_in_ref, rest = rest[:N_POST_W], rest[N_POST_W], rest[N_POST_W + 1 + N_ALIASED:]
    _conv_mixer_body(x_ref, st_ref, win_ref, wdw_ref, cg_ref, cb_ref, wout_ref, post_w, cnt_in_ref, *rest,
                     nseg=TM // DEC_SEQ, seg=DEC_SEQ)


def _full(shape):
    return pl.BlockSpec(shape, lambda *_: (0,) * len(shape))


ANY_SPEC = pl.BlockSpec(memory_space=pl.ANY)
CNT_SHAPE = jax.ShapeDtypeStruct((1, LANES), F32)
CNT_SCRATCH = pltpu.VMEM((1, LANES), F32)
PROMPT_TILES = T_PROMPT // TM
N_ALIASED = 3
TOKEN_OUT_SHAPES = [jax.ShapeDtypeStruct((T_ALL, D_MODEL), F32), jax.ShapeDtypeStruct((T_ALL, LANES), F32),
                    jax.ShapeDtypeStruct((T_ALL // TM, ER_ROWS, TM), jnp.int32)]


def _token_out_specs(tile):
    return [pl.BlockSpec((TM, D_MODEL), lambda *g: (tile(*g), 0)), pl.BlockSpec((TM, LANES), lambda *g: (tile(*g), 0)),
            pl.BlockSpec((1, ER_ROWS, TM), lambda *g: (tile(*g), 0, 0))]


def _conv_mixer(x_p, x_s, s_first, state_pad, w):
    w_specs = [_full(a.shape) for a in w]
    tiles_per_seq = SEQ // TM
    ptile = lambda b, t: b * tiles_per_seq + t
    x1, info, er, cnt, tail = pl.pallas_call(
        _conv_prompt_kernel,
        grid=(BATCH, tiles_per_seq),
        in_specs=[pl.BlockSpec((TM, D_MODEL), lambda b, t: (ptile(b, t), 0))] + w_specs,
        out_specs=_token_out_specs(ptile) + [_full(CNT_SHAPE.shape),
                                             pl.BlockSpec((1, STATE_PAD, D_MODEL), lambda b, t: (b, 0, 0))],
        out_shape=TOKEN_OUT_SHAPES + [CNT_SHAPE, jax.ShapeDtypeStruct((BATCH, STATE_PAD, D_MODEL), F32)],
        scratch_shapes=[CNT_SCRATCH, pltpu.VMEM((N_LANE_CHUNKS, 1, STATE_PAD + TM, LANES), F32),
                        pltpu.VMEM((N_LANE_CHUNKS, TM, LANES), F32)],
        compiler_params=pltpu.CompilerParams(dimension_semantics=("arbitrary", "arbitrary"),
                                             vmem_limit_bytes=VMEM_LIMIT),
        name="conv_mixer_prompt",
    )(x_p, *w)
    nb = TM // DEC_SEQ
    n_in = 2 + len(w) + 1
    x1, info, er, cnt, u_s = pl.pallas_call(
        _conv_sample_kernel,
        grid=(T_SAMPLE // TM,),
        in_specs=[pl.BlockSpec((TM, D_MODEL), lambda i: (s_first + i, 0)),
                  pl.BlockSpec((nb, STATE_PAD, D_MODEL), lambda i: (i, 0, 0))]
                 + w_specs + [_full(CNT_SHAPE.shape)] + [ANY_SPEC] * N_ALIASED,
        out_specs=_token_out_specs(lambda i: PROMPT_TILES + i)
                  + [_full(CNT_SHAPE.shape), pl.BlockSpec((TM, D_MODEL), lambda i: (i, 0))],
        out_shape=TOKEN_OUT_SHAPES + [CNT_SHAPE, jax.ShapeDtypeStruct((T_SAMPLE, D_MODEL), F32)],
        scratch_shapes=[CNT_SCRATCH, pltpu.VMEM((N_LANE_CHUNKS, nb, STATE_PAD + DEC_SEQ, LANES), F32),
                        pltpu.VMEM((N_LANE_CHUNKS, TM, LANES), F32)],
        input_output_aliases={n_in + k: k for k in range(N_ALIASED)},
        compiler_params=pltpu.CompilerParams(dimension_semantics=("arbitrary",),
                                             vmem_limit_bytes=VMEM_LIMIT),
        name="conv_mixer_sample",
    )(x_s, state_pad, *w, cnt, x1, info, er)
    return x1, info, er, cnt, tail, u_s


def _rope_lanes(v, c, sa, sb):
    width = v.shape[-1]
    half = ROT_DIM // 2
    return v * c + pltpu.roll(v, width - half, 1) * sa + pltpu.roll(v, half, 1) * sb


def _attn_mixer_body(sink_ref, x_ref, wq_ref, c_ref, sa_ref, sb_ref, kd_ref, vd_ref, wo_ref, post_w, cnt_in_ref,
                     x1_ref, info_ref, er_ref, cnt_out_ref, cnt_ref, obuf, bias_ref, *, nunit, uq, prompt):
    first_step = (pl.program_id(0) == 0) & (pl.program_id(1) == 0) if prompt else pl.program_id(0) == 0
    _init_counts(cnt_ref, first_step, cnt_in_ref)

    @pl.when(first_step)
    def _():
        shape = bias_ref.shape[1:]
        col = lax.broadcasted_iota(jnp.int32, shape, 1) & (KEY_WIN - 1)
        if prompt:
            chunk_shift = CHUNK.bit_length() - 1
            qchunk = (lax.broadcasted_iota(jnp.int32, shape, 0) & (uq - 1)) >> chunk_shift
            kchunk = col >> chunk_shift
            band = jnp.where(kchunk >= qchunk, jnp.where(kchunk <= qchunk + WINDOW // CHUNK, 0.0, NEG_INF), NEG_INF)
            bias_ref[0] = band
            bias_ref[1] = jnp.where(col >= WINDOW, band, NEG_INF)
        else:
            bias_ref[0] = jnp.where(col < WINDOW + DEC_SEQ, 0.0, NEG_INF)

    x = x_ref[...]
    q = jnp.dot(x.astype(BF16), wq_ref[...], preferred_element_type=F32)
    rep = D_MODEL // LANES
    q = _rope_lanes(q, jnp.tile(c_ref[...], (1, rep)), jnp.tile(sa_ref[...], (1, rep)),
                    jnp.tile(sb_ref[...], (1, rep)))
    qb = (q * ATTN_SCALE).astype(BF16)
    pairs = N_HEADS // N_KV_HEADS // 2
    lane_k = lax.broadcasted_iota(jnp.int32, (KEY_WIN, LANES), 1)
    lane_o = lax.broadcasted_iota(jnp.int32, (uq, LANES), 1)
    for un in range(nunit):
        r0 = un * uq
        if prompt:
            kstart = pl.multiple_of(pl.program_id(1) * (nunit * uq) + r0, LANES)
            bias = bias_ref[jnp.where(kstart == 0, 1, 0)]
        else:
            bias = bias_ref[0]
        for kvh in range(N_KV_HEADS):
            if prompt:
                kw = kd_ref[0, kvh, pl.ds(kstart, KEY_WIN), :]
                vw = vd_ref[0, kvh, pl.ds(kstart, KEY_WIN), :]
            else:
                kw = kd_ref[un, kvh]
                vw = vd_ref[un, kvh]
            zero = jnp.zeros_like(kw)
            kbd = jnp.concatenate([jnp.where(lane_k < HEAD_DIM, kw, zero),
                                   jnp.where(lane_k >= HEAD_DIM, kw, zero)], axis=0)
            vbd = jnp.concatenate([jnp.where(lane_k < HEAD_DIM, vw, zero),
                                   jnp.where(lane_k >= HEAD_DIM, vw, zero)], axis=0)
            q4 = jnp.concatenate([qb[r0:r0 + uq, (kvh * pairs + p) * LANES:(kvh * pairs + p + 1) * LANES]
                                  for p in range(pairs)], axis=0)
            s = lax.dot_general(q4, kbd, (((1,), (1,)), ((), ())), preferred_element_type=F32)
            s = s + bias
            p_parts, inv_parts = [], []
            for p in range(pairs):
                halves, invs = [], []
                for hf in range(2):
                    sk = sink_ref[(kvh * pairs + p) * 2 + hf]
                    sh = s[p * uq:(p + 1) * uq, hf * KEY_WIN:(hf + 1) * KEY_WIN]
                    m = jnp.maximum(jnp.max(sh, axis=-1, keepdims=True), sk)
                    pe = jnp.exp(sh - m)
                    den = jnp.sum(pe, axis=-1, keepdims=True) + jnp.exp(sk - m)
                    halves.append(pe.astype(BF16))
                    invs.append(1.0 / den)
                p_parts.append(jnp.concatenate(halves, axis=1))
                inv_parts.append(invs)
            pm = jnp.concatenate(p_parts, axis=0)
            o4 = jnp.dot(pm, vbd, preferred_element_type=F32)
            for p in range(pairs):
                inv = jnp.where(lane_o < HEAD_DIM, inv_parts[p][0], inv_parts[p][1])
                hp = kvh * pairs + p
                obuf[r0:r0 + uq, hp * LANES:(hp + 1) * LANES] = (o4[p * uq:(p + 1) * uq, :] * inv).astype(BF16)
    mix = jnp.dot(obuf[...], wo_ref[...], preferred_element_type=F32)
    _post_mixer(x, mix, *post_w, x1_ref, info_ref, er_ref, cnt_out_ref, cnt_ref)


N_ATTN_IN = 9


def _attn_prompt_kernel(*refs):
    ins, post_w, rest = refs[:N_ATTN_IN], refs[N_ATTN_IN:N_ATTN_IN + N_POST_W], refs[N_ATTN_IN + N_POST_W:]
    _attn_mixer_body(*ins, post_w, None, *rest, nunit=TM // UQ_PROMPT, uq=UQ_PROMPT, prompt=True)


def _attn_sample_kernel(*refs):
    ins, post_w = refs[:N_ATTN_IN], refs[N_ATTN_IN:N_ATTN_IN + N_POST_W]
    cnt_in_ref, rest = refs[N_ATTN_IN + N_POST_W], refs[N_ATTN_IN + N_POST_W + 1 + N_ALIASED:]
    _attn_mixer_body(*ins, post_w, cnt_in_ref, *rest, nunit=TM // DEC_SEQ, uq=DEC_SEQ, prompt=False)


def _attn_mixer(x, sinks, wq, wo, post_w, rope_p, rope_s, kv_p, kv_s):
    tiles_per_seq = SEQ // TM
    smem = pl.BlockSpec(memory_space=pltpu.SMEM)
    tail_specs = [_full(wo.shape)] + [_full(a.shape) for a in post_w]
    kd, vd = kv_p
    ptile = lambda b, t: b * tiles_per_seq + t
    pair_rows = N_HEADS // N_KV_HEADS // 2
    x1, info, er, cnt = pl.pallas_call(
        _attn_prompt_kernel,
        grid=(BATCH, tiles_per_seq),
        in_specs=[smem, pl.BlockSpec((TM, D_MODEL), lambda b, t: (ptile(b, t), 0)), _full(wq.shape)]
                 + [pl.BlockSpec((TM, LANES), lambda b, t: (t, 0))] * 3
                 + [pl.BlockSpec((1,) + kd.shape[1:], lambda b, t: (b, 0, 0, 0))] * 2 + tail_specs,
        out_specs=_token_out_specs(ptile) + [_full(CNT_SHAPE.shape)],
        out_shape=TOKEN_OUT_SHAPES + [CNT_SHAPE],
        scratch_shapes=[CNT_SCRATCH, pltpu.VMEM((TM, D_MODEL), BF16),
                        pltpu.VMEM((2, pair_rows * UQ_PROMPT, 2 * KEY_WIN), F32)],
        compiler_params=pltpu.CompilerParams(dimension_semantics=("arbitrary", "arbitrary"),
                                             vmem_limit_bytes=VMEM_LIMIT),
        name="attn_mixer_prompt",
    )(sinks, x, wq, *rope_p, kd, vd, wo, *post_w)
    nb = TM // DEC_SEQ
    kd, vd = kv_s
    n_in = N_ATTN_IN + N_POST_W + 1
    x1, info, er, cnt = pl.pallas_call(
        _attn_sample_kernel,
        grid=(T_SAMPLE // TM,),
        in_specs=[smem, pl.BlockSpec((TM, D_MODEL), lambda i: (PROMPT_TILES + i, 0)), _full(wq.shape)]
                 + [_full((TM, LANES))] * 3
                 + [pl.BlockSpec((nb,) + kd.shape[1:], lambda i: (i, 0, 0, 0))] * 2 + tail_specs
                 + [_full(CNT_SHAPE.shape)] + [ANY_SPEC] * N_ALIASED,
        out_specs=_token_out_specs(lambda i: PROMPT_TILES + i) + [_full(CNT_SHAPE.shape)],
        out_shape=TOKEN_OUT_SHAPES + [CNT_SHAPE],
        scratch_shapes=[CNT_SCRATCH, pltpu.VMEM((TM, D_MODEL), BF16),
                        pltpu.VMEM((1, pair_rows * DEC_SEQ, 2 * KEY_WIN), F32)],
        input_output_aliases={n_in + k: k for k in range(N_ALIASED)},
        compiler_params=pltpu.CompilerParams(dimension_semantics=("arbitrary",),
                                             vmem_limit_bytes=VMEM_LIMIT),
        name="attn_mixer_sample",
    )(sinks, x, wq, *rope_s, kd, vd, wo, *post_w, cnt, x1, info, er)
    return x1, info, er, cnt


N_TILES = T_ALL // TM


def _row_copy_wait(src_rows, dst_rows, sem):
    pltpu.make_async_copy(src_rows, dst_rows, sem).wait()


POS_BLOCK = (1, 2, TM)


def _dispatch_kernel(zrow_ref, zflag_ref, pos_ref, x1_ref, xs_hbm, zbuf, sem, zsem):
    i = pl.program_id(0)

    @pl.when(i == 0)
    def _():
        zbuf[...] = jnp.zeros_like(zbuf)

        def zero_tile(e):
            return pltpu.make_async_copy(zbuf, xs_hbm.at[pl.ds(pl.multiple_of(zrow_ref[e], MOE_TM), MOE_TM)], zsem)

        for e in range(N_EXPERTS):
            @pl.when(zflag_ref[e] > 0)
            def _():
                zero_tile(e).start()
        for e in range(N_EXPERTS):
            @pl.when(zflag_ref[e] > 0)
            def _():
                zero_tile(e).wait()

    for j in range(TM):
        for slot in range(2):
            pltpu.make_async_copy(x1_ref.at[pl.ds(j, 1)],
                                  xs_hbm.at[pl.ds(pos_ref[0, slot, j], 1)], sem).start(priority=slot)
    for slot in range(2):
        _row_copy_wait(x1_ref, xs_hbm.at[pl.ds(0, TM)], sem)


def _dispatch(x1, pos, zrow, zflag):
    return pl.pallas_call(
        _dispatch_kernel,
        grid_spec=pltpu.PrefetchScalarGridSpec(
            num_scalar_prefetch=2,
            grid=(N_TILES,),
            in_specs=[pl.BlockSpec(POS_BLOCK, lambda i, *_: (i, 0, 0), memory_space=pltpu.SMEM),
                      pl.BlockSpec((TM, D_MODEL), lambda i, *_: (i, 0))],
            out_specs=ANY_SPEC,
            scratch_shapes=[pltpu.VMEM((MOE_TM, D_MODEL), F32), pltpu.SemaphoreType.DMA,
                            pltpu.SemaphoreType.DMA],
        ),
        out_shape=jax.ShapeDtypeStruct((MOE_ROWS, D_MODEL), F32),
        compiler_params=pltpu.CompilerParams(dimension_semantics=("arbitrary",)),
        name="moe_dispatch",
    )(zrow, zflag, pos, x1)


def _moe_kernel(te_ref, slot_ref, next_ref, nv_ref, xs_ref, wg_hbm, wu_hbm, wd_hbm, ys_ref,
                wg_st, wu_st, wd_st, wgu_bf, wd_bf, sem, *, base):
    i = pl.program_id(0)

    def fetch(e, slot):
        return (pltpu.make_async_copy(wg_hbm.at[base + e], wg_st.at[slot], sem.at[slot, 0]),
                pltpu.make_async_copy(wu_hbm.at[base + e], wu_st.at[slot], sem.at[slot, 1]),
                pltpu.make_async_copy(wd_hbm.at[base + e], wd_st.at[slot], sem.at[slot, 2]))

    @pl.when(i == 0)
    def _():
        for c in fetch(te_ref[0], slot_ref[0]):
            c.start()

    @pl.when(i < nv_ref[0])
    def _():
        e = te_ref[i]
        slot = slot_ref[i]

        @pl.when((i == 0) | (e != te_ref[jnp.maximum(i - 1, 0)]))
        def _():
            for c in fetch(e, slot):
                c.wait()
            wgu_bf[:, :EXPERT_FF] = wg_st[slot].astype(BF16)
            wgu_bf[:, EXPERT_FF:] = wu_st[slot].astype(BF16)
            wd_bf[...] = wd_st[slot].astype(BF16)

            @pl.when(next_ref[i] >= 0)
            def _():
                for c in fetch(next_ref[i], 1 - slot):
                    c.start()

        hgu = jnp.dot(xs_ref[...].astype(BF16), wgu_bf[...], preferred_element_type=F32)
        hg = hgu[:, :EXPERT_FF]
        h = hg * jax.nn.sigmoid(hg) * hgu[:, EXPERT_FF:]
        ys_ref[...] = jnp.dot(h.astype(BF16), wd_bf[...], preferred_element_type=F32)


def _moe_experts(xs, tile_expert, tile_slot, tile_next, n_valid, wg, wu, wd, layer):
    row = lambda i, te, sl, nx, nv: (jnp.minimum(i, nv[0] - 1), 0)
    return pl.pallas_call(
        functools.partial(_moe_kernel, base=layer * N_EXPERTS),
        grid_spec=pltpu.PrefetchScalarGridSpec(
            num_scalar_prefetch=4,
            grid=(MOE_TILES,),
            in_specs=[pl.BlockSpec((MOE_TM, D_MODEL), row), ANY_SPEC, ANY_SPEC, ANY_SPEC],
            out_specs=pl.BlockSpec((MOE_TM, D_MODEL), row),
            scratch_shapes=[pltpu.VMEM((2, D_MODEL, EXPERT_FF), F32), pltpu.VMEM((2, D_MODEL, EXPERT_FF), F32),
                            pltpu.VMEM((2, EXPERT_FF, D_MODEL), F32),
                            pltpu.VMEM((D_MODEL, 2 * EXPERT_FF), BF16), pltpu.VMEM((EXPERT_FF, D_MODEL), BF16),
                            pltpu.SemaphoreType.DMA((2, 3))],
        ),
        out_shape=jax.ShapeDtypeStruct((MOE_ROWS, D_MODEL), F32),
        compiler_params=pltpu.CompilerParams(dimension_semantics=("arbitrary",), vmem_limit_bytes=VMEM_LIMIT),
        name="moe_experts",
    )(tile_expert, tile_slot, tile_next, n_valid, xs, wg, wu, wd)


def _ffn_ln_kernel(pos_ref, pos_next_ref, x1_ref, info_ref, g_ref, b_ref, *rest, split, with_kv):
    kv_in, rest = (rest[:4], rest[4:]) if with_kv else ((), rest)
    ys_hbm, out_refs, (ybuf, sem) = rest[0], rest[1:-2], rest[-2:]
    i = pl.program_id(0)

    def issue(p_ref, par):
        for j in range(TM):
            for slot in range(2):
                pltpu.make_async_copy(ys_hbm.at[pl.ds(p_ref[0, slot, j], 1)],
                                      ybuf.at[par, slot, pl.ds(j, 1)], sem.at[par]).start(priority=slot)

    @pl.when(i == 0)
    def _():
        issue(pos_ref, 0)

    @pl.when(i + 1 < N_TILES)
    def _():
        issue(pos_next_ref, (i + 1) % 2)

    par = i % 2
    for slot in range(2):
        _row_copy_wait(ys_hbm.at[pl.ds(0, TM)], ybuf.at[par, slot], sem.at[par])
    info = info_ref[...]
    f = info[:, 2:3] * ybuf[par, 0] + info[:, 3:4] * ybuf[par, 1]
    x2 = _layer_norm(DEEPNORM_ALPHA * x1_ref[...] + f, g_ref[...], b_ref[...])
    if with_kv:
        wkv_ref, c_ref, sa_ref, sb_ref = kv_in
        k_ref, v_ref = out_refs[-2:]
        kv = jnp.dot(x2.astype(BF16), wkv_ref[...], preferred_element_type=F32)
        k_ref[...] = _rope_lanes(kv[:, :KV_WIDTH], c_ref[...], sa_ref[...], sb_ref[...])
        v_ref[...] = kv[:, KV_WIDTH:]
    if split:
        @pl.when(i < PROMPT_TILES)
        def _():
            out_refs[0][...] = x2

        @pl.when(i >= PROMPT_TILES)
        def _():
            out_refs[1][...] = x2
    else:
        out_refs[0][...] = x2


def _ffn_ln(x1, ys, pos, info, g, b, split, kv_w=None):
    tok = lambda i: (i, 0)
    pos_spec = lambda fn: pl.BlockSpec(POS_BLOCK, fn, memory_space=pltpu.SMEM)
    if split:
        out_specs = [pl.BlockSpec((TM, D_MODEL), lambda i: (jnp.minimum(i, PROMPT_TILES - 1), 0)),
                     pl.BlockSpec((TM, D_MODEL), lambda i: (jnp.maximum(i - PROMPT_TILES, 0), 0))]
        out_shape = [jax.ShapeDtypeStruct((T_PROMPT, D_MODEL), F32), jax.ShapeDtypeStruct((T_SAMPLE, D_MODEL), F32)]
    else:
        out_specs = [pl.BlockSpec((TM, D_MODEL), tok)]
        out_shape = [jax.ShapeDtypeStruct((T_ALL, D_MODEL), F32)]
    kv_args, kv_specs = (), []
    if kv_w is not None:
        wkv, rope_all = kv_w
        kv_args = (wkv,) + tuple(rope_all)
        kv_specs = [_full(wkv.shape)] + [pl.BlockSpec((TM, LANES), tok)] * 3
        out_specs = out_specs + [pl.BlockSpec((TM, KV_WIDTH), tok)] * 2
        out_shape = out_shape + [jax.ShapeDtypeStruct((T_ALL, KV_WIDTH), F32)] * 2
    return pl.pallas_call(
        functools.partial(_ffn_ln_kernel, split=split, with_kv=kv_w is not None),
        grid=(N_TILES,),
        in_specs=[pos_spec(lambda i: (i, 0, 0)), pos_spec(lambda i: (jnp.minimum(i + 1, N_TILES - 1), 0, 0)),
                  pl.BlockSpec((TM, D_MODEL), tok), pl.BlockSpec((TM, LANES), tok), _full(g.shape), _full(b.shape)]
                 + kv_specs + [ANY_SPEC],
        out_specs=out_specs,
        out_shape=out_shape,
        scratch_shapes=[pltpu.VMEM((2, 2, TM, D_MODEL), F32), pltpu.SemaphoreType.DMA((2,))],
        compiler_params=pltpu.CompilerParams(dimension_semantics=("arbitrary",)),
        name="ffn_ln",
    )(pos, pos, x1, info, g, b, *kv_args, ys)


def _moe_layer(x1, info, er, cnt, wg, wu, wd, layer, g, b, split, kv_w):
    counts = cnt[0, :N_EXPERTS].astype(jnp.int32)
    tiles_per = (counts + MOE_TM - 1) // MOE_TM
    tile_end = jnp.cumsum(tiles_per)
    pad_start = (tile_end - tiles_per) * MOE_TM
    ids = jnp.arange(N_EXPERTS, dtype=jnp.int32)
    half = ER_ROWS // 2
    hit = er[None, :, 0:2, :] == ids[:, None, None, None]
    pos = jnp.sum(jnp.where(hit, pad_start[:, None, None, None], 0), axis=0) + er[:, half:half + 2, :]
    n_valid = tile_end[-1:]
    tile_ids = jnp.minimum(jnp.arange(MOE_TILES, dtype=jnp.int32), n_valid[0] - 1)
    tile_expert = jnp.sum(tile_ids[:, None] >= tile_end[None, :], axis=1, dtype=jnp.int32)
    nonempty = tiles_per > 0
    slot_of = (jnp.cumsum(nonempty.astype(jnp.int32)) - 1) % 2
    later = jnp.where(nonempty[None, :] & (ids[None, :] > ids[:, None]), ids[None, :], N_EXPERTS)
    next_of = jnp.min(later, axis=1)
    next_of = jnp.where(next_of < N_EXPERTS, next_of, -1)
    zrow = jnp.maximum(tile_end - 1, 0) * MOE_TM
    mine = tile_expert[:, None] == ids[None, :]
    tile_slot = jnp.sum(jnp.where(mine, slot_of[None, :], 0), axis=1)
    tile_next = jnp.sum(jnp.where(mine, next_of[None, :], 0), axis=1)
    xs = _dispatch(x1, pos, zrow, tiles_per)
    ys = _moe_experts(xs, tile_expert, tile_slot, tile_next, n_valid, wg, wu, wd, layer)
    return _ffn_ln(x1, ys, pos, info, g, b, split, kv_w)


def _rope_tables(pos):
    half = ROT_DIM // 2
    inv_freq = ROPE_THETA ** (-jnp.arange(0, ROT_DIM, 2, dtype=F32) / ROT_DIM)
    ang = pos.astype(F32)[:, None] * inv_freq[None, :]
    cos, sin = jnp.cos(ang), jnp.sin(ang)
    n = pos.shape[0]
    ones = jnp.ones((n, HEAD_DIM - ROT_DIM), F32)
    zeros_h = jnp.zeros((n, half), F32)
    zeros_r = jnp.zeros((n, HEAD_DIM - ROT_DIM), F32)
    c = jnp.concatenate([cos, cos, ones], axis=1)
    sa = jnp.concatenate([-sin, zeros_h, zeros_r], axis=1)
    sb = jnp.concatenate([zeros_h, sin, zeros_r], axis=1)
    rep = LANES // HEAD_DIM
    return tuple(jnp.tile(t, (1, rep)) for t in (c, sa, sb))


def _dup_heads(kv):
    b, l, _ = kv.shape
    h = kv.reshape(b, l, N_KV_HEADS, HEAD_DIM).transpose(0, 2, 1, 3)
    return jnp.concatenate([h, h], axis=-1).astype(BF16)


def kernel(x_prompt, x_sample, state_conv, cache_k, cache_v, ln_mix_g, ln_mix_b, ln_ffn_g, ln_ffn_b, conv_w_in, conv_w_dw, conv_ln_g, conv_ln_b, conv_w_out, w_kv, attn_w_q, attn_sinks, attn_w_o, router_w_group, router_b_group, router_w_expert, router_b_expert, expert_w_gate, expert_w_up, expert_w_down):
    x = None
    wg = expert_w_gate.reshape(DEPTH * N_EXPERTS, D_MODEL, EXPERT_FF)
    wu = expert_w_up.reshape(DEPTH * N_EXPERTS, D_MODEL, EXPERT_FF)
    wd = expert_w_down.reshape(DEPTH * N_EXPERTS, EXPERT_FF, D_MODEL)
    row = lambda v: v.reshape(1, -1)

    def router_weights(layer):
        w = jnp.concatenate([router_w_group[layer], router_w_expert[layer]], axis=1)
        w = jnp.pad(w, ((0, 0), (0, LANES - w.shape[1])))
        w_hi = w.astype(BF16)
        w_lo = (w - w_hi.astype(F32)).astype(BF16)
        bias = jnp.concatenate([router_b_group[layer], router_b_expert[layer].reshape(-1)])
        return w_hi, w_lo, jnp.pad(bias, (0, LANES - bias.shape[0])).reshape(1, LANES)

    pos_p = jnp.arange(SEQ, dtype=jnp.int32)
    pos_s = PAST_LEN + jnp.arange(DEC_SEQ, dtype=jnp.int32)
    rope_p = _rope_tables(pos_p)
    rope_s1 = _rope_tables(pos_s)
    rope_s = tuple(jnp.tile(t, (TM // DEC_SEQ, 1)) for t in rope_s1)
    rope_all = tuple(jnp.concatenate([jnp.tile(p, (BATCH, 1)), jnp.tile(s, (DEC_BATCH, 1))], axis=0)
                     for p, s in zip(rope_p, rope_s1))

    tails, us = [], []
    kv_p = kv_s = None
    outs_kv = None
    for layer in range(DEPTH):
        post_w = (row(ln_mix_g[layer]), row(ln_mix_b[layer])) + router_weights(layer)
        if layer < N_A_LAYERS:
            st = jnp.pad(state_conv[layer], ((0, 0), (STATE_PAD - (CONV_WIDTH - 1), 0), (0, 0)))
            wdw = jnp.pad(conv_w_dw[layer], ((0, STATE_PAD - CONV_WIDTH), (0, 0)))
            wdw = wdw.reshape(STATE_PAD, N_LANE_CHUNKS, LANES).transpose(1, 0, 2)
            w = (conv_w_in[layer].astype(BF16), wdw, row(conv_ln_g[layer]), row(conv_ln_b[layer]),
                 conv_w_out[layer].astype(BF16)) + post_w
            if layer == 0:
                x_in = (x_prompt.reshape(T_PROMPT, D_MODEL), x_sample.reshape(T_SAMPLE, D_MODEL), 0)
            else:
                x_in = (x, x, PROMPT_TILES)
            x1, info, er, cnt, tail, u_s = _conv_mixer(*x_in, st, w)
            tails.append(tail[:, STATE_PAD - (CONV_WIDTH - 1):, :])
            us.append(u_s.reshape(DEC_BATCH, DEC_SEQ, D_MODEL)[:, DEC_SEQ - (CONV_WIDTH - 1):, :])
        else:
            j = layer - N_A_LAYERS
            x1, info, er, cnt = _attn_mixer(x, attn_sinks[j], attn_w_q[j].astype(BF16), attn_w_o[j].astype(BF16),
                                        post_w, rope_p, rope_s, kv_p, kv_s)
        last_a = layer == N_A_LAYERS - 1
        outs = _moe_layer(x1, info, er, cnt, wg, wu, wd, layer, row(ln_ffn_g[layer]), row(ln_ffn_b[layer]),
                          split=layer == DEPTH - 1, kv_w=(w_kv.astype(BF16), rope_all) if last_a else None)
        x = outs if layer == DEPTH - 1 else outs[0]
        if last_a:
            k, v = outs[1:]
            k_p = k[:T_PROMPT].reshape(BATCH, SEQ, KV_WIDTH)
            v_p = v[:T_PROMPT].reshape(BATCH, SEQ, KV_WIDTH)
            k_s = jnp.concatenate([cache_k.reshape(DEC_BATCH, WINDOW, KV_WIDTH),
                                   k[T_PROMPT:].reshape(DEC_BATCH, DEC_SEQ, KV_WIDTH)], axis=1)
            v_s = jnp.concatenate([cache_v.reshape(DEC_BATCH, WINDOW, KV_WIDTH),
                                   v[T_PROMPT:].reshape(DEC_BATCH, DEC_SEQ, KV_WIDTH)], axis=1)
            cache_shape = (-1, WINDOW, N_KV_HEADS, HEAD_DIM)
            outs_kv = (k_p[:, -WINDOW:].reshape(cache_shape), v_p[:, -WINDOW:].reshape(cache_shape),
                       k_s[:, -WINDOW:].reshape(cache_shape), v_s[:, -WINDOW:].reshape(cache_shape))
            front = ((0, 0), (WINDOW, 0), (0, 0))
            back = ((0, 0), (0, KEY_WIN - WINDOW - DEC_SEQ), (0, 0))
            kv_p = (_dup_heads(jnp.pad(k_p, front)), _dup_heads(jnp.pad(v_p, front)))
            kv_s = (_dup_heads(jnp.pad(k_s, back)), _dup_heads(jnp.pad(v_s, back)))
    y_prompt = x[0].reshape(BATCH, SEQ, D_MODEL)
    y_sample = x[1].reshape(DEC_BATCH, DEC_SEQ, D_MODEL)
    return (y_prompt, y_sample, jnp.stack(tails, axis=0), jnp.stack(us, axis=0)) + outs_kv
```

```python
import functools

import jax
import jax.numpy as jnp
from jax import lax
from jax.experimental import pallas as pl
from jax.experimental.pallas import tpu as pltpu

D_MODEL = 1024
BATCH = 2
SEQ = 8192
DEPTH = 4
DEC_BATCH = 32
DEC_SEQ = 32
PAST_LEN = 2048
CHUNK = 64
N_A_LAYERS = DEPTH // 2
CONV_WIDTH = 31
N_HEADS = 16
N_KV_HEADS = 2
HEAD_DIM = 64
KV_WIDTH = N_KV_HEADS * HEAD_DIM
WINDOW = 128
ROT_DIM = HEAD_DIM // 4
ROPE_THETA = 500000.0
N_GROUPS = 4
EXPERTS_PER_GROUP = 8
N_EXPERTS = N_GROUPS * EXPERTS_PER_GROUP
EXPERT_FF = D_MODEL // 2
DEEPNORM_ALPHA = (2 * DEPTH) ** 0.25
LN_EPS = 1e-5
ATTN_SCALE = HEAD_DIM ** -0.5
NEG_INF = -1e30

LANES = 128
N_LANE_CHUNKS = D_MODEL // LANES
T_PROMPT = BATCH * SEQ
T_SAMPLE = DEC_BATCH * DEC_SEQ
T_ALL = T_PROMPT + T_SAMPLE
TM = 256
STATE_PAD = 32
KEY_WIN = 256
UQ_PROMPT = 2 * CHUNK
MOE_TM = 512
MOE_ROWS = 2 * T_ALL + N_EXPERTS * MOE_TM
MOE_TILES = MOE_ROWS // MOE_TM
VMEM_LIMIT = 56 * 1024 * 1024

F32 = jnp.float32
BF16 = jnp.bfloat16


def _layer_norm(x, g, b):
    mu = jnp.mean(x, axis=-1, keepdims=True)
    xc = x - mu
    var = jnp.mean(xc * xc, axis=-1, keepdims=True)
    return xc * lax.rsqrt(var + LN_EPS) * g + b


def _router(x1, rwh_ref, rwl_ref, rb_ref, cnt_ref):
    x_hi = x1.astype(BF16)
    x_lo = (x1 - x_hi.astype(F32)).astype(BF16)
    w_hi = rwh_ref[...]
    logits = (jnp.dot(x_hi, w_hi, preferred_element_type=F32)
              + jnp.dot(x_hi, rwl_ref[...], preferred_element_type=F32)
              + jnp.dot(x_lo, w_hi, preferred_element_type=F32)) + rb_ref[...]
    lane_i = lax.broadcasted_iota(jnp.int32, logits.shape, 1)
    lane = lane_i.astype(F32)
    neg = jnp.float32(-3.0e38)
    none = jnp.float32(LANES)
    gl = jnp.where(lane_i < N_GROUPS, logits, neg)
    gmax = jnp.max(gl, axis=-1, keepdims=True)
    g_idx = jnp.min(jnp.where(gl == gmax, lane, none), axis=-1, keepdims=True)
    gsum = jnp.sum(jnp.where(lane_i < N_GROUPS, jnp.exp(gl - gmax), 0.0), axis=-1, keepdims=True)
    p_grp = 1.0 / gsum
    lo = N_GROUPS + EXPERTS_PER_GROUP * g_idx
    el = jnp.where(lane >= lo, jnp.where(lane < lo + EXPERTS_PER_GROUP, logits, neg), neg)
    l1 = jnp.max(el, axis=-1, keepdims=True)
    i1 = jnp.min(jnp.where(el == l1, lane, none), axis=-1, keepdims=True)
    el2 = jnp.where(lane == i1, neg, el)
    l2 = jnp.max(el2, axis=-1, keepdims=True)
    i2 = jnp.min(jnp.where(el2 == l2, lane, none), axis=-1, keepdims=True)
    t = jnp.exp(l2 - l1)
    inv = p_grp / (1.0 + t)
    e1 = i1 - N_GROUPS
    e2 = i2 - N_GROUPS
    rows = logits.shape[0]
    onehot = jnp.where(lane == e1, 1.0, jnp.where(lane == e2, 1.0, 0.0))
    tri = (lax.broadcasted_iota(jnp.int32, (rows, rows), 0) > lax.broadcasted_iota(jnp.int32, (rows, rows), 1))
    before = jnp.dot(jnp.where(tri, 1.0, 0.0).astype(BF16), onehot.astype(BF16),
                     preferred_element_type=F32) + cnt_ref[...]
    r1 = jnp.sum(jnp.where(lane == e1, before, 0.0), axis=-1, keepdims=True)
    r2 = jnp.sum(jnp.where(lane == e2, before, 0.0), axis=-1, keepdims=True)
    cnt_ref[...] = cnt_ref[...] + jnp.sum(onehot, axis=0, keepdims=True)
    out = jnp.where(lane_i == 4, r1, jnp.where(lane_i == 5, r2, 0.0))
    out = jnp.where(lane_i == 2, inv, jnp.where(lane_i == 3, inv * t, out))
    return jnp.where(lane_i == 0, e1, jnp.where(lane_i == 1, e2, out))


ER_ROWS = 8


def _post_mixer(x, mix, lg_ref, lb_ref, rwh_ref, rwl_ref, rb_ref, x1_ref, info_ref, er_ref, cnt_out_ref, cnt_ref):
    x1 = _layer_norm(DEEPNORM_ALPHA * x + mix, lg_ref[...], lb_ref[...])
    x1_ref[...] = x1
    info = _router(x1, rwh_ref, rwl_ref, rb_ref, cnt_ref)
    info_ref[...] = info
    er_ref[0] = jnp.transpose(info)[0:ER_ROWS, :].astype(jnp.int32)
    cnt_out_ref[...] = cnt_ref[...]


def _init_counts(cnt_ref, first_step, cnt_in_ref):
    @pl.when(first_step)
    def _():
        cnt_ref[...] = jnp.zeros_like(cnt_ref) if cnt_in_ref is None else cnt_in_ref[...]


def _conv_mixer_body(x_ref, st_ref, win_ref, wdw_ref, cg_ref, cb_ref, wout_ref, post_w, cnt_in_ref,
                     x1_ref, info_ref, er_ref, cnt_out_ref, u_ref, cnt_ref, ubuf, cbuf, *, nseg, seg):
    carry = st_ref is None
    first_step = (pl.program_id(0) == 0) & (pl.program_id(1) == 0) if carry else pl.program_id(0) == 0
    _init_counts(cnt_ref, first_step, cnt_in_ref)
    x = x_ref[...]
    h = jnp.dot(x.astype(BF16), win_ref[...], preferred_element_type=F32)
    u = h[:, :D_MODEL] * jax.nn.sigmoid(h[:, D_MODEL:])
    if carry:
        @pl.when(pl.program_id(1) == 0)
        def _():
            ubuf[:, 0, 0:STATE_PAD, :] = jnp.zeros((N_LANE_CHUNKS, STATE_PAD, LANES), F32)
        u_ref[0] = u[seg - STATE_PAD:, :]
    else:
        u_ref[...] = u
    for lc in range(N_LANE_CHUNKS):
        for s in range(nseg):
            ubuf[lc, s, STATE_PAD:STATE_PAD + seg, :] = u[s * seg:(s + 1) * seg, lc * LANES:(lc + 1) * LANES]
            if not carry:
                ubuf[lc, s, 0:STATE_PAD, :] = st_ref[s, :, lc * LANES:(lc + 1) * LANES]

    def conv_lane_chunk(lc, c):
        w = wdw_ref[lc]
        for s in range(nseg):
            acc = jnp.zeros((seg, LANES), F32)
            for k in range(CONV_WIDTH):
                off = k + STATE_PAD - (CONV_WIDTH - 1)
                acc = acc + w[k:k + 1, :] * ubuf[lc, s, off:off + seg, :]
            cbuf[lc, s * seg:(s + 1) * seg, :] = acc
        return c

    lax.fori_loop(0, N_LANE_CHUNKS, conv_lane_chunk, 0)
    if carry:
        for lc in range(N_LANE_CHUNKS):
            ubuf[lc, 0, 0:STATE_PAD, :] = u[seg - STATE_PAD:, lc * LANES:(lc + 1) * LANES]
    dw = jnp.concatenate([cbuf[lc] for lc in range(N_LANE_CHUNKS)], axis=1)
    sw = _layer_norm(dw, cg_ref[...], cb_ref[...])
    sw = sw * jax.nn.sigmoid(sw)
    mix = jnp.dot(sw.astype(BF16), wout_ref[...], preferred_element_type=F32)
    _post_mixer(x, mix, *post_w, x1_ref, info_ref, er_ref, cnt_out_ref, cnt_ref)


N_POST_W = 5


def _conv_prompt_kernel(x_ref, win_ref, wdw_ref, cg_ref, cb_ref, wout_ref, *rest):
    post_w, rest = rest[:N_POST_W], rest[N_POST_W:]
    _conv_mixer_body(x_ref, None, win_ref, wdw_ref, cg_ref, cb_ref, wout_ref, post_w, None, *rest,
                     nseg=1, seg=TM)


def _conv_sample_kernel(x_ref, st_ref, win_ref, wdw_ref, cg_ref, cb_ref, wout_ref, *rest):
    post_w, cnt_in_ref, rest = rest[:N_POST_W], rest[N_POST_W], rest[N_POST_W + 1 + N_ALIASED:]
    _conv_mixer_body(x_ref, st_ref, win_ref, wdw_ref, cg_ref, cb_ref, wout_ref, post_w, cnt_in_ref, *rest,
                     nseg=TM // DEC_SEQ, seg=DEC_SEQ)


def _full(shape):
    return pl.BlockSpec(shape, lambda *_: (0,) * len(shape))


ANY_SPEC = pl.BlockSpec(memory_space=pl.ANY)
CNT_SHAPE = jax.ShapeDtypeStruct((1, LANES), F32)
CNT_SCRATCH = pltpu.VMEM((1, LANES), F32)
PROMPT_TILES = T_PROMPT // TM
N_ALIASED = 3
TOKEN_OUT_SHAPES = [jax.ShapeDtypeStruct((T_ALL, D_MODEL), F32), jax.ShapeDtypeStruct((T_ALL, LANES), F32),
                    jax.ShapeDtypeStruct((T_ALL // TM, ER_ROWS, TM), jnp.int32)]


def _token_out_specs(tile):
    return [pl.BlockSpec((TM, D_MODEL), lambda *g: (tile(*g), 0)), pl.BlockSpec((TM, LANES), lambda *g: (tile(*g), 0)),
            pl.BlockSpec((1, ER_ROWS, TM), lambda *g: (tile(*g), 0, 0))]


def _conv_mixer(x_p, x_s, s_first, state_pad, w):
    w_specs = [_full(a.shape) for a in w]
    tiles_per_seq = SEQ // TM
    ptile = lambda b, t: b * tiles_per_seq + t
    x1, info, er, cnt, tail = pl.pallas_call(
        _conv_prompt_kernel,
        grid=(BATCH, tiles_per_seq),
        in_specs=[pl.BlockSpec((TM, D_MODEL), lambda b, t: (ptile(b, t), 0))] + w_specs,
        out_specs=_token_out_specs(ptile) + [_full(CNT_SHAPE.shape),
                                             pl.BlockSpec((1, STATE_PAD, D_MODEL), lambda b, t: (b, 0, 0))],
        out_shape=TOKEN_OUT_SHAPES + [CNT_SHAPE, jax.ShapeDtypeStruct((BATCH, STATE_PAD, D_MODEL), F32)],
        scratch_shapes=[CNT_SCRATCH, pltpu.VMEM((N_LANE_CHUNKS, 1, STATE_PAD + TM, LANES), F32),
                        pltpu.VMEM((N_LANE_CHUNKS, TM, LANES), F32)],
        compiler_params=pltpu.CompilerParams(dimension_semantics=("arbitrary", "arbitrary"),
                                             vmem_limit_bytes=VMEM_LIMIT),
        name="conv_mixer_prompt",
    )(x_p, *w)
    nb = TM // DEC_SEQ
    n_in = 2 + len(w) + 1
    x1, info, er, cnt, u_s = pl.pallas_call(
        _conv_sample_kernel,
        grid=(T_SAMPLE // TM,),
        in_specs=[pl.BlockSpec((TM, D_MODEL), lambda i: (s_first + i, 0)),
                  pl.BlockSpec((nb, STATE_PAD, D_MODEL), lambda i: (i, 0, 0))]
                 + w_specs + [_full(CNT_SHAPE.shape)] + [ANY_SPEC] * N_ALIASED,
        out_specs=_token_out_specs(lambda i: PROMPT_TILES + i)
                  + [_full(CNT_SHAPE.shape), pl.BlockSpec((TM, D_MODEL), lambda i: (i, 0))],
        out_shape=TOKEN_OUT_SHAPES + [CNT_SHAPE, jax.ShapeDtypeStruct((T_SAMPLE, D_MODEL), F32)],
        scratch_shapes=[CNT_SCRATCH, pltpu.VMEM((N_LANE_CHUNKS, nb, STATE_PAD + DEC_SEQ, LANES), F32),
                        pltpu.VMEM((N_LANE_CHUNKS, TM, LANES), F32)],
        input_output_aliases={n_in + k: k for k in range(N_ALIASED)},
        compiler_params=pltpu.CompilerParams(dimension_semantics=("arbitrary",),
                                             vmem_limit_bytes=VMEM_LIMIT),
        name="conv_mixer_sample",
    )(x_s, state_pad, *w, cnt, x1, info, er)
    return x1, info, er, cnt, tail, u_s


def _rope_lanes(v, c, sa, sb):
    width = v.shape[-1]
    half = ROT_DIM // 2
    return v * c + pltpu.roll(v, width - half, 1) * sa + pltpu.roll(v, half, 1) * sb


def _attn_mixer_body(sink_ref, x_ref, wq_ref, c_ref, sa_ref, sb_ref, kd_ref, vd_ref, wo_ref, post_w, cnt_in_ref,
                     x1_ref, info_ref, er_ref, cnt_out_ref, cnt_ref, obuf, bias_ref, *, nunit, uq, prompt):
    first_step = (pl.program_id(0) == 0) & (pl.program_id(1) == 0) if prompt else pl.program_id(0) == 0
    _init_counts(cnt_ref, first_step, cnt_in_ref)

    @pl.when(first_step)
    def _():
        shape = bias_ref.shape[1:]
        col = lax.broadcasted_iota(jnp.int32, shape, 1) & (KEY_WIN - 1)
        if prompt:
            chunk_shift = CHUNK.bit_length() - 1
            qchunk = (lax.broadcasted_iota(jnp.int32, shape, 0) & (uq - 1)) >> chunk_shift
            kchunk = col >> chunk_shift
            band = jnp.where(kchunk >= qchunk, jnp.where(kchunk <= qchunk + WINDOW // CHUNK, 0.0, NEG_INF), NEG_INF)
            bias_ref[0] = band
            bias_ref[1] = jnp.where(col >= WINDOW, band, NEG_INF)
        else:
            bias_ref[0] = jnp.where(col < WINDOW + DEC_SEQ, 0.0, NEG_INF)

    x = x_ref[...]
    q = jnp.dot(x.astype(BF16), wq_ref[...], preferred_element_type=F32)
    rep = D_MODEL // LANES
    q = _rope_lanes(q, jnp.tile(c_ref[...], (1, rep)), jnp.tile(sa_ref[...], (1, rep)),
                    jnp.tile(sb_ref[...], (1, rep)))
    qb = (q * ATTN_SCALE).astype(BF16)
    pairs = N_HEADS // N_KV_HEADS // 2
    lane_k = lax.broadcasted_iota(jnp.int32, (KEY_WIN, LANES), 1)
    lane_o = lax.broadcasted_iota(jnp.int32, (uq, LANES), 1)
    for un in range(nunit):
        r0 = un * uq
        if prompt:
            kstart = pl.multiple_of(pl.program_id(1) * (nunit * uq) + r0, LANES)
            bias = bias_ref[jnp.where(kstart == 0, 1, 0)]
        else:
            bias = bias_ref[0]
        for kvh in range(N_KV_HEADS):
            if prompt:
                kw = kd_ref[0, kvh, pl.ds(kstart, KEY_WIN), :]
                vw = vd_ref[0, kvh, pl.ds(kstart, KEY_WIN), :]
            else:
                kw = kd_ref[un, kvh]
                vw = vd_ref[un, kvh]
            zero = jnp.zeros_like(kw)
            kbd = jnp.concatenate([jnp.where(lane_k < HEAD_DIM, kw, zero),
                                   jnp.where(lane_k >= HEAD_DIM, kw, zero)], axis=0)
            vbd = jnp.concatenate([jnp.where(lane_k < HEAD_DIM, vw, zero),
                                   jnp.where(lane_k >= HEAD_DIM, vw, zero)], axis=0)
            q4 = jnp.concatenate([qb[r0:r0 + uq, (kvh * pairs + p) * LANES:(kvh * pairs + p + 1) * LANES]
                                  for p in range(pairs)], axis=0)
            s = lax.dot_general(q4, kbd, (((1,), (1,)), ((), ())), preferred_element_type=F32)
            s = s + bias
            p_parts, inv_parts = [], []
            for p in range(pairs):
                halves, invs = [], []
                for hf in range(2):
                    sk = sink_ref[(kvh * pairs + p) * 2 + hf]
                    sh = s[p * uq:(p + 1) * uq, hf * KEY_WIN:(hf + 1) * KEY_WIN]
                    m = jnp.maximum(jnp.max(sh, axis=-1, keepdims=True), sk)
                    pe = jnp.exp(sh - m)
                    den = jnp.sum(pe, axis=-1, keepdims=True) + jnp.exp(sk - m)
                    halves.append(pe.astype(BF16))
                    invs.append(1.0 / den)
                p_parts.append(jnp.concatenate(halves, axis=1))
                inv_parts.append(invs)
            pm = jnp.concatenate(p_parts, axis=0)
            o4 = jnp.dot(pm, vbd, preferred_element_type=F32)
            for p in range(pairs):
                inv = jnp.where(lane_o < HEAD_DIM, inv_parts[p][0], inv_parts[p][1])
                hp = kvh * pairs + p
                obuf[r0:r0 + uq, hp * LANES:(hp + 1) * LANES] = (o4[p * uq:(p + 1) * uq, :] * inv).astype(BF16)
    mix = jnp.dot(obuf[...], wo_ref[...], preferred_element_type=F32)
    _post_mixer(x, mix, *post_w, x1_ref, info_ref, er_ref, cnt_out_ref, cnt_ref)


N_ATTN_IN = 9


def _attn_prompt_kernel(*refs):
    ins, post_w, rest = refs[:N_ATTN_IN], refs[N_ATTN_IN:N_ATTN_IN + N_POST_W], refs[N_ATTN_IN + N_POST_W:]
    _attn_mixer_body(*ins, post_w, None, *rest, nunit=TM // UQ_PROMPT, uq=UQ_PROMPT, prompt=True)


def _attn_sample_kernel(*refs):
    ins, post_w = refs[:N_ATTN_IN], refs[N_ATTN_IN:N_ATTN_IN + N_POST_W]
    cnt_in_ref, rest = refs[N_ATTN_IN + N_POST_W], refs[N_ATTN_IN + N_POST_W + 1 + N_ALIASED:]
    _attn_mixer_body(*ins, post_w, cnt_in_ref, *rest, nunit=TM // DEC_SEQ, uq=DEC_SEQ, prompt=False)


def _attn_mixer(x, sinks, wq, wo, post_w, rope_p, rope_s, kv_p, kv_s):
    tiles_per_seq = SEQ // TM
    smem = pl.BlockSpec(memory_space=pltpu.SMEM)
    tail_specs = [_full(wo.shape)] + [_full(a.shape) for a in post_w]
    kd, vd = kv_p
    ptile = lambda b, t: b * tiles_per_seq + t
    pair_rows = N_HEADS // N_KV_HEADS // 2
    x1, info, er, cnt = pl.pallas_call(
        _attn_prompt_kernel,
        grid=(BATCH, tiles_per_seq),
        in_specs=[smem, pl.BlockSpec((TM, D_MODEL), lambda b, t: (ptile(b, t), 0)), _full(wq.shape)]
                 + [pl.BlockSpec((TM, LANES), lambda b, t: (t, 0))] * 3
                 + [pl.BlockSpec((1,) + kd.shape[1:], lambda b, t: (b, 0, 0, 0))] * 2 + tail_specs,
        out_specs=_token_out_specs(ptile) + [_full(CNT_SHAPE.shape)],
        out_shape=TOKEN_OUT_SHAPES + [CNT_SHAPE],
        scratch_shapes=[CNT_SCRATCH, pltpu.VMEM((TM, D_MODEL), BF16),
                        pltpu.VMEM((2, pair_rows * UQ_PROMPT, 2 * KEY_WIN), F32)],
        compiler_params=pltpu.CompilerParams(dimension_semantics=("arbitrary", "arbitrary"),
                                             vmem_limit_bytes=VMEM_LIMIT),
        name="attn_mixer_prompt",
    )(sinks, x, wq, *rope_p, kd, vd, wo, *post_w)
    nb = TM // DEC_SEQ
    kd, vd = kv_s
    n_in = N_ATTN_IN + N_POST_W + 1
    x1, info, er, cnt = pl.pallas_call(
        _attn_sample_kernel,
        grid=(T_SAMPLE // TM,),
        in_specs=[smem, pl.BlockSpec((TM, D_MODEL), lambda i: (PROMPT_TILES + i, 0)), _full(wq.shape)]
                 + [_full((TM, LANES))] * 3
                 + [pl.BlockSpec((nb,) + kd.shape[1:], lambda i: (i, 0, 0, 0))] * 2 + tail_specs
                 + [_full(CNT_SHAPE.shape)] + [ANY_SPEC] * N_ALIASED,
        out_specs=_token_out_specs(lambda i: PROMPT_TILES + i) + [_full(CNT_SHAPE.shape)],
        out_shape=TOKEN_OUT_SHAPES + [CNT_SHAPE],
        scratch_shapes=[CNT_SCRATCH, pltpu.VMEM((TM, D_MODEL), BF16),
                        pltpu.VMEM((1, pair_rows * DEC_SEQ, 2 * KEY_WIN), F32)],
        input_output_aliases={n_in + k: k for k in range(N_ALIASED)},
        compiler_params=pltpu.CompilerParams(dimension_semantics=("arbitrary",),
                                             vmem_limit_bytes=VMEM_LIMIT),
        name="attn_mixer_sample",
    )(sinks, x, wq, *rope_s, kd, vd, wo, *post_w, cnt, x1, info, er)
    return x1, info, er, cnt


N_TILES = T_ALL // TM


def _row_copy_wait(src_rows, dst_rows, sem):
    pltpu.make_async_copy(src_rows, dst_rows, sem).wait()


PAIR = 2
N_PAIRS = N_TILES // PAIR
POS_BLOCK = (PAIR, 2, TM)


def _dispatch_kernel(zrow_ref, zflag_ref, pos_ref, x1_ref, xs_hbm, zbuf, sem, zsem):
    i = pl.program_id(0)

    @pl.when(i == 0)
    def _():
        zbuf[...] = jnp.zeros_like(zbuf)

        def zero_tile(e):
            return pltpu.make_async_copy(zbuf, xs_hbm.at[pl.ds(pl.multiple_of(zrow_ref[e], MOE_TM), MOE_TM)], zsem)

        for e in range(N_EXPERTS):
            @pl.when(zflag_ref[e] > 0)
            def _():
                zero_tile(e).start()
        for e in range(N_EXPERTS):
            @pl.when(zflag_ref[e] > 0)
            def _():
                zero_tile(e).wait()

    for h in range(PAIR):
        for j in range(TM):
            for slot in range(2):
                pltpu.make_async_copy(x1_ref.at[pl.ds(h * TM + j, 1)],
                                      xs_hbm.at[pl.ds(pos_ref[h, slot, j], 1)], sem).start(priority=slot)
    for slot in range(2):
        _row_copy_wait(x1_ref, xs_hbm.at[pl.ds(0, PAIR * TM)], sem)


def _dispatch(x1, pos, zrow, zflag):
    return pl.pallas_call(
        _dispatch_kernel,
        grid_spec=pltpu.PrefetchScalarGridSpec(
            num_scalar_prefetch=2,
            grid=(N_PAIRS,),
            in_specs=[pl.BlockSpec(POS_BLOCK, lambda i, *_: (i, 0, 0), memory_space=pltpu.SMEM),
                      pl.BlockSpec((PAIR * TM, D_MODEL), lambda i, *_: (i, 0))],
            out_specs=ANY_SPEC,
            scratch_shapes=[pltpu.VMEM((MOE_TM, D_MODEL), F32), pltpu.SemaphoreType.DMA,
                            pltpu.SemaphoreType.DMA],
        ),
        out_shape=jax.ShapeDtypeStruct((MOE_ROWS, D_MODEL), F32),
        compiler_params=pltpu.CompilerParams(dimension_semantics=("arbitrary",)),
        name="moe_dispatch",
    )(zrow, zflag, pos, x1)


def _moe_kernel(te_ref, slot_ref, next_ref, nv_ref, xs_ref, wg_hbm, wu_hbm, wd_hbm, ys_ref,
                wg_st, wu_st, wd_st, wgu_bf, wd_bf, sem, *, base):
    i = pl.program_id(0)

    def fetch(e, slot):
        return (pltpu.make_async_copy(wg_hbm.at[base + e], wg_st.at[slot], sem.at[slot, 0]),
                pltpu.make_async_copy(wu_hbm.at[base + e], wu_st.at[slot], sem.at[slot, 1]),
                pltpu.make_async_copy(wd_hbm.at[base + e], wd_st.at[slot], sem.at[slot, 2]))

    @pl.when(i == 0)
    def _():
        for c in fetch(te_ref[0], slot_ref[0]):
            c.start()

    @pl.when(i < nv_ref[0])
    def _():
        e = te_ref[i]
        slot = slot_ref[i]

        @pl.when((i == 0) | (e != te_ref[jnp.maximum(i - 1, 0)]))
        def _():
            for c in fetch(e, slot):
                c.wait()
            wgu_bf[:, :EXPERT_FF] = wg_st[slot].astype(BF16)
            wgu_bf[:, EXPERT_FF:] = wu_st[slot].astype(BF16)
            wd_bf[...] = wd_st[slot].astype(BF16)

            @pl.when(next_ref[i] >= 0)
            def _():
                for c in fetch(next_ref[i], 1 - slot):
                    c.start()

        hgu = jnp.dot(xs_ref[...].astype(BF16), wgu_bf[...], preferred_element_type=F32)
        hg = hgu[:, :EXPERT_FF]
        h = hg * jax.nn.sigmoid(hg) * hgu[:, EXPERT_FF:]
        ys_ref[...] = jnp.dot(h.astype(BF16), wd_bf[...], preferred_element_type=F32)


def _moe_experts(xs, tile_expert, tile_slot, tile_next, n_valid, wg, wu, wd, layer):
    row = lambda i, te, sl, nx, nv: (jnp.minimum(i, nv[0] - 1), 0)
    return pl.pallas_call(
        functools.partial(_moe_kernel, base=layer * N_EXPERTS),
        grid_spec=pltpu.PrefetchScalarGridSpec(
            num_scalar_prefetch=4,
            grid=(MOE_TILES,),
            in_specs=[pl.BlockSpec((MOE_TM, D_MODEL), row), ANY_SPEC, ANY_SPEC, ANY_SPEC],
            out_specs=pl.BlockSpec((MOE_TM, D_MODEL), row),
            scratch_shapes=[pltpu.VMEM((2, D_MODEL, EXPERT_FF), F32), pltpu.VMEM((2, D_MODEL, EXPERT_FF), F32),
                            pltpu.VMEM((2, EXPERT_FF, D_MODEL), F32),
                            pltpu.VMEM((D_MODEL, 2 * EXPERT_FF), BF16), pltpu.VMEM((EXPERT_FF, D_MODEL), BF16),
                            pltpu.SemaphoreType.DMA((2, 3))],
        ),
        out_shape=jax.ShapeDtypeStruct((MOE_ROWS, D_MODEL), F32),
        compiler_params=pltpu.CompilerParams(dimension_semantics=("arbitrary",), vmem_limit_bytes=VMEM_LIMIT),
        name="moe_experts",
    )(tile_expert, tile_slot, tile_next, n_valid, xs, wg, wu, wd)


def _ffn_ln_kernel(pos_ref, pos_next_ref, x1_ref, info_ref, g_ref, b_ref, *rest, split, with_kv):
    kv_in, rest = (rest[:4], rest[4:]) if with_kv else ((), rest)
    ys_hbm, out_refs, bufs, sem = rest[0], rest[1:-(PAIR + 1)], rest[-(PAIR + 1):-1], rest[-1]
    i = pl.program_id(0)

    def gather(p_ref, h, buf):
        for j in range(TM):
            for slot in range(2):
                pltpu.make_async_copy(ys_hbm.at[pl.ds(p_ref[h, slot, j], 1)],
                                      bufs[buf].at[slot, pl.ds(j, 1)], sem.at[buf]).start(priority=slot)

    def drain(buf):
        for slot in range(2):
            _row_copy_wait(ys_hbm.at[pl.ds(0, TM)], bufs[buf].at[slot], sem.at[buf])

    @pl.when(i == 0)
    def _():
        gather(pos_ref, 0, 0)

    for h in range(PAIR):
        if h + 1 < PAIR:
            gather(pos_ref, h + 1, h + 1)
        else:
            gather(pos_next_ref, 0, 0)
        drain(h)
        rows = slice(h * TM, (h + 1) * TM)
        info = info_ref[rows, :]
        f = info[:, 2:3] * bufs[h][0] + info[:, 3:4] * bufs[h][1]
        x2 = _layer_norm(DEEPNORM_ALPHA * x1_ref[rows, :] + f, g_ref[...], b_ref[...])
        if with_kv:
            wkv_ref, c_ref, sa_ref, sb_ref = kv_in
            k_ref, v_ref = out_refs[-2:]
            kv = jnp.dot(x2.astype(BF16), wkv_ref[...], preferred_element_type=F32)
            k_ref[rows, :] = _rope_lanes(kv[:, :KV_WIDTH], c_ref[rows, :], sa_ref[rows, :], sb_ref[rows, :])
            v_ref[rows, :] = kv[:, KV_WIDTH:]
        if split:
            @pl.when(i < PROMPT_TILES // PAIR)
            def _():
                out_refs[0][rows, :] = x2

            @pl.when(i >= PROMPT_TILES // PAIR)
            def _():
                out_refs[1][rows, :] = x2
        else:
            out_refs[0][rows, :] = x2

    @pl.when(i == N_PAIRS - 1)
    def _():
        drain(0)


def _ffn_ln(x1, ys, pos, info, g, b, split, kv_w=None):
    tok = lambda i: (i, 0)
    rows = PAIR * TM
    prompt_steps = PROMPT_TILES // PAIR
    pos_spec = lambda fn: pl.BlockSpec(POS_BLOCK, fn, memory_space=pltpu.SMEM)
    if split:
        out_specs = [pl.BlockSpec((rows, D_MODEL), lambda i: (jnp.minimum(i, prompt_steps - 1), 0)),
                     pl.BlockSpec((rows, D_MODEL), lambda i: (jnp.maximum(i - prompt_steps, 0), 0))]
        out_shape = [jax.ShapeDtypeStruct((T_PROMPT, D_MODEL), F32), jax.ShapeDtypeStruct((T_SAMPLE, D_MODEL), F32)]
    else:
        out_specs = [pl.BlockSpec((rows, D_MODEL), tok)]
        out_shape = [jax.ShapeDtypeStruct((T_ALL, D_MODEL), F32)]
    kv_args, kv_specs = (), []
    if kv_w is not None:
        wkv, rope_all = kv_w
        kv_args = (wkv,) + tuple(rope_all)
        kv_specs = [_full(wkv.shape)] + [pl.BlockSpec((rows, LANES), tok)] * 3
        out_specs = out_specs + [pl.BlockSpec((rows, KV_WIDTH), tok)] * 2
        out_shape = out_shape + [jax.ShapeDtypeStruct((T_ALL, KV_WIDTH), F32)] * 2
    return pl.pallas_call(
        functools.partial(_ffn_ln_kernel, split=split, with_kv=kv_w is not None),
        grid=(N_PAIRS,),
        in_specs=[pos_spec(lambda i: (i, 0, 0)), pos_spec(lambda i: (jnp.minimum(i + 1, N_PAIRS - 1), 0, 0)),
                  pl.BlockSpec((rows, D_MODEL), tok), pl.BlockSpec((rows, LANES), tok), _full(g.shape),
                  _full(b.shape)] + kv_specs + [ANY_SPEC],
        out_specs=out_specs,
        out_shape=out_shape,
        scratch_shapes=[pltpu.VMEM((2, TM, D_MODEL), F32)] * PAIR + [pltpu.SemaphoreType.DMA((PAIR,))],
        compiler_params=pltpu.CompilerParams(dimension_semantics=("arbitrary",)),
        name="ffn_ln",
    )(pos, pos, x1, info, g, b, *kv_args, ys)


def _moe_layer(x1, info, er, cnt, wg, wu, wd, layer, g, b, split, kv_w):
    counts = cnt[0, :N_EXPERTS].astype(jnp.int32)
    tiles_per = (counts + MOE_TM - 1) // MOE_TM
    tile_end = jnp.cumsum(tiles_per)
    pad_start = (tile_end - tiles_per) * MOE_TM
    ids = jnp.arange(N_EXPERTS, dtype=jnp.int32)
    half = ER_ROWS // 2
    hit = er[None, :, 0:2, :] == ids[:, None, None, None]
    pos = jnp.sum(jnp.where(hit, pad_start[:, None, None, None], 0), axis=0) + er[:, half:half + 2, :]
    n_valid = tile_end[-1:]
    tile_ids = jnp.minimum(jnp.arange(MOE_TILES, dtype=jnp.int32), n_valid[0] - 1)
    tile_expert = jnp.sum(tile_ids[:, None] >= tile_end[None, :], axis=1, dtype=jnp.int32)
    nonempty = tiles_per > 0
    slot_of = (jnp.cumsum(nonempty.astype(jnp.int32)) - 1) % 2
    later = jnp.where(nonempty[None, :] & (ids[None, :] > ids[:, None]), ids[None, :], N_EXPERTS)
    next_of = jnp.min(later, axis=1)
    next_of = jnp.where(next_of < N_EXPERTS, next_of, -1)
    zrow = jnp.maximum(tile_end - 1, 0) * MOE_TM
    mine = tile_expert[:, None] == ids[None, :]
    tile_slot = jnp.sum(jnp.where(mine, slot_of[None, :], 0), axis=1)
    tile_next = jnp.sum(jnp.where(mine, next_of[None, :], 0), axis=1)
    xs = _dispatch(x1, pos, zrow, tiles_per)
    ys = _moe_experts(xs, tile_expert, tile_slot, tile_next, n_valid, wg, wu, wd, layer)
    return _ffn_ln(x1, ys, pos, info, g, b, split, kv_w)


def _rope_tables(pos):
    half = ROT_DIM // 2
    inv_freq = ROPE_THETA ** (-jnp.arange(0, ROT_DIM, 2, dtype=F32) / ROT_DIM)
    ang = pos.astype(F32)[:, None] * inv_freq[None, :]
    cos, sin = jnp.cos(ang), jnp.sin(ang)
    n = pos.shape[0]
    ones = jnp.ones((n, HEAD_DIM - ROT_DIM), F32)
    zeros_h = jnp.zeros((n, half), F32)
    zeros_r = jnp.zeros((n, HEAD_DIM - ROT_DIM), F32)
    c = jnp.concatenate([cos, cos, ones], axis=1)
    sa = jnp.concatenate([-sin, zeros_h, zeros_r], axis=1)
    sb = jnp.concatenate([zeros_h, sin, zeros_r], axis=1)
    rep = LANES // HEAD_DIM
    return tuple(jnp.tile(t, (1, rep)) for t in (c, sa, sb))


def _dup_heads(kv):
    b, l, _ = kv.shape
    h = kv.reshape(b, l, N_KV_HEADS, HEAD_DIM).transpose(0, 2, 1, 3)
    return jnp.concatenate([h, h], axis=-1).astype(BF16)


def kernel(x_prompt, x_sample, state_conv, cache_k, cache_v, ln_mix_g, ln_mix_b, ln_ffn_g, ln_ffn_b, conv_w_in, conv_w_dw, conv_ln_g, conv_ln_b, conv_w_out, w_kv, attn_w_q, attn_sinks, attn_w_o, router_w_group, router_b_group, router_w_expert, router_b_expert, expert_w_gate, expert_w_up, expert_w_down):
    x = None
    wg = expert_w_gate.reshape(DEPTH * N_EXPERTS, D_MODEL, EXPERT_FF)
    wu = expert_w_up.reshape(DEPTH * N_EXPERTS, D_MODEL, EXPERT_FF)
    wd = expert_w_down.reshape(DEPTH * N_EXPERTS, EXPERT_FF, D_MODEL)
    row = lambda v: v.reshape(1, -1)

    def router_weights(layer):
        w = jnp.concatenate([router_w_group[layer], router_w_expert[layer]], axis=1)
        w = jnp.pad(w, ((0, 0), (0, LANES - w.shape[1])))
        w_hi = w.astype(BF16)
        w_lo = (w - w_hi.astype(F32)).astype(BF16)
        bias = jnp.concatenate([router_b_group[layer], router_b_expert[layer].reshape(-1)])
        return w_hi, w_lo, jnp.pad(bias, (0, LANES - bias.shape[0])).reshape(1, LANES)

    pos_p = jnp.arange(SEQ, dtype=jnp.int32)
    pos_s = PAST_LEN + jnp.arange(DEC_SEQ, dtype=jnp.int32)
    rope_p = _rope_tables(pos_p)
    rope_s1 = _rope_tables(pos_s)
    rope_s = tuple(jnp.tile(t, (TM // DEC_SEQ, 1)) for t in rope_s1)
    rope_all = tuple(jnp.concatenate([jnp.tile(p, (BATCH, 1)), jnp.tile(s, (DEC_BATCH, 1))], axis=0)
                     for p, s in zip(rope_p, rope_s1))

    tails, us = [], []
    kv_p = kv_s = None
    outs_kv = None
    for layer in range(DEPTH):
        post_w = (row(ln_mix_g[layer]), row(ln_mix_b[layer])) + router_weights(layer)
        if layer < N_A_LAYERS:
            st = jnp.pad(state_conv[layer], ((0, 0), (STATE_PAD - (CONV_WIDTH - 1), 0), (0, 0)))
            wdw = jnp.pad(conv_w_dw[layer], ((0, STATE_PAD - CONV_WIDTH), (0, 0)))
            wdw = wdw.reshape(STATE_PAD, N_LANE_CHUNKS, LANES).transpose(1, 0, 2)
            w = (conv_w_in[layer].astype(BF16), wdw, row(conv_ln_g[layer]), row(conv_ln_b[layer]),
                 conv_w_out[layer].astype(BF16)) + post_w
            if layer == 0:
                x_in = (x_prompt.reshape(T_PROMPT, D_MODEL), x_sample.reshape(T_SAMPLE, D_MODEL), 0)
            else:
                x_in = (x, x, PROMPT_TILES)
            x1, info, er, cnt, tail, u_s = _conv_mixer(*x_in, st, w)
            tails.append(tail[:, STATE_PAD - (CONV_WIDTH - 1):, :])
            us.append(u_s.reshape(DEC_BATCH, DEC_SEQ, D_MODEL)[:, DEC_SEQ - (CONV_WIDTH - 1):, :])
        else:
            j = layer - N_A_LAYERS
            x1, info, er, cnt = _attn_mixer(x, attn_sinks[j], attn_w_q[j].astype(BF16), attn_w_o[j].astype(BF16),
                                        post_w, rope_p, rope_s, kv_p, kv_s)
        last_a = layer == N_A_LAYERS - 1
        outs = _moe_layer(x1, info, er, cnt, wg, wu, wd, layer, row(ln_ffn_g[layer]), row(ln_ffn_b[layer]),
                          split=layer == DEPTH - 1, kv_w=(w_kv.astype(BF16), rope_all) if last_a else None)
        x = outs if layer == DEPTH - 1 else outs[0]
        if last_a:
            k, v = outs[1:]
            k_p = k[:T_PROMPT].reshape(BATCH, SEQ, KV_WIDTH)
            v_p = v[:T_PROMPT].reshape(BATCH, SEQ, KV_WIDTH)
            k_s = jnp.concatenate([cache_k.reshape(DEC_BATCH, WINDOW, KV_WIDTH),
                                   k[T_PROMPT:].reshape(DEC_BATCH, DEC_SEQ, KV_WIDTH)], axis=1)
            v_s = jnp.concatenate([cache_v.reshape(DEC_BATCH, WINDOW, KV_WIDTH),
                                   v[T_PROMPT:].reshape(DEC_BATCH, DEC_SEQ, KV_WIDTH)], axis=1)
            cache_shape = (-1, WINDOW, N_KV_HEADS, HEAD_DIM)
            outs_kv = (k_p[:, -WINDOW:].reshape(cache_shape), v_p[:, -WINDOW:].reshape(cache_shape),
                       k_s[:, -WINDOW:].reshape(cache_shape), v_s[:, -WINDOW:].reshape(cache_shape))
            front = ((0, 0), (WINDOW, 0), (0, 0))
            back = ((0, 0), (0, KEY_WIN - WINDOW - DEC_SEQ), (0, 0))
            kv_p = (_dup_heads(jnp.pad(k_p, front)), _dup_heads(jnp.pad(v_p, front)))
            kv_s = (_dup_heads(jnp.pad(k_s, back)), _dup_heads(jnp.pad(v_s, back)))
    y_prompt = x[0].reshape(BATCH, SEQ, D_MODEL)
    y_sample = x[1].reshape(DEC_BATCH, DEC_SEQ, D_MODEL)
    return (y_prompt, y_sample, jnp.stack(tails, axis=0), jnp.stack(us, axis=0)) + outs_kv
```

```python
import functools

import jax
import jax.numpy as jnp
from jax import lax
from jax.experimental import pallas as pl
from jax.experimental.pallas import tpu as pltpu

D_MODEL = 1024
BATCH = 2
SEQ = 8192
DEPTH = 4
DEC_BATCH = 32
DEC_SEQ = 32
PAST_LEN = 2048
CHUNK = 64
N_A_LAYERS = DEPTH // 2
CONV_WIDTH = 31
N_HEADS = 16
N_KV_HEADS = 2
HEAD_DIM = 64
KV_WIDTH = N_KV_HEADS * HEAD_DIM
WINDOW = 128
ROT_DIM = HEAD_DIM // 4
ROPE_THETA = 500000.0
N_GROUPS = 4
EXPERTS_PER_GROUP = 8
N_EXPERTS = N_GROUPS * EXPERTS_PER_GROUP
EXPERT_FF = D_MODEL // 2
DEEPNORM_ALPHA = (2 * DEPTH) ** 0.25
LN_EPS = 1e-5
ATTN_SCALE = HEAD_DIM ** -0.5
NEG_INF = -1e30

LANES = 128
N_LANE_CHUNKS = D_MODEL // LANES
T_PROMPT = BATCH * SEQ
T_SAMPLE = DEC_BATCH * DEC_SEQ
T_ALL = T_PROMPT + T_SAMPLE
TM = 256
STATE_PAD = 32
KEY_WIN = 256
UQ_PROMPT = 2 * CHUNK
MOE_TM = 512
MOE_ROWS = 2 * T_ALL + N_EXPERTS * MOE_TM
MOE_TILES = MOE_ROWS // MOE_TM
VMEM_LIMIT = 56 * 1024 * 1024

F32 = jnp.float32
BF16 = jnp.bfloat16


def _layer_norm(x, g, b):
    mu = jnp.mean(x, axis=-1, keepdims=True)
    xc = x - mu
    var = jnp.mean(xc * xc, axis=-1, keepdims=True)
    return xc * lax.rsqrt(var + LN_EPS) * g + b


def _router(x1, rwh_ref, rwl_ref, rb_ref, cnt_ref):
    x_hi = x1.astype(BF16)
    x_lo = (x1 - x_hi.astype(F32)).astype(BF16)
    w_hi = rwh_ref[...]
    logits = (jnp.dot(x_hi, w_hi, preferred_element_type=F32)
              + jnp.dot(x_hi, rwl_ref[...], preferred_element_type=F32)
              + jnp.dot(x_lo, w_hi, preferred_element_type=F32)) + rb_ref[...]
    lane_i = lax.broadcasted_iota(jnp.int32, logits.shape, 1)
    lane = lane_i.astype(F32)
    neg = jnp.float32(-3.0e38)
    none = jnp.float32(LANES)
    gl = jnp.where(lane_i < N_GROUPS, logits, neg)
    gmax = jnp.max(gl, axis=-1, keepdims=True)
    g_idx = jnp.min(jnp.where(gl == gmax, lane, none), axis=-1, keepdims=True)
    gsum = jnp.sum(jnp.where(lane_i < N_GROUPS, jnp.exp(gl - gmax), 0.0), axis=-1, keepdims=True)
    p_grp = 1.0 / gsum
    lo = N_GROUPS + EXPERTS_PER_GROUP * g_idx
    el = jnp.where(lane >= lo, jnp.where(lane < lo + EXPERTS_PER_GROUP, logits, neg), neg)
    l1 = jnp.max(el, axis=-1, keepdims=True)
    i1 = jnp.min(jnp.where(el == l1, lane, none), axis=-1, keepdims=True)
    el2 = jnp.where(lane == i1, neg, el)
    l2 = jnp.max(el2, axis=-1, keepdims=True)
    i2 = jnp.min(jnp.where(el2 == l2, lane, none), axis=-1, keepdims=True)
    t = jnp.exp(l2 - l1)
    inv = p_grp / (1.0 + t)
    e1 = i1 - N_GROUPS
    e2 = i2 - N_GROUPS
    rows = logits.shape[0]
    onehot = jnp.where(lane == e1, 1.0, jnp.where(lane == e2, 1.0, 0.0))
    tri = (lax.broadcasted_iota(jnp.int32, (rows, rows), 0) > lax.broadcasted_iota(jnp.int32, (rows, rows), 1))
    before = jnp.dot(jnp.where(tri, 1.0, 0.0).astype(BF16), onehot.astype(BF16),
                     preferred_element_type=F32) + cnt_ref[...]
    r1 = jnp.sum(jnp.where(lane == e1, before, 0.0), axis=-1, keepdims=True)
    r2 = jnp.sum(jnp.where(lane == e2, before, 0.0), axis=-1, keepdims=True)
    cnt_ref[...] = cnt_ref[...] + jnp.sum(onehot, axis=0, keepdims=True)
    out = jnp.where(lane_i == 4, r1, jnp.where(lane_i == 5, r2, 0.0))
    out = jnp.where(lane_i == 2, inv, jnp.where(lane_i == 3, inv * t, out))
    return jnp.where(lane_i == 0, e1, jnp.where(lane_i == 1, e2, out))


ER_ROWS = 8


def _post_mixer(x, mix, lg_ref, lb_ref, rwh_ref, rwl_ref, rb_ref, x1_ref, info_ref, er_ref, cnt_out_ref, cnt_ref):
    x1 = _layer_norm(DEEPNORM_ALPHA * x + mix, lg_ref[...], lb_ref[...])
    x1_ref[...] = x1
    info = _router(x1, rwh_ref, rwl_ref, rb_ref, cnt_ref)
    info_ref[...] = info
    er_ref[0] = jnp.transpose(info)[0:ER_ROWS, :].astype(jnp.int32)
    cnt_out_ref[...] = cnt_ref[...]


def _init_counts(cnt_ref, first_step, cnt_in_ref):
    @pl.when(first_step)
    def _():
        cnt_ref[...] = jnp.zeros_like(cnt_ref) if cnt_in_ref is None else cnt_in_ref[...]


def _conv_mixer_body(x_ref, st_ref, win_ref, wdw_ref, cg_ref, cb_ref, wout_ref, post_w, cnt_in_ref,
                     x1_ref, info_ref, er_ref, cnt_out_ref, u_ref, cnt_ref, ubuf, cbuf, *, nseg, seg):
    carry = st_ref is None
    first_step = (pl.program_id(0) == 0) & (pl.program_id(1) == 0) if carry else pl.program_id(0) == 0
    _init_counts(cnt_ref, first_step, cnt_in_ref)
    x = x_ref[...]
    h = jnp.dot(x.astype(BF16), win_ref[...], preferred_element_type=F32)
    u = h[:, :D_MODEL] * jax.nn.sigmoid(h[:, D_MODEL:])
    if carry:
        @pl.when(pl.program_id(1) == 0)
        def _():
            ubuf[:, 0, 0:STATE_PAD, :] = jnp.zeros((N_LANE_CHUNKS, STATE_PAD, LANES), F32)
        u_ref[0] = u[seg - STATE_PAD:, :]
    else:
        u_ref[...] = u
    for lc in range(N_LANE_CHUNKS):
        for s in range(nseg):
            ubuf[lc, s, STATE_PAD:STATE_PAD + seg, :] = u[s * seg:(s + 1) * seg, lc * LANES:(lc + 1) * LANES]
            if not carry:
                ubuf[lc, s, 0:STATE_PAD, :] = st_ref[s, :, lc * LANES:(lc + 1) * LANES]

    def conv_lane_chunk(lc, c):
        w = wdw_ref[lc]
        for s in range(nseg):
            acc = jnp.zeros((seg, LANES), F32)
            for k in range(CONV_WIDTH):
                off = k + STATE_PAD - (CONV_WIDTH - 1)
                acc = acc + w[k:k + 1, :] * ubuf[lc, s, off:off + seg, :]
            cbuf[lc, s * seg:(s + 1) * seg, :] = acc
        return c

    lax.fori_loop(0, N_LANE_CHUNKS, conv_lane_chunk, 0)
    if carry:
        for lc in range(N_LANE_CHUNKS):
            ubuf[lc, 0, 0:STATE_PAD, :] = u[seg - STATE_PAD:, lc * LANES:(lc + 1) * LANES]
    dw = jnp.concatenate([cbuf[lc] for lc in range(N_LANE_CHUNKS)], axis=1)
    sw = _layer_norm(dw, cg_ref[...], cb_ref[...])
    sw = sw * jax.nn.sigmoid(sw)
    mix = jnp.dot(sw.astype(BF16), wout_ref[...], preferred_element_type=F32)
    _post_mixer(x, mix, *post_w, x1_ref, info_ref, er_ref, cnt_out_ref, cnt_ref)


N_POST_W = 5


def _conv_prompt_kernel(x_ref, win_ref, wdw_ref, cg_ref, cb_ref, wout_ref, *rest):
    post_w, rest = rest[:N_POST_W], rest[N_POST_W:]
    _conv_mixer_body(x_ref, None, win_ref, wdw_ref, cg_ref, cb_ref, wout_ref, post_w, None, *rest,
                     nseg=1, seg=TM)


def _conv_sample_kernel(x_ref, st_ref, win_ref, wdw_ref, cg_ref, cb_ref, wout_ref, *rest):
    post_w, cnt_in_ref, rest = rest[:N_POST_W], rest[N_POST_W], rest[N_POST_W + 1 + N_ALIASED:]
    _conv_mixer_body(x_ref, st_ref, win_ref, wdw_ref, cg_ref, cb_ref, wout_ref, post_w, cnt_in_ref, *rest,
                     nseg=TM // DEC_SEQ, seg=DEC_SEQ)


def _full(shape):
    return pl.BlockSpec(shape, lambda *_: (0,) * len(shape))


ANY_SPEC = pl.BlockSpec(memory_space=pl.ANY)
CNT_SHAPE = jax.ShapeDtypeStruct((1, LANES), F32)
CNT_SCRATCH = pltpu.VMEM((1, LANES), F32)
PROMPT_TILES = T_PROMPT // TM
N_ALIASED = 3
TOKEN_OUT_SHAPES = [jax.ShapeDtypeStruct((T_ALL, D_MODEL), F32), jax.ShapeDtypeStruct((T_ALL, LANES), F32),
                    jax.ShapeDtypeStruct((T_ALL // TM, ER_ROWS, TM), jnp.int32)]


def _token_out_specs(tile):
    return [pl.BlockSpec((TM, D_MODEL), lambda *g: (tile(*g), 0)), pl.BlockSpec((TM, LANES), lambda *g: (tile(*g), 0)),
            pl.BlockSpec((1, ER_ROWS, TM), lambda *g: (tile(*g), 0, 0))]


def _conv_mixer(x_p, x_s, s_first, state_pad, w):
    w_specs = [_full(a.shape) for a in w]
    tiles_per_seq = SEQ // TM
    ptile = lambda b, t: b * tiles_per_seq + t
    x1, info, er, cnt, tail = pl.pallas_call(
        _conv_prompt_kernel,
        grid=(BATCH, tiles_per_seq),
        in_specs=[pl.BlockSpec((TM, D_MODEL), lambda b, t: (ptile(b, t), 0))] + w_specs,
        out_specs=_token_out_specs(ptile) + [_full(CNT_SHAPE.shape),
                                             pl.BlockSpec((1, STATE_PAD, D_MODEL), lambda b, t: (b, 0, 0))],
        out_shape=TOKEN_OUT_SHAPES + [CNT_SHAPE, jax.ShapeDtypeStruct((BATCH, STATE_PAD, D_MODEL), F32)],
        scratch_shapes=[CNT_SCRATCH, pltpu.VMEM((N_LANE_CHUNKS, 1, STATE_PAD + TM, LANES), F32),
                        pltpu.VMEM((N_LANE_CHUNKS, TM, LANES), F32)],
        compiler_params=pltpu.CompilerParams(dimension_semantics=("arbitrary", "arbitrary"),
                                             vmem_limit_bytes=VMEM_LIMIT),
        name="conv_mixer_prompt",
    )(x_p, *w)
    nb = TM // DEC_SEQ
    n_in = 2 + len(w) + 1
    x1, info, er, cnt, u_s = pl.pallas_call(
        _conv_sample_kernel,
        grid=(T_SAMPLE // TM,),
        in_specs=[pl.BlockSpec((TM, D_MODEL), lambda i: (s_first + i, 0)),
                  pl.BlockSpec((nb, STATE_PAD, D_MODEL), lambda i: (i, 0, 0))]
                 + w_specs + [_full(CNT_SHAPE.shape)] + [ANY_SPEC] * N_ALIASED,
        out_specs=_token_out_specs(lambda i: PROMPT_TILES + i)
                  + [_full(CNT_SHAPE.shape), pl.BlockSpec((TM, D_MODEL), lambda i: (i, 0))],
        out_shape=TOKEN_OUT_SHAPES + [CNT_SHAPE, jax.ShapeDtypeStruct((T_SAMPLE, D_MODEL), F32)],
        scratch_shapes=[CNT_SCRATCH, pltpu.VMEM((N_LANE_CHUNKS, nb, STATE_PAD + DEC_SEQ, LANES), F32),
                        pltpu.VMEM((N_LANE_CHUNKS, TM, LANES), F32)],
        input_output_aliases={n_in + k: k for k in range(N_ALIASED)},
        compiler_params=pltpu.CompilerParams(dimension_semantics=("arbitrary",),
                                             vmem_limit_bytes=VMEM_LIMIT),
        name="conv_mixer_sample",
    )(x_s, state_pad, *w, cnt, x1, info, er)
    return x1, info, er, cnt, tail, u_s


def _rope_lanes(v, c, sa, sb):
    width = v.shape[-1]
    half = ROT_DIM // 2
    return v * c + pltpu.roll(v, width - half, 1) * sa + pltpu.roll(v, half, 1) * sb


def _attn_mixer_body(sink_ref, x_ref, wq_ref, c_ref, sa_ref, sb_ref, kd_ref, vd_ref, wo_ref, post_w, cnt_in_ref,
                     x1_ref, info_ref, er_ref, cnt_out_ref, cnt_ref, obuf, bias_ref, *, nunit, uq, prompt):
    first_step = (pl.program_id(0) == 0) & (pl.program_id(1) == 0) if prompt else pl.program_id(0) == 0
    _init_counts(cnt_ref, first_step, cnt_in_ref)

    @pl.when(first_step)
    def _():
        shape = bias_ref.shape[1:]
        col = lax.broadcasted_iota(jnp.int32, shape, 1) & (KEY_WIN - 1)
        if prompt:
            chunk_shift = CHUNK.bit_length() - 1
            qchunk = (lax.broadcasted_iota(jnp.int32, shape, 0) & (uq - 1)) >> chunk_shift
            kchunk = col >> chunk_shift
            band = jnp.where(kchunk >= qchunk, jnp.where(kchunk <= qchunk + WINDOW // CHUNK, 0.0, NEG_INF), NEG_INF)
            bias_ref[0] = band
            bias_ref[1] = jnp.where(col >= WINDOW, band, NEG_INF)
        else:
            bias_ref[0] = jnp.where(col < WINDOW + DEC_SEQ, 0.0, NEG_INF)

    x = x_ref[...]
    q = jnp.dot(x.astype(BF16), wq_ref[...], preferred_element_type=F32)
    rep = D_MODEL // LANES
    q = _rope_lanes(q, jnp.tile(c_ref[...], (1, rep)), jnp.tile(sa_ref[...], (1, rep)),
                    jnp.tile(sb_ref[...], (1, rep)))
    qb = (q * ATTN_SCALE).astype(BF16)
    pairs = N_HEADS // N_KV_HEADS // 2
    lane_k = lax.broadcasted_iota(jnp.int32, (KEY_WIN, LANES), 1)
    lane_o = lax.broadcasted_iota(jnp.int32, (uq, LANES), 1)
    for un in range(nunit):
        r0 = un * uq
        if prompt:
            kstart = pl.multiple_of(pl.program_id(1) * (nunit * uq) + r0, LANES)
            bias = bias_ref[jnp.where(kstart == 0, 1, 0)]
        else:
            bias = bias_ref[0]
        for kvh in range(N_KV_HEADS):
            if prompt:
                kw = kd_ref[0, kvh, pl.ds(kstart, KEY_WIN), :]
                vw = vd_ref[0, kvh, pl.ds(kstart, KEY_WIN), :]
            else:
                kw = kd_ref[un, kvh]
                vw = vd_ref[un, kvh]
            zero = jnp.zeros_like(kw)
            kbd = jnp.concatenate([jnp.where(lane_k < HEAD_DIM, kw, zero),
                                   jnp.where(lane_k >= HEAD_DIM, kw, zero)], axis=0)
            vbd = jnp.concatenate([jnp.where(lane_k < HEAD_DIM, vw, zero),
                                   jnp.where(lane_k >= HEAD_DIM, vw, zero)], axis=0)
            q4 = jnp.concatenate([qb[r0:r0 + uq, (kvh * pairs + p) * LANES:(kvh * pairs + p + 1) * LANES]
                                  for p in range(pairs)], axis=0)
            s = lax.dot_general(q4, kbd, (((1,), (1,)), ((), ())), preferred_element_type=F32)
            s = s + bias
            p_parts, inv_parts = [], []
            for p in range(pairs):
                halves, invs = [], []
                for hf in range(2):
                    sk = sink_ref[(kvh * pairs + p) * 2 + hf]
                    sh = s[p * uq:(p + 1) * uq, hf * KEY_WIN:(hf + 1) * KEY_WIN]
                    m = jnp.maximum(jnp.max(sh, axis=-1, keepdims=True), sk)
                    pe = jnp.exp(sh - m)
                    den = jnp.sum(pe, axis=-1, keepdims=True) + jnp.exp(sk - m)
                    halves.append(pe.astype(BF16))
                    invs.append(1.0 / den)
                p_parts.append(jnp.concatenate(halves, axis=1))
                inv_parts.append(invs)
            pm = jnp.concatenate(p_parts, axis=0)
            o4 = jnp.dot(pm, vbd, preferred_element_type=F32)
            for p in range(pairs):
                inv = jnp.where(lane_o < HEAD_DIM, inv_parts[p][0], inv_parts[p][1])
                hp = kvh * pairs + p
                obuf[r0:r0 + uq, hp * LANES:(hp + 1) * LANES] = (o4[p * uq:(p + 1) * uq, :] * inv).astype(BF16)
    mix = jnp.dot(obuf[...], wo_ref[...], preferred_element_type=F32)
    _post_mixer(x, mix, *post_w, x1_ref, info_ref, er_ref, cnt_out_ref, cnt_ref)


N_ATTN_IN = 9


def _attn_prompt_kernel(*refs):
    ins, post_w, rest = refs[:N_ATTN_IN], refs[N_ATTN_IN:N_ATTN_IN + N_POST_W], refs[N_ATTN_IN + N_POST_W:]
    _attn_mixer_body(*ins, post_w, None, *rest, nunit=TM // UQ_PROMPT, uq=UQ_PROMPT, prompt=True)


def _attn_sample_kernel(*refs):
    ins, post_w = refs[:N_ATTN_IN], refs[N_ATTN_IN:N_ATTN_IN + N_POST_W]
    cnt_in_ref, rest = refs[N_ATTN_IN + N_POST_W], refs[N_ATTN_IN + N_POST_W + 1 + N_ALIASED:]
    _attn_mixer_body(*ins, post_w, cnt_in_ref, *rest, nunit=TM // DEC_SEQ, uq=DEC_SEQ, prompt=False)


def _attn_mixer(x, sinks, wq, wo, post_w, rope_p, rope_s, kv_p, kv_s):
    tiles_per_seq = SEQ // TM
    smem = pl.BlockSpec(memory_space=pltpu.SMEM)
    tail_specs = [_full(wo.shape)] + [_full(a.shape) for a in post_w]
    kd, vd = kv_p
    ptile = lambda b, t: b * tiles_per_seq + t
    pair_rows = N_HEADS // N_KV_HEADS // 2
    x1, info, er, cnt = pl.pallas_call(
        _attn_prompt_kernel,
        grid=(BATCH, tiles_per_seq),
        in_specs=[smem, pl.BlockSpec((TM, D_MODEL), lambda b, t: (ptile(b, t), 0)), _full(wq.shape)]
                 + [pl.BlockSpec((TM, LANES), lambda b, t: (t, 0))] * 3
                 + [pl.BlockSpec((1,) + kd.shape[1:], lambda b, t: (b, 0, 0, 0))] * 2 + tail_specs,
        out_specs=_token_out_specs(ptile) + [_full(CNT_SHAPE.shape)],
        out_shape=TOKEN_OUT_SHAPES + [CNT_SHAPE],
        scratch_shapes=[CNT_SCRATCH, pltpu.VMEM((TM, D_MODEL), BF16),
                        pltpu.VMEM((2, pair_rows * UQ_PROMPT, 2 * KEY_WIN), F32)],
        compiler_params=pltpu.CompilerParams(dimension_semantics=("arbitrary", "arbitrary"),
                                             vmem_limit_bytes=VMEM_LIMIT),
        name="attn_mixer_prompt",
    )(sinks, x, wq, *rope_p, kd, vd, wo, *post_w)
    nb = TM // DEC_SEQ
    kd, vd = kv_s
    n_in = N_ATTN_IN + N_POST_W + 1
    x1, info, er, cnt = pl.pallas_call(
        _attn_sample_kernel,
        grid=(T_SAMPLE // TM,),
        in_specs=[smem, pl.BlockSpec((TM, D_MODEL), lambda i: (PROMPT_TILES + i, 0)), _full(wq.shape)]
                 + [_full((TM, LANES))] * 3
                 + [pl.BlockSpec((nb,) + kd.shape[1:], lambda i: (i, 0, 0, 0))] * 2 + tail_specs
                 + [_full(CNT_SHAPE.shape)] + [ANY_SPEC] * N_ALIASED,
        out_specs=_token_out_specs(lambda i: PROMPT_TILES + i) + [_full(CNT_SHAPE.shape)],
        out_shape=TOKEN_OUT_SHAPES + [CNT_SHAPE],
        scratch_shapes=[CNT_SCRATCH, pltpu.VMEM((TM, D_MODEL), BF16),
                        pltpu.VMEM((1, pair_rows * DEC_SEQ, 2 * KEY_WIN), F32)],
        input_output_aliases={n_in + k: k for k in range(N_ALIASED)},
        compiler_params=pltpu.CompilerParams(dimension_semantics=("arbitrary",),
                                             vmem_limit_bytes=VMEM_LIMIT),
        name="attn_mixer_sample",
    )(sinks, x, wq, *rope_s, kd, vd, wo, *post_w, cnt, x1, info, er)
    return x1, info, er, cnt


N_TILES = T_ALL // TM


def _row_copy_wait(src_rows, dst_rows, sem):
    pltpu.make_async_copy(src_rows, dst_rows, sem).wait()


PAIR = 4
N_PAIRS = N_TILES // PAIR
assert N_TILES % PAIR == 0 and PROMPT_TILES % PAIR == 0
POS_BLOCK = (PAIR, 2, TM)


def _dispatch_kernel(zrow_ref, zflag_ref, pos_ref, x1_ref, xs_hbm, zbuf, sem, zsem):
    i = pl.program_id(0)

    @pl.when(i == 0)
    def _():
        zbuf[...] = jnp.zeros_like(zbuf)

        def zero_tile(e):
            return pltpu.make_async_copy(zbuf, xs_hbm.at[pl.ds(pl.multiple_of(zrow_ref[e], MOE_TM), MOE_TM)], zsem)

        for e in range(N_EXPERTS):
            @pl.when(zflag_ref[e] > 0)
            def _():
                zero_tile(e).start()
        for e in range(N_EXPERTS):
            @pl.when(zflag_ref[e] > 0)
            def _():
                zero_tile(e).wait()

    for h in range(PAIR):
        for j in range(TM):
            for slot in range(2):
                pltpu.make_async_copy(x1_ref.at[pl.ds(h * TM + j, 1)],
                                      xs_hbm.at[pl.ds(pos_ref[h, slot, j], 1)], sem).start(priority=slot)
    for slot in range(2):
        _row_copy_wait(x1_ref, xs_hbm.at[pl.ds(0, PAIR * TM)], sem)


def _dispatch(x1, pos, zrow, zflag):
    return pl.pallas_call(
        _dispatch_kernel,
        grid_spec=pltpu.PrefetchScalarGridSpec(
            num_scalar_prefetch=2,
            grid=(N_PAIRS,),
            in_specs=[pl.BlockSpec(POS_BLOCK, lambda i, *_: (i, 0, 0), memory_space=pltpu.SMEM),
                      pl.BlockSpec((PAIR * TM, D_MODEL), lambda i, *_: (i, 0))],
            out_specs=ANY_SPEC,
            scratch_shapes=[pltpu.VMEM((MOE_TM, D_MODEL), F32), pltpu.SemaphoreType.DMA,
                            pltpu.SemaphoreType.DMA],
        ),
        out_shape=jax.ShapeDtypeStruct((MOE_ROWS, D_MODEL), F32),
        compiler_params=pltpu.CompilerParams(dimension_semantics=("arbitrary",), vmem_limit_bytes=VMEM_LIMIT),
        name="moe_dispatch",
    )(zrow, zflag, pos, x1)


def _moe_kernel(te_ref, slot_ref, next_ref, nv_ref, xs_ref, wg_hbm, wu_hbm, wd_hbm, ys_ref,
                wg_st, wu_st, wd_st, wgu_bf, wd_bf, sem, *, base):
    i = pl.program_id(0)

    def fetch(e, slot):
        return (pltpu.make_async_copy(wg_hbm.at[base + e], wg_st.at[slot], sem.at[slot, 0]),
                pltpu.make_async_copy(wu_hbm.at[base + e], wu_st.at[slot], sem.at[slot, 1]),
                pltpu.make_async_copy(wd_hbm.at[base + e], wd_st.at[slot], sem.at[slot, 2]))

    @pl.when(i == 0)
    def _():
        for c in fetch(te_ref[0], slot_ref[0]):
            c.start()

    @pl.when(i < nv_ref[0])
    def _():
        e = te_ref[i]
        slot = slot_ref[i]

        @pl.when((i == 0) | (e != te_ref[jnp.maximum(i - 1, 0)]))
        def _():
            for c in fetch(e, slot):
                c.wait()
            wgu_bf[:, :EXPERT_FF] = wg_st[slot].astype(BF16)
            wgu_bf[:, EXPERT_FF:] = wu_st[slot].astype(BF16)
            wd_bf[...] = wd_st[slot].astype(BF16)

            @pl.when(next_ref[i] >= 0)
            def _():
                for c in fetch(next_ref[i], 1 - slot):
                    c.start()

        hgu = jnp.dot(xs_ref[...].astype(BF16), wgu_bf[...], preferred_element_type=F32)
        hg = hgu[:, :EXPERT_FF]
        h = hg * jax.nn.sigmoid(hg) * hgu[:, EXPERT_FF:]
        ys_ref[...] = jnp.dot(h.astype(BF16), wd_bf[...], preferred_element_type=F32)


def _moe_experts(xs, tile_expert, tile_slot, tile_next, n_valid, wg, wu, wd, layer):
    row = lambda i, te, sl, nx, nv: (jnp.minimum(i, nv[0] - 1), 0)
    return pl.pallas_call(
        functools.partial(_moe_kernel, base=layer * N_EXPERTS),
        grid_spec=pltpu.PrefetchScalarGridSpec(
            num_scalar_prefetch=4,
            grid=(MOE_TILES,),
            in_specs=[pl.BlockSpec((MOE_TM, D_MODEL), row), ANY_SPEC, ANY_SPEC, ANY_SPEC],
            out_specs=pl.BlockSpec((MOE_TM, D_MODEL), row),
            scratch_shapes=[pltpu.VMEM((2, D_MODEL, EXPERT_FF), F32), pltpu.VMEM((2, D_MODEL, EXPERT_FF), F32),
                            pltpu.VMEM((2, EXPERT_FF, D_MODEL), F32),
                            pltpu.VMEM((D_MODEL, 2 * EXPERT_FF), BF16), pltpu.VMEM((EXPERT_FF, D_MODEL), BF16),
                            pltpu.SemaphoreType.DMA((2, 3))],
        ),
        out_shape=jax.ShapeDtypeStruct((MOE_ROWS, D_MODEL), F32),
        compiler_params=pltpu.CompilerParams(dimension_semantics=("arbitrary",), vmem_limit_bytes=VMEM_LIMIT),
        name="moe_experts",
    )(tile_expert, tile_slot, tile_next, n_valid, xs, wg, wu, wd)


def _ffn_ln_kernel(pos_ref, pos_next_ref, x1_ref, info_ref, g_ref, b_ref, *rest, split, with_kv):
    kv_in, rest = (rest[:4], rest[4:]) if with_kv else ((), rest)
    ys_hbm, out_refs, bufs, sem = rest[0], rest[1:-(PAIR + 1)], rest[-(PAIR + 1):-1], rest[-1]
    i = pl.program_id(0)

    def gather(p_ref, h, buf):
        for j in range(TM):
            for slot in range(2):
                pltpu.make_async_copy(ys_hbm.at[pl.ds(p_ref[h, slot, j], 1)],
                                      bufs[buf].at[slot, pl.ds(j, 1)], sem.at[buf]).start(priority=slot)

    def drain(buf):
        for slot in range(2):
            _row_copy_wait(ys_hbm.at[pl.ds(0, TM)], bufs[buf].at[slot], sem.at[buf])

    @pl.when(i == 0)
    def _():
        gather(pos_ref, 0, 0)

    for h in range(PAIR):
        drain(h)
        if h + 1 < PAIR:
            gather(pos_ref, h + 1, h + 1)
        else:
            gather(pos_next_ref, 0, 0)
        rows = slice(h * TM, (h + 1) * TM)
        info = info_ref[rows, :]
        f = info[:, 2:3] * bufs[h][0] + info[:, 3:4] * bufs[h][1]
        x2 = _layer_norm(DEEPNORM_ALPHA * x1_ref[rows, :] + f, g_ref[...], b_ref[...])
        if with_kv:
            wkv_ref, c_ref, sa_ref, sb_ref = kv_in
            k_ref, v_ref = out_refs[-2:]
            kv = jnp.dot(x2.astype(BF16), wkv_ref[...], preferred_element_type=F32)
            k_ref[rows, :] = _rope_lanes(kv[:, :KV_WIDTH], c_ref[rows, :], sa_ref[rows, :], sb_ref[rows, :])
            v_ref[rows, :] = kv[:, KV_WIDTH:]
        if split:
            @pl.when(i < PROMPT_TILES // PAIR)
            def _():
                out_refs[0][rows, :] = x2

            @pl.when(i >= PROMPT_TILES // PAIR)
            def _():
                out_refs[1][rows, :] = x2
        else:
            out_refs[0][rows, :] = x2

    @pl.when(i == N_PAIRS - 1)
    def _():
        drain(0)


def _ffn_ln(x1, ys, pos, info, g, b, split, kv_w=None):
    tok = lambda i: (i, 0)
    rows = PAIR * TM
    prompt_steps = PROMPT_TILES // PAIR
    pos_spec = lambda fn: pl.BlockSpec(POS_BLOCK, fn, memory_space=pltpu.SMEM)
    if split:
        out_specs = [pl.BlockSpec((rows, D_MODEL), lambda i: (jnp.minimum(i, prompt_steps - 1), 0)),
                     pl.BlockSpec((rows, D_MODEL), lambda i: (jnp.maximum(i - prompt_steps, 0), 0))]
        out_shape = [jax.ShapeDtypeStruct((T_PROMPT, D_MODEL), F32), jax.ShapeDtypeStruct((T_SAMPLE, D_MODEL), F32)]
    else:
        out_specs = [pl.BlockSpec((rows, D_MODEL), tok)]
        out_shape = [jax.ShapeDtypeStruct((T_ALL, D_MODEL), F32)]
    kv_args, kv_specs = (), []
    if kv_w is not None:
        wkv, rope_all = kv_w
        kv_args = (wkv,) + tuple(rope_all)
        kv_specs = [_full(wkv.shape)] + [pl.BlockSpec((rows, LANES), tok)] * 3
        out_specs = out_specs + [pl.BlockSpec((rows, KV_WIDTH), tok)] * 2
        out_shape = out_shape + [jax.ShapeDtypeStruct((T_ALL, KV_WIDTH), F32)] * 2
    return pl.pallas_call(
        functools.partial(_ffn_ln_kernel, split=split, with_kv=kv_w is not None),
        grid=(N_PAIRS,),
        in_specs=[pos_spec(lambda i: (i, 0, 0)), pos_spec(lambda i: (jnp.minimum(i + 1, N_PAIRS - 1), 0, 0)),
                  pl.BlockSpec((rows, D_MODEL), tok), pl.BlockSpec((rows, LANES), tok), _full(g.shape),
                  _full(b.shape)] + kv_specs + [ANY_SPEC],
        out_specs=out_specs,
        out_shape=out_shape,
        scratch_shapes=[pltpu.VMEM((2, TM, D_MODEL), F32)] * PAIR + [pltpu.SemaphoreType.DMA((PAIR,))],
        compiler_params=pltpu.CompilerParams(dimension_semantics=("arbitrary",), vmem_limit_bytes=VMEM_LIMIT),
        name="ffn_ln",
    )(pos, pos, x1, info, g, b, *kv_args, ys)


def _moe_layer(x1, info, er, cnt, wg, wu, wd, layer, g, b, split, kv_w):
    counts = cnt[0, :N_EXPERTS].astype(jnp.int32)
    tiles_per = (counts + MOE_TM - 1) // MOE_TM
    tile_end = jnp.cumsum(tiles_per)
    pad_start = (tile_end - tiles_per) * MOE_TM
    ids = jnp.arange(N_EXPERTS, dtype=jnp.int32)
    half = ER_ROWS // 2
    hit = er[None, :, 0:2, :] == ids[:, None, None, None]
    pos = jnp.sum(jnp.where(hit, pad_start[:, None, None, None], 0), axis=0) + er[:, half:half + 2, :]
    n_valid = tile_end[-1:]
    tile_ids = jnp.minimum(jnp.arange(MOE_TILES, dtype=jnp.int32), n_valid[0] - 1)
    tile_expert = jnp.sum(tile_ids[:, None] >= tile_end[None, :], axis=1, dtype=jnp.int32)
    nonempty = tiles_per > 0
    slot_of = (jnp.cumsum(nonempty.astype(jnp.int32)) - 1) % 2
    later = jnp.where(nonempty[None, :] & (ids[None, :] > ids[:, None]), ids[None, :], N_EXPERTS)
    next_of = jnp.min(later, axis=1)
    next_of = jnp.where(next_of < N_EXPERTS, next_of, -1)
    zrow = jnp.maximum(tile_end - 1, 0) * MOE_TM
    mine = tile_expert[:, None] == ids[None, :]
    tile_slot = jnp.sum(jnp.where(mine, slot_of[None, :], 0), axis=1)
    tile_next = jnp.sum(jnp.where(mine, next_of[None, :], 0), axis=1)
    xs = _dispatch(x1, pos, zrow, tiles_per)
    ys = _moe_experts(xs, tile_expert, tile_slot, tile_next, n_valid, wg, wu, wd, layer)
    return _ffn_ln(x1, ys, pos, info, g, b, split, kv_w)


def _rope_tables(pos):
    half = ROT_DIM // 2
    inv_freq = ROPE_THETA ** (-jnp.arange(0, ROT_DIM, 2, dtype=F32) / ROT_DIM)
    ang = pos.astype(F32)[:, None] * inv_freq[None, :]
    cos, sin = jnp.cos(ang), jnp.sin(ang)
    n = pos.shape[0]
    ones = jnp.ones((n, HEAD_DIM - ROT_DIM), F32)
    zeros_h = jnp.zeros((n, half), F32)
    zeros_r = jnp.zeros((n, HEAD_DIM - ROT_DIM), F32)
    c = jnp.concatenate([cos, cos, ones], axis=1)
    sa = jnp.concatenate([-sin, zeros_h, zeros_r], axis=1)
    sb = jnp.concatenate([zeros_h, sin, zeros_r], axis=1)
    rep = LANES // HEAD_DIM
    return tuple(jnp.tile(t, (1, rep)) for t in (c, sa, sb))


def _dup_heads(kv):
    b, l, _ = kv.shape
    h = kv.reshape(b, l, N_KV_HEADS, HEAD_DIM).transpose(0, 2, 1, 3)
    return jnp.concatenate([h, h], axis=-1).astype(BF16)


def kernel(x_prompt, x_sample, state_conv, cache_k, cache_v, ln_mix_g, ln_mix_b, ln_ffn_g, ln_ffn_b, conv_w_in, conv_w_dw, conv_ln_g, conv_ln_b, conv_w_out, w_kv, attn_w_q, attn_sinks, attn_w_o, router_w_group, router_b_group, router_w_expert, router_b_expert, expert_w_gate, expert_w_up, expert_w_down):
    x = None
    wg = expert_w_gate.reshape(DEPTH * N_EXPERTS, D_MODEL, EXPERT_FF)
    wu = expert_w_up.reshape(DEPTH * N_EXPERTS, D_MODEL, EXPERT_FF)
    wd = expert_w_down.reshape(DEPTH * N_EXPERTS, EXPERT_FF, D_MODEL)
    row = lambda v: v.reshape(1, -1)

    def router_weights(layer):
        w = jnp.concatenate([router_w_group[layer], router_w_expert[layer]], axis=1)
        w = jnp.pad(w, ((0, 0), (0, LANES - w.shape[1])))
        w_hi = w.astype(BF16)
        w_lo = (w - w_hi.astype(F32)).astype(BF16)
        bias = jnp.concatenate([router_b_group[layer], router_b_expert[layer].reshape(-1)])
        return w_hi, w_lo, jnp.pad(bias, (0, LANES - bias.shape[0])).reshape(1, LANES)

    pos_p = jnp.arange(SEQ, dtype=jnp.int32)
    pos_s = PAST_LEN + jnp.arange(DEC_SEQ, dtype=jnp.int32)
    rope_p = _rope_tables(pos_p)
    rope_s1 = _rope_tables(pos_s)
    rope_s = tuple(jnp.tile(t, (TM // DEC_SEQ, 1)) for t in rope_s1)
    rope_all = tuple(jnp.concatenate([jnp.tile(p, (BATCH, 1)), jnp.tile(s, (DEC_BATCH, 1))], axis=0)
                     for p, s in zip(rope_p, rope_s1))

    tails, us = [], []
    kv_p = kv_s = None
    outs_kv = None
    for layer in range(DEPTH):
        post_w = (row(ln_mix_g[layer]), row(ln_mix_b[layer])) + router_weights(layer)
        if layer < N_A_LAYERS:
            st = jnp.pad(state_conv[layer], ((0, 0), (STATE_PAD - (CONV_WIDTH - 1), 0), (0, 0)))
            wdw = jnp.pad(conv_w_dw[layer], ((0, STATE_PAD - CONV_WIDTH), (0, 0)))
            wdw = wdw.reshape(STATE_PAD, N_LANE_CHUNKS, LANES).transpose(1, 0, 2)
            w = (conv_w_in[layer].astype(BF16), wdw, row(conv_ln_g[layer]), row(conv_ln_b[layer]),
                 conv_w_out[layer].astype(BF16)) + post_w
            if layer == 0:
                x_in = (x_prompt.reshape(T_PROMPT, D_MODEL), x_sample.reshape(T_SAMPLE, D_MODEL), 0)
            else:
                x_in = (x, x, PROMPT_TILES)
            x1, info, er, cnt, tail, u_s = _conv_mixer(*x_in, st, w)
            tails.append(tail[:, STATE_PAD - (CONV_WIDTH - 1):, :])
            us.append(u_s.reshape(DEC_BATCH, DEC_SEQ, D_MODEL)[:, DEC_SEQ - (CONV_WIDTH - 1):, :])
        else:
            j = layer - N_A_LAYERS
            x1, info, er, cnt = _attn_mixer(x, attn_sinks[j], attn_w_q[j].astype(BF16), attn_w_o[j].astype(BF16),
                                        post_w, rope_p, rope_s, kv_p, kv_s)
        last_a = layer == N_A_LAYERS - 1
        outs = _moe_layer(x1, info, er, cnt, wg, wu, wd, layer, row(ln_ffn_g[layer]), row(ln_ffn_b[layer]),
                          split=layer == DEPTH - 1, kv_w=(w_kv.astype(BF16), rope_all) if last_a else None)
        x = outs if layer == DEPTH - 1 else outs[0]
        if last_a:
            k, v = outs[1:]
            k_p = k[:T_PROMPT].reshape(BATCH, SEQ, KV_WIDTH)
            v_p = v[:T_PROMPT].reshape(BATCH, SEQ, KV_WIDTH)
            k_s = jnp.concatenate([cache_k.reshape(DEC_BATCH, WINDOW, KV_WIDTH),
                                   k[T_PROMPT:].reshape(DEC_BATCH, DEC_SEQ, KV_WIDTH)], axis=1)
            v_s = jnp.concatenate([cache_v.reshape(DEC_BATCH, WINDOW, KV_WIDTH),
                                   v[T_PROMPT:].reshape(DEC_BATCH, DEC_SEQ, KV_WIDTH)], axis=1)
            cache_shape = (-1, WINDOW, N_KV_HEADS, HEAD_DIM)
            outs_kv = (k_p[:, -WINDOW:].reshape(cache_shape), v_p[:, -WINDOW:].reshape(cache_shape),
                       k_s[:, -WINDOW:].reshape(cache_shape), v_s[:, -WINDOW:].reshape(cache_shape))
            front = ((0, 0), (WINDOW, 0), (0, 0))
            back = ((0, 0), (0, KEY_WIN - WINDOW - DEC_SEQ), (0, 0))
            kv_p = (_dup_heads(jnp.pad(k_p, front)), _dup_heads(jnp.pad(v_p, front)))
            kv_s = (_dup_heads(jnp.pad(k_s, back)), _dup_heads(jnp.pad(v_s, back)))
    y_prompt = x[0].reshape(BATCH, SEQ, D_MODEL)
    y_sample = x[1].reshape(DEC_BATCH, DEC_SEQ, D_MODEL)
    return (y_prompt, y_sample, jnp.stack(tails, axis=0), jnp.stack(us, axis=0)) + outs_kv
```

```python
import functools

import jax
import jax.numpy as jnp
from jax import lax
from jax.experimental import pallas as pl
from jax.experimental.pallas import tpu as pltpu

D_MODEL = 1024
BATCH = 2
SEQ = 8192
DEPTH = 4
DEC_BATCH = 32
DEC_SEQ = 32
PAST_LEN = 2048
CHUNK = 64
N_A_LAYERS = DEPTH // 2
CONV_WIDTH = 31
N_HEADS = 16
N_KV_HEADS = 2
HEAD_DIM = 64
KV_WIDTH = N_KV_HEADS * HEAD_DIM
WINDOW = 128
ROT_DIM = HEAD_DIM // 4
ROPE_THETA = 500000.0
N_GROUPS = 4
EXPERTS_PER_GROUP = 8
N_EXPERTS = N_GROUPS * EXPERTS_PER_GROUP
EXPERT_FF = D_MODEL // 2
DEEPNORM_ALPHA = (2 * DEPTH) ** 0.25
LN_EPS = 1e-5
ATTN_SCALE = HEAD_DIM ** -0.5
NEG_INF = -1e30

LANES = 128
N_LANE_CHUNKS = D_MODEL // LANES
T_PROMPT = BATCH * SEQ
T_SAMPLE = DEC_BATCH * DEC_SEQ
T_ALL = T_PROMPT + T_SAMPLE
TM = 256
STATE_PAD = 32
KEY_WIN = 256
UQ_PROMPT = 2 * CHUNK
MOE_TM = 512
MOE_ROWS = 2 * T_ALL + N_EXPERTS * MOE_TM
MOE_TILES = MOE_ROWS // MOE_TM
VMEM_LIMIT = 56 * 1024 * 1024

F32 = jnp.float32
BF16 = jnp.bfloat16


def _layer_norm(x, g, b):
    mu = jnp.mean(x, axis=-1, keepdims=True)
    xc = x - mu
    var = jnp.mean(xc * xc, axis=-1, keepdims=True)
    return xc * lax.rsqrt(var + LN_EPS) * g + b


def _router(x1, rwh_ref, rwl_ref, rb_ref, cnt_ref):
    x_hi = x1.astype(BF16)
    x_lo = (x1 - x_hi.astype(F32)).astype(BF16)
    w_hi = rwh_ref[...]
    logits = (jnp.dot(x_hi, w_hi, preferred_element_type=F32)
              + jnp.dot(x_hi, rwl_ref[...], preferred_element_type=F32)
              + jnp.dot(x_lo, w_hi, preferred_element_type=F32)) + rb_ref[...]
    lane_i = lax.broadcasted_iota(jnp.int32, logits.shape, 1)
    lane = lane_i.astype(F32)
    neg = jnp.float32(-3.0e38)
    none = jnp.float32(LANES)
    gl = jnp.where(lane_i < N_GROUPS, logits, neg)
    gmax = jnp.max(gl, axis=-1, keepdims=True)
    g_idx = jnp.min(jnp.where(gl == gmax, lane, none), axis=-1, keepdims=True)
    gsum = jnp.sum(jnp.where(lane_i < N_GROUPS, jnp.exp(gl - gmax), 0.0), axis=-1, keepdims=True)
    p_grp = 1.0 / gsum
    lo = N_GROUPS + EXPERTS_PER_GROUP * g_idx
    el = jnp.where(lane >= lo, jnp.where(lane < lo + EXPERTS_PER_GROUP, logits, neg), neg)
    l1 = jnp.max(el, axis=-1, keepdims=True)
    i1 = jnp.min(jnp.where(el == l1, lane, none), axis=-1, keepdims=True)
    el2 = jnp.where(lane == i1, neg, el)
    l2 = jnp.max(el2, axis=-1, keepdims=True)
    i2 = jnp.min(jnp.where(el2 == l2, lane, none), axis=-1, keepdims=True)
    t = jnp.exp(l2 - l1)
    inv = p_grp / (1.0 + t)
    e1 = i1 - N_GROUPS
    e2 = i2 - N_GROUPS
    rows = logits.shape[0]
    onehot = jnp.where(lane == e1, 1.0, jnp.where(lane == e2, 1.0, 0.0))
    tri = (lax.broadcasted_iota(jnp.int32, (rows, rows), 0) > lax.broadcasted_iota(jnp.int32, (rows, rows), 1))
    before = jnp.dot(jnp.where(tri, 1.0, 0.0).astype(BF16), onehot.astype(BF16),
                     preferred_element_type=F32) + cnt_ref[...]
    r1 = jnp.sum(jnp.where(lane == e1, before, 0.0), axis=-1, keepdims=True)
    r2 = jnp.sum(jnp.where(lane == e2, before, 0.0), axis=-1, keepdims=True)
    cnt_ref[...] = cnt_ref[...] + jnp.sum(onehot, axis=0, keepdims=True)
    out = jnp.where(lane_i == 4, r1, jnp.where(lane_i == 5, r2, 0.0))
    out = jnp.where(lane_i == 2, inv, jnp.where(lane_i == 3, inv * t, out))
    return jnp.where(lane_i == 0, e1, jnp.where(lane_i == 1, e2, out))


ER_ROWS = 8


def _post_mixer(x, mix, lg_ref, lb_ref, rwh_ref, rwl_ref, rb_ref, x1_ref, info_ref, er_ref, cnt_out_ref, cnt_ref):
    x1 = _layer_norm(DEEPNORM_ALPHA * x + mix, lg_ref[...], lb_ref[...])
    x1_ref[...] = x1
    info = _router(x1, rwh_ref, rwl_ref, rb_ref, cnt_ref)
    info_ref[...] = info
    er_ref[0] = jnp.transpose(info)[0:ER_ROWS, :].astype(jnp.int32)
    cnt_out_ref[...] = cnt_ref[...]


def _init_counts(cnt_ref, first_step, cnt_in_ref):
    @pl.when(first_step)
    def _():
        cnt_ref[...] = jnp.zeros_like(cnt_ref) if cnt_in_ref is None else cnt_in_ref[...]


def _conv_mixer_body(x_ref, st_ref, win_ref, wdw_ref, cg_ref, cb_ref, wout_ref, post_w, cnt_in_ref,
                     x1_ref, info_ref, er_ref, cnt_out_ref, u_ref, cnt_ref, ubuf, cbuf, *, nseg, seg):
    carry = st_ref is None
    first_step = (pl.program_id(0) == 0) & (pl.program_id(1) == 0) if carry else pl.program_id(0) == 0
    _init_counts(cnt_ref, first_step, cnt_in_ref)
    x = x_ref[...]
    h = jnp.dot(x.astype(BF16), win_ref[...], preferred_element_type=F32)
    u = h[:, :D_MODEL] * jax.nn.sigmoid(h[:, D_MODEL:])
    if carry:
        @pl.when(pl.program_id(1) == 0)
        def _():
            ubuf[:, 0, 0:STATE_PAD, :] = jnp.zeros((N_LANE_CHUNKS, STATE_PAD, LANES), F32)
        u_ref[0] = u[seg - STATE_PAD:, :]
    else:
        u_ref[...] = u
    for lc in range(N_LANE_CHUNKS):
        for s in range(nseg):
            ubuf[lc, s, STATE_PAD:STATE_PAD + seg, :] = u[s * seg:(s + 1) * seg, lc * LANES:(lc + 1) * LANES]
            if not carry:
                ubuf[lc, s, 0:STATE_PAD, :] = st_ref[s, :, lc * LANES:(lc + 1) * LANES]

    def conv_lane_chunk(lc, c):
        w = wdw_ref[lc]
        for s in range(nseg):
            acc = jnp.zeros((seg, LANES), F32)
            for k in range(CONV_WIDTH):
                off = k + STATE_PAD - (CONV_WIDTH - 1)
                acc = acc + w[k:k + 1, :] * ubuf[lc, s, off:off + seg, :]
            cbuf[lc, s * seg:(s + 1) * seg, :] = acc
        return c

    lax.fori_loop(0, N_LANE_CHUNKS, conv_lane_chunk, 0)
    if carry:
        for lc in range(N_LANE_CHUNKS):
            ubuf[lc, 0, 0:STATE_PAD, :] = u[seg - STATE_PAD:, lc * LANES:(lc + 1) * LANES]
    dw = jnp.concatenate([cbuf[lc] for lc in range(N_LANE_CHUNKS)], axis=1)
    sw = _layer_norm(dw, cg_ref[...], cb_ref[...])
    sw = sw * jax.nn.sigmoid(sw)
    mix = jnp.dot(sw.astype(BF16), wout_ref[...], preferred_element_type=F32)
    _post_mixer(x, mix, *post_w, x1_ref, info_ref, er_ref, cnt_out_ref, cnt_ref)


N_POST_W = 5


def _conv_prompt_kernel(x_ref, win_ref, wdw_ref, cg_ref, cb_ref, wout_ref, *rest):
    post_w, rest = rest[:N_POST_W], rest[N_POST_W:]
    _conv_mixer_body(x_ref, None, win_ref, wdw_ref, cg_ref, cb_ref, wout_ref, post_w, None, *rest,
                     nseg=1, seg=TM)


def _conv_sample_kernel(x_ref, st_ref, win_ref, wdw_ref, cg_ref, cb_ref, wout_ref, *rest):
    post_w, cnt_in_ref, rest = rest[:N_POST_W], rest[N_POST_W], rest[N_POST_W + 1 + N_ALIASED:]
    _conv_mixer_body(x_ref, st_ref, win_ref, wdw_ref, cg_ref, cb_ref, wout_ref, post_w, cnt_in_ref, *rest,
                     nseg=TM // DEC_SEQ, seg=DEC_SEQ)


def _full(shape):
    return pl.BlockSpec(shape, lambda *_: (0,) * len(shape))


ANY_SPEC = pl.BlockSpec(memory_space=pl.ANY)
CNT_SHAPE = jax.ShapeDtypeStruct((1, LANES), F32)
CNT_SCRATCH = pltpu.VMEM((1, LANES), F32)
PROMPT_TILES = T_PROMPT // TM
N_ALIASED = 3
TOKEN_OUT_SHAPES = [jax.ShapeDtypeStruct((T_ALL, D_MODEL), F32), jax.ShapeDtypeStruct((T_ALL, LANES), F32),
                    jax.ShapeDtypeStruct((T_ALL // TM, ER_ROWS, TM), jnp.int32)]


def _token_out_specs(tile):
    return [pl.BlockSpec((TM, D_MODEL), lambda *g: (tile(*g), 0)), pl.BlockSpec((TM, LANES), lambda *g: (tile(*g), 0)),
            pl.BlockSpec((1, ER_ROWS, TM), lambda *g: (tile(*g), 0, 0))]


def _conv_mixer(x_p, x_s, s_first, state_pad, w):
    w_specs = [_full(a.shape) for a in w]
    tiles_per_seq = SEQ // TM
    ptile = lambda b, t: b * tiles_per_seq + t
    x1, info, er, cnt, tail = pl.pallas_call(
        _conv_prompt_kernel,
        grid=(BATCH, tiles_per_seq),
        in_specs=[pl.BlockSpec((TM, D_MODEL), lambda b, t: (ptile(b, t), 0))] + w_specs,
        out_specs=_token_out_specs(ptile) + [_full(CNT_SHAPE.shape),
                                             pl.BlockSpec((1, STATE_PAD, D_MODEL), lambda b, t: (b, 0, 0))],
        out_shape=TOKEN_OUT_SHAPES + [CNT_SHAPE, jax.ShapeDtypeStruct((BATCH, STATE_PAD, D_MODEL), F32)],
        scratch_shapes=[CNT_SCRATCH, pltpu.VMEM((N_LANE_CHUNKS, 1, STATE_PAD + TM, LANES), F32),
                        pltpu.VMEM((N_LANE_CHUNKS, TM, LANES), F32)],
        compiler_params=pltpu.CompilerParams(dimension_semantics=("arbitrary", "arbitrary"),
                                             vmem_limit_bytes=VMEM_LIMIT),
        name="conv_mixer_prompt",
    )(x_p, *w)
    nb = TM // DEC_SEQ
    n_in = 2 + len(w) + 1
    x1, info, er, cnt, u_s = pl.pallas_call(
        _conv_sample_kernel,
        grid=(T_SAMPLE // TM,),
        in_specs=[pl.BlockSpec((TM, D_MODEL), lambda i: (s_first + i, 0)),
                  pl.BlockSpec((nb, STATE_PAD, D_MODEL), lambda i: (i, 0, 0))]
                 + w_specs + [_full(CNT_SHAPE.shape)] + [ANY_SPEC] * N_ALIASED,
        out_specs=_token_out_specs(lambda i: PROMPT_TILES + i)
                  + [_full(CNT_SHAPE.shape), pl.BlockSpec((TM, D_MODEL), lambda i: (i, 0))],
        out_shape=TOKEN_OUT_SHAPES + [CNT_SHAPE, jax.ShapeDtypeStruct((T_SAMPLE, D_MODEL), F32)],
        scratch_shapes=[CNT_SCRATCH, pltpu.VMEM((N_LANE_CHUNKS, nb, STATE_PAD + DEC_SEQ, LANES), F32),
                        pltpu.VMEM((N_LANE_CHUNKS, TM, LANES), F32)],
        input_output_aliases={n_in + k: k for k in range(N_ALIASED)},
        compiler_params=pltpu.CompilerParams(dimension_semantics=("arbitrary",),
                                             vmem_limit_bytes=VMEM_LIMIT),
        name="conv_mixer_sample",
    )(x_s, state_pad, *w, cnt, x1, info, er)
    return x1, info, er, cnt, tail, u_s


def _rope_lanes(v, c, sa, sb):
    width = v.shape[-1]
    half = ROT_DIM // 2
    return v * c + pltpu.roll(v, width - half, 1) * sa + pltpu.roll(v, half, 1) * sb


def _attn_mixer_body(sink_ref, x_ref, wq_ref, c_ref, sa_ref, sb_ref, kd_ref, vd_ref, wo_ref, post_w, cnt_in_ref,
                     x1_ref, info_ref, er_ref, cnt_out_ref, cnt_ref, obuf, bias_ref, *, nunit, uq, prompt):
    first_step = (pl.program_id(0) == 0) & (pl.program_id(1) == 0) if prompt else pl.program_id(0) == 0
    _init_counts(cnt_ref, first_step, cnt_in_ref)

    @pl.when(first_step)
    def _():
        shape = bias_ref.shape[1:]
        col = lax.broadcasted_iota(jnp.int32, shape, 1) & (KEY_WIN - 1)
        if prompt:
            chunk_shift = CHUNK.bit_length() - 1
            qchunk = (lax.broadcasted_iota(jnp.int32, shape, 0) & (uq - 1)) >> chunk_shift
            kchunk = col >> chunk_shift
            band = jnp.where(kchunk >= qchunk, jnp.where(kchunk <= qchunk + WINDOW // CHUNK, 0.0, NEG_INF), NEG_INF)
            bias_ref[0] = band
            bias_ref[1] = jnp.where(col >= WINDOW, band, NEG_INF)
        else:
            bias_ref[0] = jnp.where(col < WINDOW + DEC_SEQ, 0.0, NEG_INF)

    x = x_ref[...]
    q = jnp.dot(x.astype(BF16), wq_ref[...], preferred_element_type=F32)
    rep = D_MODEL // LANES
    q = _rope_lanes(q, jnp.tile(c_ref[...], (1, rep)), jnp.tile(sa_ref[...], (1, rep)),
                    jnp.tile(sb_ref[...], (1, rep)))
    qb = (q * ATTN_SCALE).astype(BF16)
    pairs = N_HEADS // N_KV_HEADS // 2
    lane_k = lax.broadcasted_iota(jnp.int32, (KEY_WIN, LANES), 1)
    lane_o = lax.broadcasted_iota(jnp.int32, (uq, LANES), 1)
    for un in range(nunit):
        r0 = un * uq
        if prompt:
            kstart = pl.multiple_of(pl.program_id(1) * (nunit * uq) + r0, LANES)
            bias = bias_ref[jnp.where(kstart == 0, 1, 0)]
        else:
            bias = bias_ref[0]
        for kvh in range(N_KV_HEADS):
            if prompt:
                kw = kd_ref[0, kvh, pl.ds(kstart, KEY_WIN), :]
                vw = vd_ref[0, kvh, pl.ds(kstart, KEY_WIN), :]
            else:
                kw = kd_ref[un, kvh]
                vw = vd_ref[un, kvh]
            zero = jnp.zeros_like(kw)
            kbd = jnp.concatenate([jnp.where(lane_k < HEAD_DIM, kw, zero),
                                   jnp.where(lane_k >= HEAD_DIM, kw, zero)], axis=0)
            vbd = jnp.concatenate([jnp.where(lane_k < HEAD_DIM, vw, zero),
                                   jnp.where(lane_k >= HEAD_DIM, vw, zero)], axis=0)
            q4 = jnp.concatenate([qb[r0:r0 + uq, (kvh * pairs + p) * LANES:(kvh * pairs + p + 1) * LANES]
                                  for p in range(pairs)], axis=0)
            s = lax.dot_general(q4, kbd, (((1,), (1,)), ((), ())), preferred_element_type=F32)
            s = s + bias
            p_parts, inv_parts = [], []
            for p in range(pairs):
                halves, invs = [], []
                for hf in range(2):
                    sk = sink_ref[(kvh * pairs + p) * 2 + hf]
                    sh = s[p * uq:(p + 1) * uq, hf * KEY_WIN:(hf + 1) * KEY_WIN]
                    m = jnp.maximum(jnp.max(sh, axis=-1, keepdims=True), sk)
                    pe = jnp.exp(sh - m)
                    den = jnp.sum(pe, axis=-1, keepdims=True) + jnp.exp(sk - m)
                    halves.append(pe.astype(BF16))
                    invs.append(1.0 / den)
                p_parts.append(jnp.concatenate(halves, axis=1))
                inv_parts.append(invs)
            pm = jnp.concatenate(p_parts, axis=0)
            o4 = jnp.dot(pm, vbd, preferred_element_type=F32)
            for p in range(pairs):
                inv = jnp.where(lane_o < HEAD_DIM, inv_parts[p][0], inv_parts[p][1])
                hp = kvh * pairs + p
                obuf[r0:r0 + uq, hp * LANES:(hp + 1) * LANES] = (o4[p * uq:(p + 1) * uq, :] * inv).astype(BF16)
    mix = jnp.dot(obuf[...], wo_ref[...], preferred_element_type=F32)
    _post_mixer(x, mix, *post_w, x1_ref, info_ref, er_ref, cnt_out_ref, cnt_ref)


N_ATTN_IN = 9


def _attn_prompt_kernel(*refs):
    ins, post_w, rest = refs[:N_ATTN_IN], refs[N_ATTN_IN:N_ATTN_IN + N_POST_W], refs[N_ATTN_IN + N_POST_W:]
    _attn_mixer_body(*ins, post_w, None, *rest, nunit=TM // UQ_PROMPT, uq=UQ_PROMPT, prompt=True)


def _attn_sample_kernel(*refs):
    ins, post_w = refs[:N_ATTN_IN], refs[N_ATTN_IN:N_ATTN_IN + N_POST_W]
    cnt_in_ref, rest = refs[N_ATTN_IN + N_POST_W], refs[N_ATTN_IN + N_POST_W + 1 + N_ALIASED:]
    _attn_mixer_body(*ins, post_w, cnt_in_ref, *rest, nunit=TM // DEC_SEQ, uq=DEC_SEQ, prompt=False)


def _attn_mixer(x, sinks, wq, wo, post_w, rope_p, rope_s, kv_p, kv_s):
    tiles_per_seq = SEQ // TM
    smem = pl.BlockSpec(memory_space=pltpu.SMEM)
    tail_specs = [_full(wo.shape)] + [_full(a.shape) for a in post_w]
    kd, vd = kv_p
    ptile = lambda b, t: b * tiles_per_seq + t
    pair_rows = N_HEADS // N_KV_HEADS // 2
    x1, info, er, cnt = pl.pallas_call(
        _attn_prompt_kernel,
        grid=(BATCH, tiles_per_seq),
        in_specs=[smem, pl.BlockSpec((TM, D_MODEL), lambda b, t: (ptile(b, t), 0)), _full(wq.shape)]
                 + [pl.BlockSpec((TM, LANES), lambda b, t: (t, 0))] * 3
                 + [pl.BlockSpec((1,) + kd.shape[1:], lambda b, t: (b, 0, 0, 0))] * 2 + tail_specs,
        out_specs=_token_out_specs(ptile) + [_full(CNT_SHAPE.shape)],
        out_shape=TOKEN_OUT_SHAPES + [CNT_SHAPE],
        scratch_shapes=[CNT_SCRATCH, pltpu.VMEM((TM, D_MODEL), BF16),
                        pltpu.VMEM((2, pair_rows * UQ_PROMPT, 2 * KEY_WIN), F32)],
        compiler_params=pltpu.CompilerParams(dimension_semantics=("arbitrary", "arbitrary"),
                                             vmem_limit_bytes=VMEM_LIMIT),
        name="attn_mixer_prompt",
    )(sinks, x, wq, *rope_p, kd, vd, wo, *post_w)
    nb = TM // DEC_SEQ
    kd, vd = kv_s
    n_in = N_ATTN_IN + N_POST_W + 1
    x1, info, er, cnt = pl.pallas_call(
        _attn_sample_kernel,
        grid=(T_SAMPLE // TM,),
        in_specs=[smem, pl.BlockSpec((TM, D_MODEL), lambda i: (PROMPT_TILES + i, 0)), _full(wq.shape)]
                 + [_full((TM, LANES))] * 3
                 + [pl.BlockSpec((nb,) + kd.shape[1:], lambda i: (i, 0, 0, 0))] * 2 + tail_specs
                 + [_full(CNT_SHAPE.shape)] + [ANY_SPEC] * N_ALIASED,
        out_specs=_token_out_specs(lambda i: PROMPT_TILES + i) + [_full(CNT_SHAPE.shape)],
        out_shape=TOKEN_OUT_SHAPES + [CNT_SHAPE],
        scratch_shapes=[CNT_SCRATCH, pltpu.VMEM((TM, D_MODEL), BF16),
                        pltpu.VMEM((1, pair_rows * DEC_SEQ, 2 * KEY_WIN), F32)],
        input_output_aliases={n_in + k: k for k in range(N_ALIASED)},
        compiler_params=pltpu.CompilerParams(dimension_semantics=("arbitrary",),
                                             vmem_limit_bytes=VMEM_LIMIT),
        name="attn_mixer_sample",
    )(sinks, x, wq, *rope_s, kd, vd, wo, *post_w, cnt, x1, info, er)
    return x1, info, er, cnt


N_TILES = T_ALL // TM


def _row_copy_wait(src_rows, dst_rows, sem):
    pltpu.make_async_copy(src_rows, dst_rows, sem).wait()


PAIR = 4
N_PAIRS = N_TILES // PAIR
assert N_TILES % PAIR == 0 and PROMPT_TILES % PAIR == 0
GATHER_LEAD = 2
assert 0 < GATHER_LEAD < PAIR
POS_BLOCK = (PAIR, 2, TM)


def _dispatch_kernel(zrow_ref, zflag_ref, pos_ref, x1_ref, xs_hbm, zbuf, sem, zsem):
    i = pl.program_id(0)

    @pl.when(i == 0)
    def _():
        zbuf[...] = jnp.zeros_like(zbuf)

        def zero_tile(e):
            return pltpu.make_async_copy(zbuf, xs_hbm.at[pl.ds(pl.multiple_of(zrow_ref[e], MOE_TM), MOE_TM)], zsem)

        for e in range(N_EXPERTS):
            @pl.when(zflag_ref[e] > 0)
            def _():
                zero_tile(e).start()
        for e in range(N_EXPERTS):
            @pl.when(zflag_ref[e] > 0)
            def _():
                zero_tile(e).wait()

    for h in range(PAIR):
        for j in range(TM):
            for slot in range(2):
                pltpu.make_async_copy(x1_ref.at[pl.ds(h * TM + j, 1)],
                                      xs_hbm.at[pl.ds(pos_ref[h, slot, j], 1)], sem).start(priority=slot)
    for slot in range(2):
        _row_copy_wait(x1_ref, xs_hbm.at[pl.ds(0, PAIR * TM)], sem)


def _dispatch(x1, pos, zrow, zflag):
    return pl.pallas_call(
        _dispatch_kernel,
        grid_spec=pltpu.PrefetchScalarGridSpec(
            num_scalar_prefetch=2,
            grid=(N_PAIRS,),
            in_specs=[pl.BlockSpec(POS_BLOCK, lambda i, *_: (i, 0, 0), memory_space=pltpu.SMEM),
                      pl.BlockSpec((PAIR * TM, D_MODEL), lambda i, *_: (i, 0))],
            out_specs=ANY_SPEC,
            scratch_shapes=[pltpu.VMEM((MOE_TM, D_MODEL), F32), pltpu.SemaphoreType.DMA,
                            pltpu.SemaphoreType.DMA],
        ),
        out_shape=jax.ShapeDtypeStruct((MOE_ROWS, D_MODEL), F32),
        compiler_params=pltpu.CompilerParams(dimension_semantics=("arbitrary",), vmem_limit_bytes=VMEM_LIMIT),
        name="moe_dispatch",
    )(zrow, zflag, pos, x1)


def _moe_kernel(te_ref, slot_ref, next_ref, nv_ref, xs_ref, wg_hbm, wu_hbm, wd_hbm, ys_ref,
                wg_st, wu_st, wd_st, wgu_bf, wd_bf, sem, *, base):
    i = pl.program_id(0)

    def fetch(e, slot):
        return (pltpu.make_async_copy(wg_hbm.at[base + e], wg_st.at[slot], sem.at[slot, 0]),
                pltpu.make_async_copy(wu_hbm.at[base + e], wu_st.at[slot], sem.at[slot, 1]),
                pltpu.make_async_copy(wd_hbm.at[base + e], wd_st.at[slot], sem.at[slot, 2]))

    @pl.when(i == 0)
    def _():
        for c in fetch(te_ref[0], slot_ref[0]):
            c.start()

    @pl.when(i < nv_ref[0])
    def _():
        e = te_ref[i]
        slot = slot_ref[i]

        @pl.when((i == 0) | (e != te_ref[jnp.maximum(i - 1, 0)]))
        def _():
            for c in fetch(e, slot):
                c.wait()
            wgu_bf[:, :EXPERT_FF] = wg_st[slot].astype(BF16)
            wgu_bf[:, EXPERT_FF:] = wu_st[slot].astype(BF16)
            wd_bf[...] = wd_st[slot].astype(BF16)

            @pl.when(next_ref[i] >= 0)
            def _():
                for c in fetch(next_ref[i], 1 - slot):
                    c.start()

        hgu = jnp.dot(xs_ref[...].astype(BF16), wgu_bf[...], preferred_element_type=F32)
        hg = hgu[:, :EXPERT_FF]
        h = hg * jax.nn.sigmoid(hg) * hgu[:, EXPERT_FF:]
        ys_ref[...] = jnp.dot(h.astype(BF16), wd_bf[...], preferred_element_type=F32)


def _moe_experts(xs, tile_expert, tile_slot, tile_next, n_valid, wg, wu, wd, layer):
    row = lambda i, te, sl, nx, nv: (jnp.minimum(i, nv[0] - 1), 0)
    return pl.pallas_call(
        functools.partial(_moe_kernel, base=layer * N_EXPERTS),
        grid_spec=pltpu.PrefetchScalarGridSpec(
            num_scalar_prefetch=4,
            grid=(MOE_TILES,),
            in_specs=[pl.BlockSpec((MOE_TM, D_MODEL), row), ANY_SPEC, ANY_SPEC, ANY_SPEC],
            out_specs=pl.BlockSpec((MOE_TM, D_MODEL), row),
            scratch_shapes=[pltpu.VMEM((2, D_MODEL, EXPERT_FF), F32), pltpu.VMEM((2, D_MODEL, EXPERT_FF), F32),
                            pltpu.VMEM((2, EXPERT_FF, D_MODEL), F32),
                            pltpu.VMEM((D_MODEL, 2 * EXPERT_FF), BF16), pltpu.VMEM((EXPERT_FF, D_MODEL), BF16),
                            pltpu.SemaphoreType.DMA((2, 3))],
        ),
        out_shape=jax.ShapeDtypeStruct((MOE_ROWS, D_MODEL), F32),
        compiler_params=pltpu.CompilerParams(dimension_semantics=("arbitrary",), vmem_limit_bytes=VMEM_LIMIT),
        name="moe_experts",
    )(tile_expert, tile_slot, tile_next, n_valid, xs, wg, wu, wd)


def _ffn_ln_kernel(pos_ref, pos_next_ref, x1_ref, info_ref, g_ref, b_ref, *rest, split, with_kv):
    kv_in, rest = (rest[:4], rest[4:]) if with_kv else ((), rest)
    ys_hbm, out_refs, bufs, sem = rest[0], rest[1:-(PAIR + 1)], rest[-(PAIR + 1):-1], rest[-1]
    i = pl.program_id(0)

    def gather(p_ref, h, buf):
        for j in range(TM):
            for slot in range(2):
                pltpu.make_async_copy(ys_hbm.at[pl.ds(p_ref[h, slot, j], 1)],
                                      bufs[buf].at[slot, pl.ds(j, 1)], sem.at[buf]).start(priority=slot)

    def drain(buf):
        for slot in range(2):
            _row_copy_wait(ys_hbm.at[pl.ds(0, TM)], bufs[buf].at[slot], sem.at[buf])

    @pl.when(i == 0)
    def _():
        for h in range(GATHER_LEAD):
            gather(pos_ref, h, h)

    for h in range(PAIR):
        drain(h)
        ahead = h + GATHER_LEAD
        if ahead < PAIR:
            gather(pos_ref, ahead, ahead)
        else:
            gather(pos_next_ref, ahead - PAIR, ahead - PAIR)
        rows = slice(h * TM, (h + 1) * TM)
        info = info_ref[rows, :]
        f = info[:, 2:3] * bufs[h][0] + info[:, 3:4] * bufs[h][1]
        x2 = _layer_norm(DEEPNORM_ALPHA * x1_ref[rows, :] + f, g_ref[...], b_ref[...])
        if with_kv:
            wkv_ref, c_ref, sa_ref, sb_ref = kv_in
            k_ref, v_ref = out_refs[-2:]
            kv = jnp.dot(x2.astype(BF16), wkv_ref[...], preferred_element_type=F32)
            k_ref[rows, :] = _rope_lanes(kv[:, :KV_WIDTH], c_ref[rows, :], sa_ref[rows, :], sb_ref[rows, :])
            v_ref[rows, :] = kv[:, KV_WIDTH:]
        if split:
            @pl.when(i < PROMPT_TILES // PAIR)
            def _():
                out_refs[0][rows, :] = x2

            @pl.when(i >= PROMPT_TILES // PAIR)
            def _():
                out_refs[1][rows, :] = x2
        else:
            out_refs[0][rows, :] = x2

    @pl.when(i == N_PAIRS - 1)
    def _():
        for h in range(GATHER_LEAD):
            drain(h)


def _ffn_ln(x1, ys, pos, info, g, b, split, kv_w=None):
    tok = lambda i: (i, 0)
    rows = PAIR * TM
    prompt_steps = PROMPT_TILES // PAIR
    pos_spec = lambda fn: pl.BlockSpec(POS_BLOCK, fn, memory_space=pltpu.SMEM)
    if split:
        out_specs = [pl.BlockSpec((rows, D_MODEL), lambda i: (jnp.minimum(i, prompt_steps - 1), 0)),
                     pl.BlockSpec((rows, D_MODEL), lambda i: (jnp.maximum(i - prompt_steps, 0), 0))]
        out_shape = [jax.ShapeDtypeStruct((T_PROMPT, D_MODEL), F32), jax.ShapeDtypeStruct((T_SAMPLE, D_MODEL), F32)]
    else:
        out_specs = [pl.BlockSpec((rows, D_MODEL), tok)]
        out_shape = [jax.ShapeDtypeStruct((T_ALL, D_MODEL), F32)]
    kv_args, kv_specs = (), []
    if kv_w is not None:
        wkv, rope_all = kv_w
        kv_args = (wkv,) + tuple(rope_all)
        kv_specs = [_full(wkv.shape)] + [pl.BlockSpec((rows, LANES), tok)] * 3
        out_specs = out_specs + [pl.BlockSpec((rows, KV_WIDTH), tok)] * 2
        out_shape = out_shape + [jax.ShapeDtypeStruct((T_ALL, KV_WIDTH), F32)] * 2
    return pl.pallas_call(
        functools.partial(_ffn_ln_kernel, split=split, with_kv=kv_w is not None),
        grid=(N_PAIRS,),
        in_specs=[pos_spec(lambda i: (i, 0, 0)), pos_spec(lambda i: (jnp.minimum(i + 1, N_PAIRS - 1), 0, 0)),
                  pl.BlockSpec((rows, D_MODEL), tok), pl.BlockSpec((rows, LANES), tok), _full(g.shape),
                  _full(b.shape)] + kv_specs + [ANY_SPEC],
        out_specs=out_specs,
        out_shape=out_shape,
        scratch_shapes=[pltpu.VMEM((2, TM, D_MODEL), F32)] * PAIR + [pltpu.SemaphoreType.DMA((PAIR,))],
        compiler_params=pltpu.CompilerParams(dimension_semantics=("arbitrary",), vmem_limit_bytes=VMEM_LIMIT),
        name="ffn_ln",
    )(pos, pos, x1, info, g, b, *kv_args, ys)


def _moe_layer(x1, info, er, cnt, wg, wu, wd, layer, g, b, split, kv_w):
    counts = cnt[0, :N_EXPERTS].astype(jnp.int32)
    tiles_per = (counts + MOE_TM - 1) // MOE_TM
    tile_end = jnp.cumsum(tiles_per)
    pad_start = (tile_end - tiles_per) * MOE_TM
    ids = jnp.arange(N_EXPERTS, dtype=jnp.int32)
    half = ER_ROWS // 2
    hit = er[None, :, 0:2, :] == ids[:, None, None, None]
    pos = jnp.sum(jnp.where(hit, pad_start[:, None, None, None], 0), axis=0) + er[:, half:half + 2, :]
    n_valid = tile_end[-1:]
    tile_ids = jnp.minimum(jnp.arange(MOE_TILES, dtype=jnp.int32), n_valid[0] - 1)
    tile_expert = jnp.sum(tile_ids[:, None] >= tile_end[None, :], axis=1, dtype=jnp.int32)
    nonempty = tiles_per > 0
    slot_of = (jnp.cumsum(nonempty.astype(jnp.int32)) - 1) % 2
    later = jnp.where(nonempty[None, :] & (ids[None, :] > ids[:, None]), ids[None, :], N_EXPERTS)
    next_of = jnp.min(later, axis=1)
    next_of = jnp.where(next_of < N_EXPERTS, next_of, -1)
    zrow = jnp.maximum(tile_end - 1, 0) * MOE_TM
    mine = tile_expert[:, None] == ids[None, :]
    tile_slot = jnp.sum(jnp.where(mine, slot_of[None, :], 0), axis=1)
    tile_next = jnp.sum(jnp.where(mine, next_of[None, :], 0), axis=1)
    xs = _dispatch(x1, pos, zrow, tiles_per)
    ys = _moe_experts(xs, tile_expert, tile_slot, tile_next, n_valid, wg, wu, wd, layer)
    return _ffn_ln(x1, ys, pos, info, g, b, split, kv_w)


def _rope_tables(pos):
    half = ROT_DIM // 2
    inv_freq = ROPE_THETA ** (-jnp.arange(0, ROT_DIM, 2, dtype=F32) / ROT_DIM)
    ang = pos.astype(F32)[:, None] * inv_freq[None, :]
    cos, sin = jnp.cos(ang), jnp.sin(ang)
    n = pos.shape[0]
    ones = jnp.ones((n, HEAD_DIM - ROT_DIM), F32)
    zeros_h = jnp.zeros((n, half), F32)
    zeros_r = jnp.zeros((n, HEAD_DIM - ROT_DIM), F32)
    c = jnp.concatenate([cos, cos, ones], axis=1)
    sa = jnp.concatenate([-sin, zeros_h, zeros_r], axis=1)
    sb = jnp.concatenate([zeros_h, sin, zeros_r], axis=1)
    rep = LANES // HEAD_DIM
    return tuple(jnp.tile(t, (1, rep)) for t in (c, sa, sb))


def _dup_heads(kv):
    b, l, _ = kv.shape
    h = kv.reshape(b, l, N_KV_HEADS, HEAD_DIM).transpose(0, 2, 1, 3)
    return jnp.concatenate([h, h], axis=-1).astype(BF16)


def kernel(x_prompt, x_sample, state_conv, cache_k, cache_v, ln_mix_g, ln_mix_b, ln_ffn_g, ln_ffn_b, conv_w_in, conv_w_dw, conv_ln_g, conv_ln_b, conv_w_out, w_kv, attn_w_q, attn_sinks, attn_w_o, router_w_group, router_b_group, router_w_expert, router_b_expert, expert_w_gate, expert_w_up, expert_w_down):
    x = None
    wg = expert_w_gate.reshape(DEPTH * N_EXPERTS, D_MODEL, EXPERT_FF)
    wu = expert_w_up.reshape(DEPTH * N_EXPERTS, D_MODEL, EXPERT_FF)
    wd = expert_w_down.reshape(DEPTH * N_EXPERTS, EXPERT_FF, D_MODEL)
    row = lambda v: v.reshape(1, -1)

    def router_weights(layer):
        w = jnp.concatenate([router_w_group[layer], router_w_expert[layer]], axis=1)
        w = jnp.pad(w, ((0, 0), (0, LANES - w.shape[1])))
        w_hi = w.astype(BF16)
        w_lo = (w - w_hi.astype(F32)).astype(BF16)
        bias = jnp.concatenate([router_b_group[layer], router_b_expert[layer].reshape(-1)])
        return w_hi, w_lo, jnp.pad(bias, (0, LANES - bias.shape[0])).reshape(1, LANES)

    pos_p = jnp.arange(SEQ, dtype=jnp.int32)
    pos_s = PAST_LEN + jnp.arange(DEC_SEQ, dtype=jnp.int32)
    rope_p = _rope_tables(pos_p)
    rope_s1 = _rope_tables(pos_s)
    rope_s = tuple(jnp.tile(t, (TM // DEC_SEQ, 1)) for t in rope_s1)
    rope_all = tuple(jnp.concatenate([jnp.tile(p, (BATCH, 1)), jnp.tile(s, (DEC_BATCH, 1))], axis=0)
                     for p, s in zip(rope_p, rope_s1))

    tails, us = [], []
    kv_p = kv_s = None
    outs_kv = None
    for layer in range(DEPTH):
        post_w = (row(ln_mix_g[layer]), row(ln_mix_b[layer])) + router_weights(layer)
        if layer < N_A_LAYERS:
            st = jnp.pad(state_conv[layer], ((0, 0), (STATE_PAD - (CONV_WIDTH - 1), 0), (0, 0)))
            wdw = jnp.pad(conv_w_dw[layer], ((0, STATE_PAD - CONV_WIDTH), (0, 0)))
            wdw = wdw.reshape(STATE_PAD, N_LANE_CHUNKS, LANES).transpose(1, 0, 2)
            w = (conv_w_in[layer].astype(BF16), wdw, row(conv_ln_g[layer]), row(conv_ln_b[layer]),
                 conv_w_out[layer].astype(BF16)) + post_w
            if layer == 0:
                x_in = (x_prompt.reshape(T_PROMPT, D_MODEL), x_sample.reshape(T_SAMPLE, D_MODEL), 0)
            else:
                x_in = (x, x, PROMPT_TILES)
            x1, info, er, cnt, tail, u_s = _conv_mixer(*x_in, st, w)
            tails.append(tail[:, STATE_PAD - (CONV_WIDTH - 1):, :])
            us.append(u_s.reshape(DEC_BATCH, DEC_SEQ, D_MODEL)[:, DEC_SEQ - (CONV_WIDTH - 1):, :])
        else:
            j = layer - N_A_LAYERS
            x1, info, er, cnt = _attn_mixer(x, attn_sinks[j], attn_w_q[j].astype(BF16), attn_w_o[j].astype(BF16),
                                        post_w, rope_p, rope_s, kv_p, kv_s)
        last_a = layer == N_A_LAYERS - 1
        outs = _moe_layer(x1, info, er, cnt, wg, wu, wd, layer, row(ln_ffn_g[layer]), row(ln_ffn_b[layer]),
                          split=layer == DEPTH - 1, kv_w=(w_kv.astype(BF16), rope_all) if last_a else None)
        x = outs if layer == DEPTH - 1 else outs[0]
        if last_a:
            k, v = outs[1:]
            k_p = k[:T_PROMPT].reshape(BATCH, SEQ, KV_WIDTH)
            v_p = v[:T_PROMPT].reshape(BATCH, SEQ, KV_WIDTH)
            k_s = jnp.concatenate([cache_k.reshape(DEC_BATCH, WINDOW, KV_WIDTH),
                                   k[T_PROMPT:].reshape(DEC_BATCH, DEC_SEQ, KV_WIDTH)], axis=1)
            v_s = jnp.concatenate([cache_v.reshape(DEC_BATCH, WINDOW, KV_WIDTH),
                                   v[T_PROMPT:].reshape(DEC_BATCH, DEC_SEQ, KV_WIDTH)], axis=1)
            cache_shape = (-1, WINDOW, N_KV_HEADS, HEAD_DIM)
            outs_kv = (k_p[:, -WINDOW:].reshape(cache_shape), v_p[:, -WINDOW:].reshape(cache_shape),
                       k_s[:, -WINDOW:].reshape(cache_shape), v_s[:, -WINDOW:].reshape(cache_shape))
            front = ((0, 0), (WINDOW, 0), (0, 0))
            back = ((0, 0), (0, KEY_WIN - WINDOW - DEC_SEQ), (0, 0))
            kv_p = (_dup_heads(jnp.pad(k_p, front)), _dup_heads(jnp.pad(v_p, front)))
            kv_s = (_dup_heads(jnp.pad(k_s, back)), _dup_heads(jnp.pad(v_s, back)))
    y_prompt = x[0].reshape(BATCH, SEQ, D_MODEL)
    y_sample = x[1].reshape(DEC_BATCH, DEC_SEQ, D_MODEL)
    return (y_prompt, y_sample, jnp.stack(tails, axis=0), jnp.stack(us, axis=0)) + outs_kv
```

```python
import functools

import jax
import jax.numpy as jnp
from jax import lax
from jax.experimental import pallas as pl
from jax.experimental.pallas import tpu as pltpu

D_MODEL = 1024
BATCH = 2
SEQ = 8192
DEPTH = 4
DEC_BATCH = 32
DEC_SEQ = 32
PAST_LEN = 2048
CHUNK = 64
N_A_LAYERS = DEPTH // 2
CONV_WIDTH = 31
N_HEADS = 16
N_KV_HEADS = 2
HEAD_DIM = 64
KV_WIDTH = N_KV_HEADS * HEAD_DIM
WINDOW = 128
ROT_DIM = HEAD_DIM // 4
ROPE_THETA = 500000.0
N_GROUPS = 4
EXPERTS_PER_GROUP = 8
N_EXPERTS = N_GROUPS * EXPERTS_PER_GROUP
EXPERT_FF = D_MODEL // 2
DEEPNORM_ALPHA = (2 * DEPTH) ** 0.25
LN_EPS = 1e-5
ATTN_SCALE = HEAD_DIM ** -0.5
NEG_INF = -1e30

LANES = 128
N_LANE_CHUNKS = D_MODEL // LANES
T_PROMPT = BATCH * SEQ
T_SAMPLE = DEC_BATCH * DEC_SEQ
T_ALL = T_PROMPT + T_SAMPLE
TM = 256
STATE_PAD = 32
KEY_WIN = 256
UQ_PROMPT = 2 * CHUNK
MOE_TM = 512
MOE_ROWS = 2 * T_ALL + N_EXPERTS * MOE_TM
MOE_TILES = MOE_ROWS // MOE_TM
VMEM_LIMIT = 56 * 1024 * 1024

F32 = jnp.float32
BF16 = jnp.bfloat16


def _layer_norm(x, g, b):
    mu = jnp.mean(x, axis=-1, keepdims=True)
    xc = x - mu
    var = jnp.mean(xc * xc, axis=-1, keepdims=True)
    return xc * lax.rsqrt(var + LN_EPS) * g + b


def _router(x1, rwh_ref, rwl_ref, rb_ref, cnt_ref):
    x_hi = x1.astype(BF16)
    x_lo = (x1 - x_hi.astype(F32)).astype(BF16)
    w_hi = rwh_ref[...]
    logits = (jnp.dot(x_hi, w_hi, preferred_element_type=F32)
              + jnp.dot(x_hi, rwl_ref[...], preferred_element_type=F32)
              + jnp.dot(x_lo, w_hi, preferred_element_type=F32)) + rb_ref[...]
    lane_i = lax.broadcasted_iota(jnp.int32, logits.shape, 1)
    lane = lane_i.astype(F32)
    neg = jnp.float32(-3.0e38)
    none = jnp.float32(LANES)
    gl = jnp.where(lane_i < N_GROUPS, logits, neg)
    gmax = jnp.max(gl, axis=-1, keepdims=True)
    g_idx = jnp.min(jnp.where(gl == gmax, lane, none), axis=-1, keepdims=True)
    gsum = jnp.sum(jnp.where(lane_i < N_GROUPS, jnp.exp(gl - gmax), 0.0), axis=-1, keepdims=True)
    p_grp = 1.0 / gsum
    lo = N_GROUPS + EXPERTS_PER_GROUP * g_idx
    el = jnp.where(lane >= lo, jnp.where(lane < lo + EXPERTS_PER_GROUP, logits, neg), neg)
    l1 = jnp.max(el, axis=-1, keepdims=True)
    i1 = jnp.min(jnp.where(el == l1, lane, none), axis=-1, keepdims=True)
    el2 = jnp.where(lane == i1, neg, el)
    l2 = jnp.max(el2, axis=-1, keepdims=True)
    i2 = jnp.min(jnp.where(el2 == l2, lane, none), axis=-1, keepdims=True)
    t = jnp.exp(l2 - l1)
    inv = p_grp / (1.0 + t)
    e1 = i1 - N_GROUPS
    e2 = i2 - N_GROUPS
    rows = logits.shape[0]
    onehot = jnp.where(lane == e1, 1.0, jnp.where(lane == e2, 1.0, 0.0))
    tri = (lax.broadcasted_iota(jnp.int32, (rows, rows), 0) > lax.broadcasted_iota(jnp.int32, (rows, rows), 1))
    before = jnp.dot(jnp.where(tri, 1.0, 0.0).astype(BF16), onehot.astype(BF16),
                     preferred_element_type=F32) + cnt_ref[...]
    r1 = jnp.sum(jnp.where(lane == e1, before, 0.0), axis=-1, keepdims=True)
    r2 = jnp.sum(jnp.where(lane == e2, before, 0.0), axis=-1, keepdims=True)
    cnt_ref[...] = cnt_ref[...] + jnp.sum(onehot, axis=0, keepdims=True)
    out = jnp.where(lane_i == 4, r1, jnp.where(lane_i == 5, r2, 0.0))
    out = jnp.where(lane_i == 2, inv, jnp.where(lane_i == 3, inv * t, out))
    return jnp.where(lane_i == 0, e1, jnp.where(lane_i == 1, e2, out))


ER_ROWS = 8


def _post_mixer(x, mix, lg_ref, lb_ref, rwh_ref, rwl_ref, rb_ref, x1_ref, info_ref, er_ref, cnt_out_ref, cnt_ref):
    x1 = _layer_norm(DEEPNORM_ALPHA * x + mix, lg_ref[...], lb_ref[...])
    x1_ref[...] = x1
    info = _router(x1, rwh_ref, rwl_ref, rb_ref, cnt_ref)
    info_ref[...] = info
    er_ref[0] = jnp.transpose(info)[0:ER_ROWS, :].astype(jnp.int32)
    cnt_out_ref[...] = cnt_ref[...]


def _init_counts(cnt_ref, first_step, cnt_in_ref):
    @pl.when(first_step)
    def _():
        cnt_ref[...] = jnp.zeros_like(cnt_ref) if cnt_in_ref is None else cnt_in_ref[...]


def _conv_mixer_body(x_ref, st_ref, win_ref, wdw_ref, cg_ref, cb_ref, wout_ref, post_w, cnt_in_ref,
                     x1_ref, info_ref, er_ref, cnt_out_ref, u_ref, cnt_ref, ubuf, cbuf, *, nseg, seg):
    carry = st_ref is None
    first_step = (pl.program_id(0) == 0) & (pl.program_id(1) == 0) if carry else pl.program_id(0) == 0
    _init_counts(cnt_ref, first_step, cnt_in_ref)
    x = x_ref[...]
    h = jnp.dot(x.astype(BF16), win_ref[...], preferred_element_type=F32)
    u = h[:, :D_MODEL] * jax.nn.sigmoid(h[:, D_MODEL:])
    if carry:
        @pl.when(pl.program_id(1) == 0)
        def _():
            ubuf[:, 0, 0:STATE_PAD, :] = jnp.zeros((N_LANE_CHUNKS, STATE_PAD, LANES), F32)
        u_ref[0] = u[seg - STATE_PAD:, :]
    else:
        u_ref[...] = u
    for lc in range(N_LANE_CHUNKS):
        for s in range(nseg):
            ubuf[lc, s, STATE_PAD:STATE_PAD + seg, :] = u[s * seg:(s + 1) * seg, lc * LANES:(lc + 1) * LANES]
            if not carry:
                ubuf[lc, s, 0:STATE_PAD, :] = st_ref[s, :, lc * LANES:(lc + 1) * LANES]

    def conv_lane_chunk(lc, c):
        w = wdw_ref[lc]
        for s in range(nseg):
            acc = jnp.zeros((seg, LANES), F32)
            for k in range(CONV_WIDTH):
                off = k + STATE_PAD - (CONV_WIDTH - 1)
                acc = acc + w[k:k + 1, :] * ubuf[lc, s, off:off + seg, :]
            cbuf[lc, s * seg:(s + 1) * seg, :] = acc
        return c

    lax.fori_loop(0, N_LANE_CHUNKS, conv_lane_chunk, 0)
    if carry:
        for lc in range(N_LANE_CHUNKS):
            ubuf[lc, 0, 0:STATE_PAD, :] = u[seg - STATE_PAD:, lc * LANES:(lc + 1) * LANES]
    dw = jnp.concatenate([cbuf[lc] for lc in range(N_LANE_CHUNKS)], axis=1)
    sw = _layer_norm(dw, cg_ref[...], cb_ref[...])
    sw = sw * jax.nn.sigmoid(sw)
    mix = jnp.dot(sw.astype(BF16), wout_ref[...], preferred_element_type=F32)
    _post_mixer(x, mix, *post_w, x1_ref, info_ref, er_ref, cnt_out_ref, cnt_ref)


N_POST_W = 5


def _conv_prompt_kernel(x_ref, win_ref, wdw_ref, cg_ref, cb_ref, wout_ref, *rest):
    post_w, rest = rest[:N_POST_W], rest[N_POST_W:]
    _conv_mixer_body(x_ref, None, win_ref, wdw_ref, cg_ref, cb_ref, wout_ref, post_w, None, *rest,
                     nseg=1, seg=TM)


def _conv_sample_kernel(x_ref, st_ref, win_ref, wdw_ref, cg_ref, cb_ref, wout_ref, *rest):
    post_w, cnt_in_ref, rest = rest[:N_POST_W], rest[N_POST_W], rest[N_POST_W + 1 + N_ALIASED:]
    _conv_mixer_body(x_ref, st_ref, win_ref, wdw_ref, cg_ref, cb_ref, wout_ref, post_w, cnt_in_ref, *rest,
                     nseg=TM // DEC_SEQ, seg=DEC_SEQ)


def _full(shape):
    return pl.BlockSpec(shape, lambda *_: (0,) * len(shape))


ANY_SPEC = pl.BlockSpec(memory_space=pl.ANY)
CNT_SHAPE = jax.ShapeDtypeStruct((1, LANES), F32)
CNT_SCRATCH = pltpu.VMEM((1, LANES), F32)
PROMPT_TILES = T_PROMPT // TM
N_ALIASED = 3
TOKEN_OUT_SHAPES = [jax.ShapeDtypeStruct((T_ALL, D_MODEL), F32), jax.ShapeDtypeStruct((T_ALL, LANES), F32),
                    jax.ShapeDtypeStruct((T_ALL // TM, ER_ROWS, TM), jnp.int32)]


def _token_out_specs(tile):
    return [pl.BlockSpec((TM, D_MODEL), lambda *g: (tile(*g), 0)), pl.BlockSpec((TM, LANES), lambda *g: (tile(*g), 0)),
            pl.BlockSpec((1, ER_ROWS, TM), lambda *g: (tile(*g), 0, 0))]


def _conv_mixer(x_p, x_s, s_first, state_pad, w):
    w_specs = [_full(a.shape) for a in w]
    tiles_per_seq = SEQ // TM
    ptile = lambda b, t: b * tiles_per_seq + t
    x1, info, er, cnt, tail = pl.pallas_call(
        _conv_prompt_kernel,
        grid=(BATCH, tiles_per_seq),
        in_specs=[pl.BlockSpec((TM, D_MODEL), lambda b, t: (ptile(b, t), 0))] + w_specs,
        out_specs=_token_out_specs(ptile) + [_full(CNT_SHAPE.shape),
                                             pl.BlockSpec((1, STATE_PAD, D_MODEL), lambda b, t: (b, 0, 0))],
        out_shape=TOKEN_OUT_SHAPES + [CNT_SHAPE, jax.ShapeDtypeStruct((BATCH, STATE_PAD, D_MODEL), F32)],
        scratch_shapes=[CNT_SCRATCH, pltpu.VMEM((N_LANE_CHUNKS, 1, STATE_PAD + TM, LANES), F32),
                        pltpu.VMEM((N_LANE_CHUNKS, TM, LANES), F32)],
        compiler_params=pltpu.CompilerParams(dimension_semantics=("arbitrary", "arbitrary"),
                                             vmem_limit_bytes=VMEM_LIMIT),
        name="conv_mixer_prompt",
    )(x_p, *w)
    nb = TM // DEC_SEQ
    n_in = 2 + len(w) + 1
    x1, info, er, cnt, u_s = pl.pallas_call(
        _conv_sample_kernel,
        grid=(T_SAMPLE // TM,),
        in_specs=[pl.BlockSpec((TM, D_MODEL), lambda i: (s_first + i, 0)),
                  pl.BlockSpec((nb, STATE_PAD, D_MODEL), lambda i: (i, 0, 0))]
                 + w_specs + [_full(CNT_SHAPE.shape)] + [ANY_SPEC] * N_ALIASED,
        out_specs=_token_out_specs(lambda i: PROMPT_TILES + i)
                  + [_full(CNT_SHAPE.shape), pl.BlockSpec((TM, D_MODEL), lambda i: (i, 0))],
        out_shape=TOKEN_OUT_SHAPES + [CNT_SHAPE, jax.ShapeDtypeStruct((T_SAMPLE, D_MODEL), F32)],
        scratch_shapes=[CNT_SCRATCH, pltpu.VMEM((N_LANE_CHUNKS, nb, STATE_PAD + DEC_SEQ, LANES), F32),
                        pltpu.VMEM((N_LANE_CHUNKS, TM, LANES), F32)],
        input_output_aliases={n_in + k: k for k in range(N_ALIASED)},
        compiler_params=pltpu.CompilerParams(dimension_semantics=("arbitrary",),
                                             vmem_limit_bytes=VMEM_LIMIT),
        name="conv_mixer_sample",
    )(x_s, state_pad, *w, cnt, x1, info, er)
    return x1, info, er, cnt, tail, u_s


def _rope_lanes(v, c, sa, sb):
    width = v.shape[-1]
    half = ROT_DIM // 2
    return v * c + pltpu.roll(v, width - half, 1) * sa + pltpu.roll(v, half, 1) * sb


def _attn_mixer_body(sink_ref, x_ref, wq_ref, c_ref, sa_ref, sb_ref, kd_ref, vd_ref, wo_ref, post_w, cnt_in_ref,
                     x1_ref, info_ref, er_ref, cnt_out_ref, cnt_ref, obuf, bias_ref, *, nunit, uq, prompt):
    first_step = (pl.program_id(0) == 0) & (pl.program_id(1) == 0) if prompt else pl.program_id(0) == 0
    _init_counts(cnt_ref, first_step, cnt_in_ref)

    @pl.when(first_step)
    def _():
        shape = bias_ref.shape[1:]
        col = lax.broadcasted_iota(jnp.int32, shape, 1) & (KEY_WIN - 1)
        if prompt:
            chunk_shift = CHUNK.bit_length() - 1
            qchunk = (lax.broadcasted_iota(jnp.int32, shape, 0) & (uq - 1)) >> chunk_shift
            kchunk = col >> chunk_shift
            band = jnp.where(kchunk >= qchunk, jnp.where(kchunk <= qchunk + WINDOW // CHUNK, 0.0, NEG_INF), NEG_INF)
            bias_ref[0] = band
            bias_ref[1] = jnp.where(col >= WINDOW, band, NEG_INF)
        else:
            bias_ref[0] = jnp.where(col < WINDOW + DEC_SEQ, 0.0, NEG_INF)

    x = x_ref[...]
    q = jnp.dot(x.astype(BF16), wq_ref[...], preferred_element_type=F32)
    rep = D_MODEL // LANES
    q = _rope_lanes(q, jnp.tile(c_ref[...], (1, rep)), jnp.tile(sa_ref[...], (1, rep)),
                    jnp.tile(sb_ref[...], (1, rep)))
    qb = (q * ATTN_SCALE).astype(BF16)
    pairs = N_HEADS // N_KV_HEADS // 2
    lane_k = lax.broadcasted_iota(jnp.int32, (KEY_WIN, LANES), 1)
    lane_o = lax.broadcasted_iota(jnp.int32, (uq, LANES), 1)
    for un in range(nunit):
        r0 = un * uq
        if prompt:
            kstart = pl.multiple_of(pl.program_id(1) * (nunit * uq) + r0, LANES)
            bias = bias_ref[jnp.where(kstart == 0, 1, 0)]
        else:
            bias = bias_ref[0]
        for kvh in range(N_KV_HEADS):
            if prompt:
                kw = kd_ref[0, kvh, pl.ds(kstart, KEY_WIN), :]
                vw = vd_ref[0, kvh, pl.ds(kstart, KEY_WIN), :]
            else:
                kw = kd_ref[un, kvh]
                vw = vd_ref[un, kvh]
            zero = jnp.zeros_like(kw)
            kbd = jnp.concatenate([jnp.where(lane_k < HEAD_DIM, kw, zero),
                                   jnp.where(lane_k >= HEAD_DIM, kw, zero)], axis=0)
            vbd = jnp.concatenate([jnp.where(lane_k < HEAD_DIM, vw, zero),
                                   jnp.where(lane_k >= HEAD_DIM, vw, zero)], axis=0)
            q4 = jnp.concatenate([qb[r0:r0 + uq, (kvh * pairs + p) * LANES:(kvh * pairs + p + 1) * LANES]
                                  for p in range(pairs)], axis=0)
            s = lax.dot_general(q4, kbd, (((1,), (1,)), ((), ())), preferred_element_type=F32)
            s = s + bias
            p_parts, inv_parts = [], []
            for p in range(pairs):
                halves, invs = [], []
                for hf in range(2):
                    sk = sink_ref[(kvh * pairs + p) * 2 + hf]
                    sh = s[p * uq:(p + 1) * uq, hf * KEY_WIN:(hf + 1) * KEY_WIN]
                    m = jnp.maximum(jnp.max(sh, axis=-1, keepdims=True), sk)
                    pe = jnp.exp(sh - m)
                    den = jnp.sum(pe, axis=-1, keepdims=True) + jnp.exp(sk - m)
                    halves.append(pe.astype(BF16))
                    invs.append(1.0 / den)
                p_parts.append(jnp.concatenate(halves, axis=1))
                inv_parts.append(invs)
            pm = jnp.concatenate(p_parts, axis=0)
            o4 = jnp.dot(pm, vbd, preferred_element_type=F32)
            for p in range(pairs):
                inv = jnp.where(lane_o < HEAD_DIM, inv_parts[p][0], inv_parts[p][1])
                hp = kvh * pairs + p
                obuf[r0:r0 + uq, hp * LANES:(hp + 1) * LANES] = (o4[p * uq:(p + 1) * uq, :] * inv).astype(BF16)
    mix = jnp.dot(obuf[...], wo_ref[...], preferred_element_type=F32)
    _post_mixer(x, mix, *post_w, x1_ref, info_ref, er_ref, cnt_out_ref, cnt_ref)


N_ATTN_IN = 9


def _attn_prompt_kernel(*refs):
    ins, post_w, rest = refs[:N_ATTN_IN], refs[N_ATTN_IN:N_ATTN_IN + N_POST_W], refs[N_ATTN_IN + N_POST_W:]
    _attn_mixer_body(*ins, post_w, None, *rest, nunit=TM // UQ_PROMPT, uq=UQ_PROMPT, prompt=True)


def _attn_sample_kernel(*refs):
    ins, post_w = refs[:N_ATTN_IN], refs[N_ATTN_IN:N_ATTN_IN + N_POST_W]
    cnt_in_ref, rest = refs[N_ATTN_IN + N_POST_W], refs[N_ATTN_IN + N_POST_W + 1 + N_ALIASED:]
    _attn_mixer_body(*ins, post_w, cnt_in_ref, *rest, nunit=TM // DEC_SEQ, uq=DEC_SEQ, prompt=False)


def _attn_mixer(x, sinks, wq, wo, post_w, rope_p, rope_s, kv_p, kv_s):
    tiles_per_seq = SEQ // TM
    smem = pl.BlockSpec(memory_space=pltpu.SMEM)
    tail_specs = [_full(wo.shape)] + [_full(a.shape) for a in post_w]
    kd, vd = kv_p
    ptile = lambda b, t: b * tiles_per_seq + t
    pair_rows = N_HEADS // N_KV_HEADS // 2
    x1, info, er, cnt = pl.pallas_call(
        _attn_prompt_kernel,
        grid=(BATCH, tiles_per_seq),
        in_specs=[smem, pl.BlockSpec((TM, D_MODEL), lambda b, t: (ptile(b, t), 0)), _full(wq.shape)]
                 + [pl.BlockSpec((TM, LANES), lambda b, t: (t, 0))] * 3
                 + [pl.BlockSpec((1,) + kd.shape[1:], lambda b, t: (b, 0, 0, 0))] * 2 + tail_specs,
        out_specs=_token_out_specs(ptile) + [_full(CNT_SHAPE.shape)],
        out_shape=TOKEN_OUT_SHAPES + [CNT_SHAPE],
        scratch_shapes=[CNT_SCRATCH, pltpu.VMEM((TM, D_MODEL), BF16),
                        pltpu.VMEM((2, pair_rows * UQ_PROMPT, 2 * KEY_WIN), F32)],
        compiler_params=pltpu.CompilerParams(dimension_semantics=("arbitrary", "arbitrary"),
                                             vmem_limit_bytes=VMEM_LIMIT),
        name="attn_mixer_prompt",
    )(sinks, x, wq, *rope_p, kd, vd, wo, *post_w)
    nb = TM // DEC_SEQ
    kd, vd = kv_s
    n_in = N_ATTN_IN + N_POST_W + 1
    x1, info, er, cnt = pl.pallas_call(
        _attn_sample_kernel,
        grid=(T_SAMPLE // TM,),
        in_specs=[smem, pl.BlockSpec((TM, D_MODEL), lambda i: (PROMPT_TILES + i, 0)), _full(wq.shape)]
                 + [_full((TM, LANES))] * 3
                 + [pl.BlockSpec((nb,) + kd.shape[1:], lambda i: (i, 0, 0, 0))] * 2 + tail_specs
                 + [_full(CNT_SHAPE.shape)] + [ANY_SPEC] * N_ALIASED,
        out_specs=_token_out_specs(lambda i: PROMPT_TILES + i) + [_full(CNT_SHAPE.shape)],
        out_shape=TOKEN_OUT_SHAPES + [CNT_SHAPE],
        scratch_shapes=[CNT_SCRATCH, pltpu.VMEM((TM, D_MODEL), BF16),
                        pltpu.VMEM((1, pair_rows * DEC_SEQ, 2 * KEY_WIN), F32)],
        input_output_aliases={n_in + k: k for k in range(N_ALIASED)},
        compiler_params=pltpu.CompilerParams(dimension_semantics=("arbitrary",),
                                             vmem_limit_bytes=VMEM_LIMIT),
        name="attn_mixer_sample",
    )(sinks, x, wq, *rope_s, kd, vd, wo, *post_w, cnt, x1, info, er)
    return x1, info, er, cnt


N_TILES = T_ALL // TM


def _row_copy_wait(src_rows, dst_rows, sem):
    pltpu.make_async_copy(src_rows, dst_rows, sem).wait()


PAIR = 4
N_PAIRS = N_TILES // PAIR
assert N_TILES % PAIR == 0 and PROMPT_TILES % PAIR == 0
GATHER_LEAD = 2
assert 0 < GATHER_LEAD < PAIR
POS_BLOCK = (PAIR, 2, TM)


def _dispatch_kernel(zrow_ref, zflag_ref, pos_ref, x1_ref, xs_hbm, zbuf, sem, zsem):
    i = pl.program_id(0)

    @pl.when(i == 0)
    def _():
        zbuf[...] = jnp.zeros_like(zbuf)

        def zero_tile(e):
            return pltpu.make_async_copy(zbuf, xs_hbm.at[pl.ds(pl.multiple_of(zrow_ref[e], MOE_TM), MOE_TM)], zsem)

        for e in range(N_EXPERTS):
            @pl.when(zflag_ref[e] > 0)
            def _():
                zero_tile(e).start()
        for e in range(N_EXPERTS):
            @pl.when(zflag_ref[e] > 0)
            def _():
                zero_tile(e).wait()

    for h in range(PAIR):
        for j in range(TM):
            for slot in range(2):
                pltpu.make_async_copy(x1_ref.at[pl.ds(h * TM + j, 1)],
                                      xs_hbm.at[pl.ds(pos_ref[h, slot, j], 1)], sem).start(priority=slot)
    for slot in range(2):
        _row_copy_wait(x1_ref, xs_hbm.at[pl.ds(0, PAIR * TM)], sem)


def _dispatch(x1, pos, zrow, zflag):
    return pl.pallas_call(
        _dispatch_kernel,
        grid_spec=pltpu.PrefetchScalarGridSpec(
            num_scalar_prefetch=2,
            grid=(N_PAIRS,),
            in_specs=[pl.BlockSpec(POS_BLOCK, lambda i, *_: (i, 0, 0), memory_space=pltpu.SMEM),
                      pl.BlockSpec((PAIR * TM, D_MODEL), lambda i, *_: (i, 0))],
            out_specs=ANY_SPEC,
            scratch_shapes=[pltpu.VMEM((MOE_TM, D_MODEL), F32), pltpu.SemaphoreType.DMA,
                            pltpu.SemaphoreType.DMA],
        ),
        out_shape=jax.ShapeDtypeStruct((MOE_ROWS, D_MODEL), F32),
        compiler_params=pltpu.CompilerParams(dimension_semantics=("arbitrary",), vmem_limit_bytes=VMEM_LIMIT),
        name="moe_dispatch",
    )(zrow, zflag, pos, x1)


def _moe_kernel(te_ref, slot_ref, next_ref, nv_ref, xs_ref, wg_hbm, wu_hbm, wd_hbm, ys_ref,
                wg_st, wu_st, wd_st, wgu_bf, wd_bf, sem, *, base):
    i = pl.program_id(0)

    def fetch(e, slot):
        return (pltpu.make_async_copy(wg_hbm.at[base + e], wg_st.at[slot], sem.at[slot, 0]),
                pltpu.make_async_copy(wu_hbm.at[base + e], wu_st.at[slot], sem.at[slot, 1]),
                pltpu.make_async_copy(wd_hbm.at[base + e], wd_st.at[slot], sem.at[slot, 2]))

    @pl.when(i == 0)
    def _():
        for c in fetch(te_ref[0], slot_ref[0]):
            c.start()

    @pl.when(i < nv_ref[0])
    def _():
        e = te_ref[i]
        slot = slot_ref[i]

        @pl.when((i == 0) | (e != te_ref[jnp.maximum(i - 1, 0)]))
        def _():
            for c in fetch(e, slot):
                c.wait()
            wgu_bf[:, :EXPERT_FF] = wg_st[slot].astype(BF16)
            wgu_bf[:, EXPERT_FF:] = wu_st[slot].astype(BF16)
            wd_bf[...] = wd_st[slot].astype(BF16)

            @pl.when(next_ref[i] >= 0)
            def _():
                for c in fetch(next_ref[i], 1 - slot):
                    c.start(priority=1)

        hgu = jnp.dot(xs_ref[...].astype(BF16), wgu_bf[...], preferred_element_type=F32)
        hg = hgu[:, :EXPERT_FF]
        h = hg * jax.nn.sigmoid(hg) * hgu[:, EXPERT_FF:]
        ys_ref[...] = jnp.dot(h.astype(BF16), wd_bf[...], preferred_element_type=F32)


def _moe_experts(xs, tile_expert, tile_slot, tile_next, n_valid, wg, wu, wd, layer):
    row = lambda i, te, sl, nx, nv: (jnp.minimum(i, nv[0] - 1), 0)
    return pl.pallas_call(
        functools.partial(_moe_kernel, base=layer * N_EXPERTS),
        grid_spec=pltpu.PrefetchScalarGridSpec(
            num_scalar_prefetch=4,
            grid=(MOE_TILES,),
            in_specs=[pl.BlockSpec((MOE_TM, D_MODEL), row), ANY_SPEC, ANY_SPEC, ANY_SPEC],
            out_specs=pl.BlockSpec((MOE_TM, D_MODEL), row),
            scratch_shapes=[pltpu.VMEM((2, D_MODEL, EXPERT_FF), F32), pltpu.VMEM((2, D_MODEL, EXPERT_FF), F32),
                            pltpu.VMEM((2, EXPERT_FF, D_MODEL), F32),
                            pltpu.VMEM((D_MODEL, 2 * EXPERT_FF), BF16), pltpu.VMEM((EXPERT_FF, D_MODEL), BF16),
                            pltpu.SemaphoreType.DMA((2, 3))],
        ),
        out_shape=jax.ShapeDtypeStruct((MOE_ROWS, D_MODEL), F32),
        compiler_params=pltpu.CompilerParams(dimension_semantics=("arbitrary",), vmem_limit_bytes=VMEM_LIMIT),
        name="moe_experts",
    )(tile_expert, tile_slot, tile_next, n_valid, xs, wg, wu, wd)


def _ffn_ln_kernel(pos_ref, pos_next_ref, x1_ref, info_ref, g_ref, b_ref, *rest, split, with_kv):
    kv_in, rest = (rest[:4], rest[4:]) if with_kv else ((), rest)
    ys_hbm, out_refs, bufs, sem = rest[0], rest[1:-(PAIR + 1)], rest[-(PAIR + 1):-1], rest[-1]
    i = pl.program_id(0)

    def gather(p_ref, h, buf):
        for j in range(TM):
            for slot in range(2):
                pltpu.make_async_copy(ys_hbm.at[pl.ds(p_ref[h, slot, j], 1)],
                                      bufs[buf].at[slot, pl.ds(j, 1)], sem.at[buf]).start(priority=slot)

    def drain(buf):
        for slot in range(2):
            _row_copy_wait(ys_hbm.at[pl.ds(0, TM)], bufs[buf].at[slot], sem.at[buf])

    @pl.when(i == 0)
    def _():
        for h in range(GATHER_LEAD):
            gather(pos_ref, h, h)

    for h in range(PAIR):
        drain(h)
        ahead = h + GATHER_LEAD
        if ahead < PAIR:
            gather(pos_ref, ahead, ahead)
        else:
            gather(pos_next_ref, ahead - PAIR, ahead - PAIR)
        rows = slice(h * TM, (h + 1) * TM)
        info = info_ref[rows, :]
        f = info[:, 2:3] * bufs[h][0] + info[:, 3:4] * bufs[h][1]
        x2 = _layer_norm(DEEPNORM_ALPHA * x1_ref[rows, :] + f, g_ref[...], b_ref[...])
        if with_kv:
            wkv_ref, c_ref, sa_ref, sb_ref = kv_in
            k_ref, v_ref = out_refs[-2:]
            kv = jnp.dot(x2.astype(BF16), wkv_ref[...], preferred_element_type=F32)
            k_ref[rows, :] = _rope_lanes(kv[:, :KV_WIDTH], c_ref[rows, :], sa_ref[rows, :], sb_ref[rows, :])
            v_ref[rows, :] = kv[:, KV_WIDTH:]
        if split:
            @pl.when(i < PROMPT_TILES // PAIR)
            def _():
                out_refs[0][rows, :] = x2

            @pl.when(i >= PROMPT_TILES // PAIR)
            def _():
                out_refs[1][rows, :] = x2
        else:
            out_refs[0][rows, :] = x2

    @pl.when(i == N_PAIRS - 1)
    def _():
        for h in range(GATHER_LEAD):
            drain(h)


def _ffn_ln(x1, ys, pos, info, g, b, split, kv_w=None):
    tok = lambda i: (i, 0)
    rows = PAIR * TM
    prompt_steps = PROMPT_TILES // PAIR
    pos_spec = lambda fn: pl.BlockSpec(POS_BLOCK, fn, memory_space=pltpu.SMEM)
    if split:
        out_specs = [pl.BlockSpec((rows, D_MODEL), lambda i: (jnp.minimum(i, prompt_steps - 1), 0)),
                     pl.BlockSpec((rows, D_MODEL), lambda i: (jnp.maximum(i - prompt_steps, 0), 0))]
        out_shape = [jax.ShapeDtypeStruct((T_PROMPT, D_MODEL), F32), jax.ShapeDtypeStruct((T_SAMPLE, D_MODEL), F32)]
    else:
        out_specs = [pl.BlockSpec((rows, D_MODEL), tok)]
        out_shape = [jax.ShapeDtypeStruct((T_ALL, D_MODEL), F32)]
    kv_args, kv_specs = (), []
    if kv_w is not None:
        wkv, rope_all = kv_w
        kv_args = (wkv,) + tuple(rope_all)
        kv_specs = [_full(wkv.shape)] + [pl.BlockSpec((rows, LANES), tok)] * 3
        out_specs = out_specs + [pl.BlockSpec((rows, KV_WIDTH), tok)] * 2
        out_shape = out_shape + [jax.ShapeDtypeStruct((T_ALL, KV_WIDTH), F32)] * 2
    return pl.pallas_call(
        functools.partial(_ffn_ln_kernel, split=split, with_kv=kv_w is not None),
        grid=(N_PAIRS,),
        in_specs=[pos_spec(lambda i: (i, 0, 0)), pos_spec(lambda i: (jnp.minimum(i + 1, N_PAIRS - 1), 0, 0)),
                  pl.BlockSpec((rows, D_MODEL), tok), pl.BlockSpec((rows, LANES), tok), _full(g.shape),
                  _full(b.shape)] + kv_specs + [ANY_SPEC],
        out_specs=out_specs,
        out_shape=out_shape,
        scratch_shapes=[pltpu.VMEM((2, TM, D_MODEL), F32)] * PAIR + [pltpu.SemaphoreType.DMA((PAIR,))],
        compiler_params=pltpu.CompilerParams(dimension_semantics=("arbitrary",), vmem_limit_bytes=VMEM_LIMIT),
        name="ffn_ln",
    )(pos, pos, x1, info, g, b, *kv_args, ys)


def _moe_layer(x1, info, er, cnt, wg, wu, wd, layer, g, b, split, kv_w):
    counts = cnt[0, :N_EXPERTS].astype(jnp.int32)
    tiles_per = (counts + MOE_TM - 1) // MOE_TM
    tile_end = jnp.cumsum(tiles_per)
    pad_start = (tile_end - tiles_per) * MOE_TM
    ids = jnp.arange(N_EXPERTS, dtype=jnp.int32)
    half = ER_ROWS // 2
    hit = er[None, :, 0:2, :] == ids[:, None, None, None]
    pos = jnp.sum(jnp.where(hit, pad_start[:, None, None, None], 0), axis=0) + er[:, half:half + 2, :]
    n_valid = tile_end[-1:]
    tile_ids = jnp.minimum(jnp.arange(MOE_TILES, dtype=jnp.int32), n_valid[0] - 1)
    tile_expert = jnp.sum(tile_ids[:, None] >= tile_end[None, :], axis=1, dtype=jnp.int32)
    nonempty = tiles_per > 0
    slot_of = (jnp.cumsum(nonempty.astype(jnp.int32)) - 1) % 2
    later = jnp.where(nonempty[None, :] & (ids[None, :] > ids[:, None]), ids[None, :], N_EXPERTS)
    next_of = jnp.min(later, axis=1)
    next_of = jnp.where(next_of < N_EXPERTS, next_of, -1)
    zrow = jnp.maximum(tile_end - 1, 0) * MOE_TM
    mine = tile_expert[:, None] == ids[None, :]
    tile_slot = jnp.sum(jnp.where(mine, slot_of[None, :], 0), axis=1)
    tile_next = jnp.sum(jnp.where(mine, next_of[None, :], 0), axis=1)
    xs = _dispatch(x1, pos, zrow, tiles_per)
    ys = _moe_experts(xs, tile_expert, tile_slot, tile_next, n_valid, wg, wu, wd, layer)
    return _ffn_ln(x1, ys, pos, info, g, b, split, kv_w)


def _rope_tables(pos):
    half = ROT_DIM // 2
    inv_freq = ROPE_THETA ** (-jnp.arange(0, ROT_DIM, 2, dtype=F32) / ROT_DIM)
    ang = pos.astype(F32)[:, None] * inv_freq[None, :]
    cos, sin = jnp.cos(ang), jnp.sin(ang)
    n = pos.shape[0]
    ones = jnp.ones((n, HEAD_DIM - ROT_DIM), F32)
    zeros_h = jnp.zeros((n, half), F32)
    zeros_r = jnp.zeros((n, HEAD_DIM - ROT_DIM), F32)
    c = jnp.concatenate([cos, cos, ones], axis=1)
    sa = jnp.concatenate([-sin, zeros_h, zeros_r], axis=1)
    sb = jnp.concatenate([zeros_h, sin, zeros_r], axis=1)
    rep = LANES // HEAD_DIM
    return tuple(jnp.tile(t, (1, rep)) for t in (c, sa, sb))


def _dup_heads(kv):
    b, l, _ = kv.shape
    h = kv.reshape(b, l, N_KV_HEADS, HEAD_DIM).transpose(0, 2, 1, 3)
    return jnp.concatenate([h, h], axis=-1).astype(BF16)


def kernel(x_prompt, x_sample, state_conv, cache_k, cache_v, ln_mix_g, ln_mix_b, ln_ffn_g, ln_ffn_b, conv_w_in, conv_w_dw, conv_ln_g, conv_ln_b, conv_w_out, w_kv, attn_w_q, attn_sinks, attn_w_o, router_w_group, router_b_group, router_w_expert, router_b_expert, expert_w_gate, expert_w_up, expert_w_down):
    x = None
    wg = expert_w_gate.reshape(DEPTH * N_EXPERTS, D_MODEL, EXPERT_FF)
    wu = expert_w_up.reshape(DEPTH * N_EXPERTS, D_MODEL, EXPERT_FF)
    wd = expert_w_down.reshape(DEPTH * N_EXPERTS, EXPERT_FF, D_MODEL)
    row = lambda v: v.reshape(1, -1)

    def router_weights(layer):
        w = jnp.concatenate([router_w_group[layer], router_w_expert[layer]], axis=1)
        w = jnp.pad(w, ((0, 0), (0, LANES - w.shape[1])))
        w_hi = w.astype(BF16)
        w_lo = (w - w_hi.astype(F32)).astype(BF16)
        bias = jnp.concatenate([router_b_group[layer], router_b_expert[layer].reshape(-1)])
        return w_hi, w_lo, jnp.pad(bias, (0, LANES - bias.shape[0])).reshape(1, LANES)

    pos_p = jnp.arange(SEQ, dtype=jnp.int32)
    pos_s = PAST_LEN + jnp.arange(DEC_SEQ, dtype=jnp.int32)
    rope_p = _rope_tables(pos_p)
    rope_s1 = _rope_tables(pos_s)
    rope_s = tuple(jnp.tile(t, (TM // DEC_SEQ, 1)) for t in rope_s1)
    rope_all = tuple(jnp.concatenate([jnp.tile(p, (BATCH, 1)), jnp.tile(s, (DEC_BATCH, 1))], axis=0)
                     for p, s in zip(rope_p, rope_s1))

    tails, us = [], []
    kv_p = kv_s = None
    outs_kv = None
    for layer in range(DEPTH):
        post_w = (row(ln_mix_g[layer]), row(ln_mix_b[layer])) + router_weights(layer)
        if layer < N_A_LAYERS:
            st = jnp.pad(state_conv[layer], ((0, 0), (STATE_PAD - (CONV_WIDTH - 1), 0), (0, 0)))
            wdw = jnp.pad(conv_w_dw[layer], ((0, STATE_PAD - CONV_WIDTH), (0, 0)))
            wdw = wdw.reshape(STATE_PAD, N_LANE_CHUNKS, LANES).transpose(1, 0, 2)
            w = (conv_w_in[layer].astype(BF16), wdw, row(conv_ln_g[layer]), row(conv_ln_b[layer]),
                 conv_w_out[layer].astype(BF16)) + post_w
            if layer == 0:
                x_in = (x_prompt.reshape(T_PROMPT, D_MODEL), x_sample.reshape(T_SAMPLE, D_MODEL), 0)
            else:
                x_in = (x, x, PROMPT_TILES)
            x1, info, er, cnt, tail, u_s = _conv_mixer(*x_in, st, w)
            tails.append(tail[:, STATE_PAD - (CONV_WIDTH - 1):, :])
            us.append(u_s.reshape(DEC_BATCH, DEC_SEQ, D_MODEL)[:, DEC_SEQ - (CONV_WIDTH - 1):, :])
        else:
            j = layer - N_A_LAYERS
            x1, info, er, cnt = _attn_mixer(x, attn_sinks[j], attn_w_q[j].astype(BF16), attn_w_o[j].astype(BF16),
                                        post_w, rope_p, rope_s, kv_p, kv_s)
        last_a = layer == N_A_LAYERS - 1
        outs = _moe_layer(x1, info, er, cnt, wg, wu, wd, layer, row(ln_ffn_g[layer]), row(ln_ffn_b[layer]),
                          split=layer == DEPTH - 1, kv_w=(w_kv.astype(BF16), rope_all) if last_a else None)
        x = outs if layer == DEPTH - 1 else outs[0]
        if last_a:
            k, v = outs[1:]
            k_p = k[:T_PROMPT].reshape(BATCH, SEQ, KV_WIDTH)
            v_p = v[:T_PROMPT].reshape(BATCH, SEQ, KV_WIDTH)
            k_s = jnp.concatenate([cache_k.reshape(DEC_BATCH, WINDOW, KV_WIDTH),
                                   k[T_PROMPT:].reshape(DEC_BATCH, DEC_SEQ, KV_WIDTH)], axis=1)
            v_s = jnp.concatenate([cache_v.reshape(DEC_BATCH, WINDOW, KV_WIDTH),
                                   v[T_PROMPT:].reshape(DEC_BATCH, DEC_SEQ, KV_WIDTH)], axis=1)
            cache_shape = (-1, WINDOW, N_KV_HEADS, HEAD_DIM)
            outs_kv = (k_p[:, -WINDOW:].reshape(cache_shape), v_p[:, -WINDOW:].reshape(cache_shape),
                       k_s[:, -WINDOW:].reshape(cache_shape), v_s[:, -WINDOW:].reshape(cache_shape))
            front = ((0, 0), (WINDOW, 0), (0, 0))
            back = ((0, 0), (0, KEY_WIN - WINDOW - DEC_SEQ), (0, 0))
            kv_p = (_dup_heads(jnp.pad(k_p, front)), _dup_heads(jnp.pad(v_p, front)))
            kv_s = (_dup_heads(jnp.pad(k_s, back)), _dup_heads(jnp.pad(v_s, back)))
    y_prompt = x[0].reshape(BATCH, SEQ, D_MODEL)
    y_sample = x[1].reshape(DEC_BATCH, DEC_SEQ, D_MODEL)
    return (y_prompt, y_sample, jnp.stack(tails, axis=0), jnp.stack(us, axis=0)) + outs_kv
```

```python
import functools

import jax
import jax.numpy as jnp
from jax import lax
from jax.experimental import pallas as pl
from jax.experimental.pallas import tpu as pltpu

D_MODEL = 1024
BATCH = 2
SEQ = 8192
DEPTH = 4
DEC_BATCH = 32
DEC_SEQ = 32
PAST_LEN = 2048
CHUNK = 64
N_A_LAYERS = DEPTH // 2
CONV_WIDTH = 31
N_HEADS = 16
N_KV_HEADS = 2
HEAD_DIM = 64
KV_WIDTH = N_KV_HEADS * HEAD_DIM
WINDOW = 128
ROT_DIM = HEAD_DIM // 4
ROPE_THETA = 500000.0
N_GROUPS = 4
EXPERTS_PER_GROUP = 8
N_EXPERTS = N_GROUPS * EXPERTS_PER_GROUP
EXPERT_FF = D_MODEL // 2
DEEPNORM_ALPHA = (2 * DEPTH) ** 0.25
LN_EPS = 1e-5
ATTN_SCALE = HEAD_DIM ** -0.5
NEG_INF = -1e30

LANES = 128
N_LANE_CHUNKS = D_MODEL // LANES
T_PROMPT = BATCH * SEQ
T_SAMPLE = DEC_BATCH * DEC_SEQ
T_ALL = T_PROMPT + T_SAMPLE
TM = 256
STATE_PAD = 32
KEY_WIN = 256
UQ_PROMPT = 2 * CHUNK
MOE_TM = 512
MOE_ROWS = 2 * T_ALL + N_EXPERTS * MOE_TM
MOE_TILES = MOE_ROWS // MOE_TM
VMEM_LIMIT = 56 * 1024 * 1024

F32 = jnp.float32
BF16 = jnp.bfloat16


def _layer_norm(x, g, b):
    mu = jnp.mean(x, axis=-1, keepdims=True)
    xc = x - mu
    var = jnp.mean(xc * xc, axis=-1, keepdims=True)
    return xc * lax.rsqrt(var + LN_EPS) * g + b


def _router(x1, rwh_ref, rwl_ref, rb_ref, cnt_ref):
    x_hi = x1.astype(BF16)
    x_lo = (x1 - x_hi.astype(F32)).astype(BF16)
    w_hi = rwh_ref[...]
    logits = (jnp.dot(x_hi, w_hi, preferred_element_type=F32)
              + jnp.dot(x_hi, rwl_ref[...], preferred_element_type=F32)
              + jnp.dot(x_lo, w_hi, preferred_element_type=F32)) + rb_ref[...]
    lane_i = lax.broadcasted_iota(jnp.int32, logits.shape, 1)
    lane = lane_i.astype(F32)
    neg = jnp.float32(-3.0e38)
    none = jnp.float32(LANES)
    gl = jnp.where(lane_i < N_GROUPS, logits, neg)
    gmax = jnp.max(gl, axis=-1, keepdims=True)
    g_idx = jnp.min(jnp.where(gl == gmax, lane, none), axis=-1, keepdims=True)
    gsum = jnp.sum(jnp.where(lane_i < N_GROUPS, jnp.exp(gl - gmax), 0.0), axis=-1, keepdims=True)
    p_grp = 1.0 / gsum
    lo = N_GROUPS + EXPERTS_PER_GROUP * g_idx
    el = jnp.where(lane >= lo, jnp.where(lane < lo + EXPERTS_PER_GROUP, logits, neg), neg)
    l1 = jnp.max(el, axis=-1, keepdims=True)
    i1 = jnp.min(jnp.where(el == l1, lane, none), axis=-1, keepdims=True)
    el2 = jnp.where(lane == i1, neg, el)
    l2 = jnp.max(el2, axis=-1, keepdims=True)
    i2 = jnp.min(jnp.where(el2 == l2, lane, none), axis=-1, keepdims=True)
    t = jnp.exp(l2 - l1)
    inv = p_grp / (1.0 + t)
    e1 = i1 - N_GROUPS
    e2 = i2 - N_GROUPS
    rows = logits.shape[0]
    onehot = jnp.where(lane == e1, 1.0, jnp.where(lane == e2, 1.0, 0.0))
    tri = (lax.broadcasted_iota(jnp.int32, (rows, rows), 0) > lax.broadcasted_iota(jnp.int32, (rows, rows), 1))
    before = jnp.dot(jnp.where(tri, 1.0, 0.0).astype(BF16), onehot.astype(BF16),
                     preferred_element_type=F32) + cnt_ref[...]
    r1 = jnp.sum(jnp.where(lane == e1, before, 0.0), axis=-1, keepdims=True)
    r2 = jnp.sum(jnp.where(lane == e2, before, 0.0), axis=-1, keepdims=True)
    cnt_ref[...] = cnt_ref[...] + jnp.sum(onehot, axis=0, keepdims=True)
    out = jnp.where(lane_i == 4, r1, jnp.where(lane_i == 5, r2, 0.0))
    out = jnp.where(lane_i == 2, inv, jnp.where(lane_i == 3, inv * t, out))
    return jnp.where(lane_i == 0, e1, jnp.where(lane_i == 1, e2, out))


ER_ROWS = 8


def _post_mixer(x, mix, lg_ref, lb_ref, rwh_ref, rwl_ref, rb_ref, x1_ref, info_ref, er_ref, cnt_out_ref, cnt_ref):
    x1 = _layer_norm(DEEPNORM_ALPHA * x + mix, lg_ref[...], lb_ref[...])
    x1_ref[...] = x1
    info = _router(x1, rwh_ref, rwl_ref, rb_ref, cnt_ref)
    info_ref[...] = info
    er_ref[0] = jnp.transpose(info)[0:ER_ROWS, :].astype(jnp.int32)
    cnt_out_ref[...] = cnt_ref[...]


def _init_counts(cnt_ref, first_step, cnt_in_ref):
    @pl.when(first_step)
    def _():
        cnt_ref[...] = jnp.zeros_like(cnt_ref) if cnt_in_ref is None else cnt_in_ref[...]


def _conv_mixer_body(x_ref, st_ref, win_ref, wdw_ref, cg_ref, cb_ref, wout_ref, post_w, cnt_in_ref,
                     x1_ref, info_ref, er_ref, cnt_out_ref, u_ref, cnt_ref, ubuf, cbuf, *, nseg, seg):
    carry = st_ref is None
    first_step = (pl.program_id(0) == 0) & (pl.program_id(1) == 0) if carry else pl.program_id(0) == 0
    _init_counts(cnt_ref, first_step, cnt_in_ref)
    x = x_ref[...]
    h = jnp.dot(x.astype(BF16), win_ref[...], preferred_element_type=F32)
    u = h[:, :D_MODEL] * jax.nn.sigmoid(h[:, D_MODEL:])
    if carry:
        @pl.when(pl.program_id(1) == 0)
        def _():
            ubuf[:, 0, 0:STATE_PAD, :] = jnp.zeros((N_LANE_CHUNKS, STATE_PAD, LANES), F32)
        u_ref[0] = u[seg - STATE_PAD:, :]
    else:
        u_ref[...] = u
    for lc in range(N_LANE_CHUNKS):
        for s in range(nseg):
            ubuf[lc, s, STATE_PAD:STATE_PAD + seg, :] = u[s * seg:(s + 1) * seg, lc * LANES:(lc + 1) * LANES]
            if not carry:
                ubuf[lc, s, 0:STATE_PAD, :] = st_ref[s, :, lc * LANES:(lc + 1) * LANES]

    def conv_lane_chunk(lc, c):
        w = wdw_ref[lc]
        for s in range(nseg):
            acc = jnp.zeros((seg, LANES), F32)
            for k in range(CONV_WIDTH):
                off = k + STATE_PAD - (CONV_WIDTH - 1)
                acc = acc + w[k:k + 1, :] * ubuf[lc, s, off:off + seg, :]
            cbuf[lc, s * seg:(s + 1) * seg, :] = acc
        return c

    lax.fori_loop(0, N_LANE_CHUNKS, conv_lane_chunk, 0)
    if carry:
        for lc in range(N_LANE_CHUNKS):
            ubuf[lc, 0, 0:STATE_PAD, :] = u[seg - STATE_PAD:, lc * LANES:(lc + 1) * LANES]
    dw = jnp.concatenate([cbuf[lc] for lc in range(N_LANE_CHUNKS)], axis=1)
    sw = _layer_norm(dw, cg_ref[...], cb_ref[...])
    sw = sw * jax.nn.sigmoid(sw)
    mix = jnp.dot(sw.astype(BF16), wout_ref[...], preferred_element_type=F32)
    _post_mixer(x, mix, *post_w, x1_ref, info_ref, er_ref, cnt_out_ref, cnt_ref)


N_POST_W = 5


def _conv_prompt_kernel(x_ref, win_ref, wdw_ref, cg_ref, cb_ref, wout_ref, *rest):
    post_w, rest = rest[:N_POST_W], rest[N_POST_W:]
    _conv_mixer_body(x_ref, None, win_ref, wdw_ref, cg_ref, cb_ref, wout_ref, post_w, None, *rest,
                     nseg=1, seg=TM)


def _conv_sample_kernel(x_ref, st_ref, win_ref, wdw_ref, cg_ref, cb_ref, wout_ref, *rest):
    post_w, cnt_in_ref, rest = rest[:N_POST_W], rest[N_POST_W], rest[N_POST_W + 1 + N_ALIASED:]
    _conv_mixer_body(x_ref, st_ref, win_ref, wdw_ref, cg_ref, cb_ref, wout_ref, post_w, cnt_in_ref, *rest,
                     nseg=TM // DEC_SEQ, seg=DEC_SEQ)


def _full(shape):
    return pl.BlockSpec(shape, lambda *_: (0,) * len(shape))


ANY_SPEC = pl.BlockSpec(memory_space=pl.ANY)
CNT_SHAPE = jax.ShapeDtypeStruct((1, LANES), F32)
CNT_SCRATCH = pltpu.VMEM((1, LANES), F32)
PROMPT_TILES = T_PROMPT // TM
N_ALIASED = 3
TOKEN_OUT_SHAPES = [jax.ShapeDtypeStruct((T_ALL, D_MODEL), F32), jax.ShapeDtypeStruct((T_ALL, LANES), F32),
                    jax.ShapeDtypeStruct((T_ALL // TM, ER_ROWS, TM), jnp.int32)]


def _token_out_specs(tile):
    return [pl.BlockSpec((TM, D_MODEL), lambda *g: (tile(*g), 0)), pl.BlockSpec((TM, LANES), lambda *g: (tile(*g), 0)),
            pl.BlockSpec((1, ER_ROWS, TM), lambda *g: (tile(*g), 0, 0))]


def _conv_mixer(x_p, x_s, s_first, state_pad, w):
    w_specs = [_full(a.shape) for a in w]
    tiles_per_seq = SEQ // TM
    ptile = lambda b, t: b * tiles_per_seq + t
    x1, info, er, cnt, tail = pl.pallas_call(
        _conv_prompt_kernel,
        grid=(BATCH, tiles_per_seq),
        in_specs=[pl.BlockSpec((TM, D_MODEL), lambda b, t: (ptile(b, t), 0))] + w_specs,
        out_specs=_token_out_specs(ptile) + [_full(CNT_SHAPE.shape),
                                             pl.BlockSpec((1, STATE_PAD, D_MODEL), lambda b, t: (b, 0, 0))],
        out_shape=TOKEN_OUT_SHAPES + [CNT_SHAPE, jax.ShapeDtypeStruct((BATCH, STATE_PAD, D_MODEL), F32)],
        scratch_shapes=[CNT_SCRATCH, pltpu.VMEM((N_LANE_CHUNKS, 1, STATE_PAD + TM, LANES), F32),
                        pltpu.VMEM((N_LANE_CHUNKS, TM, LANES), F32)],
        compiler_params=pltpu.CompilerParams(dimension_semantics=("arbitrary", "arbitrary"),
                                             vmem_limit_bytes=VMEM_LIMIT),
        name="conv_mixer_prompt",
    )(x_p, *w)
    nb = TM // DEC_SEQ
    n_in = 2 + len(w) + 1
    x1, info, er, cnt, u_s = pl.pallas_call(
        _conv_sample_kernel,
        grid=(T_SAMPLE // TM,),
        in_specs=[pl.BlockSpec((TM, D_MODEL), lambda i: (s_first + i, 0)),
                  pl.BlockSpec((nb, STATE_PAD, D_MODEL), lambda i: (i, 0, 0))]
                 + w_specs + [_full(CNT_SHAPE.shape)] + [ANY_SPEC] * N_ALIASED,
        out_specs=_token_out_specs(lambda i: PROMPT_TILES + i)
                  + [_full(CNT_SHAPE.shape), pl.BlockSpec((TM, D_MODEL), lambda i: (i, 0))],
        out_shape=TOKEN_OUT_SHAPES + [CNT_SHAPE, jax.ShapeDtypeStruct((T_SAMPLE, D_MODEL), F32)],
        scratch_shapes=[CNT_SCRATCH, pltpu.VMEM((N_LANE_CHUNKS, nb, STATE_PAD + DEC_SEQ, LANES), F32),
                        pltpu.VMEM((N_LANE_CHUNKS, TM, LANES), F32)],
        input_output_aliases={n_in + k: k for k in range(N_ALIASED)},
        compiler_params=pltpu.CompilerParams(dimension_semantics=("arbitrary",),
                                             vmem_limit_bytes=VMEM_LIMIT),
        name="conv_mixer_sample",
    )(x_s, state_pad, *w, cnt, x1, info, er)
    return x1, info, er, cnt, tail, u_s


def _rope_lanes(v, c, sa, sb):
    width = v.shape[-1]
    half = ROT_DIM // 2
    return v * c + pltpu.roll(v, width - half, 1) * sa + pltpu.roll(v, half, 1) * sb


def _attn_mixer_body(sink_ref, x_ref, wq_ref, c_ref, sa_ref, sb_ref, kd_ref, vd_ref, wo_ref, post_w, cnt_in_ref,
                     x1_ref, info_ref, er_ref, cnt_out_ref, cnt_ref, obuf, bias_ref, *, nunit, uq, prompt):
    first_step = (pl.program_id(0) == 0) & (pl.program_id(1) == 0) if prompt else pl.program_id(0) == 0
    _init_counts(cnt_ref, first_step, cnt_in_ref)

    @pl.when(first_step)
    def _():
        shape = bias_ref.shape[1:]
        col = lax.broadcasted_iota(jnp.int32, shape, 1) & (KEY_WIN - 1)
        if prompt:
            chunk_shift = CHUNK.bit_length() - 1
            qchunk = (lax.broadcasted_iota(jnp.int32, shape, 0) & (uq - 1)) >> chunk_shift
            kchunk = col >> chunk_shift
            band = jnp.where(kchunk >= qchunk, jnp.where(kchunk <= qchunk + WINDOW // CHUNK, 0.0, NEG_INF), NEG_INF)
            bias_ref[0] = band
            bias_ref[1] = jnp.where(col >= WINDOW, band, NEG_INF)
        else:
            bias_ref[0] = jnp.where(col < WINDOW + DEC_SEQ, 0.0, NEG_INF)

    x = x_ref[...]
    q = jnp.dot(x.astype(BF16), wq_ref[...], preferred_element_type=F32)
    rep = D_MODEL // LANES
    q = _rope_lanes(q, jnp.tile(c_ref[...], (1, rep)), jnp.tile(sa_ref[...], (1, rep)),
                    jnp.tile(sb_ref[...], (1, rep)))
    qb = (q * ATTN_SCALE).astype(BF16)
    pairs = N_HEADS // N_KV_HEADS // 2
    lane_k = lax.broadcasted_iota(jnp.int32, (KEY_WIN, LANES), 1)
    lane_o = lax.broadcasted_iota(jnp.int32, (uq, LANES), 1)
    for un in range(nunit):
        r0 = un * uq
        if prompt:
            kstart = pl.multiple_of(pl.program_id(1) * (nunit * uq) + r0, LANES)
            bias = bias_ref[jnp.where(kstart == 0, 1, 0)]
        else:
            bias = bias_ref[0]
        for kvh in range(N_KV_HEADS):
            if prompt:
                kw = kd_ref[0, kvh, pl.ds(kstart, KEY_WIN), :]
                vw = vd_ref[0, kvh, pl.ds(kstart, KEY_WIN), :]
            else:
                kw = kd_ref[un, kvh]
                vw = vd_ref[un, kvh]
            zero = jnp.zeros_like(kw)
            kbd = jnp.concatenate([jnp.where(lane_k < HEAD_DIM, kw, zero),
                                   jnp.where(lane_k >= HEAD_DIM, kw, zero)], axis=0)
            vbd = jnp.concatenate([jnp.where(lane_k < HEAD_DIM, vw, zero),
                                   jnp.where(lane_k >= HEAD_DIM, vw, zero)], axis=0)
            q4 = jnp.concatenate([qb[r0:r0 + uq, (kvh * pairs + p) * LANES:(kvh * pairs + p + 1) * LANES]
                                  for p in range(pairs)], axis=0)
            s = lax.dot_general(q4, kbd, (((1,), (1,)), ((), ())), preferred_element_type=F32)
            s = s + bias
            p_parts, inv_parts = [], []
            for p in range(pairs):
                halves, invs = [], []
                for hf in range(2):
                    sk = sink_ref[(kvh * pairs + p) * 2 + hf]
                    sh = s[p * uq:(p + 1) * uq, hf * KEY_WIN:(hf + 1) * KEY_WIN]
                    m = jnp.maximum(jnp.max(sh, axis=-1, keepdims=True), sk)
                    pe = jnp.exp(sh - m)
                    den = jnp.sum(pe, axis=-1, keepdims=True) + jnp.exp(sk - m)
                    halves.append(pe.astype(BF16))
                    invs.append(1.0 / den)
                p_parts.append(jnp.concatenate(halves, axis=1))
                inv_parts.append(invs)
            pm = jnp.concatenate(p_parts, axis=0)
            o4 = jnp.dot(pm, vbd, preferred_element_type=F32)
            for p in range(pairs):
                inv = jnp.where(lane_o < HEAD_DIM, inv_parts[p][0], inv_parts[p][1])
                hp = kvh * pairs + p
                obuf[r0:r0 + uq, hp * LANES:(hp + 1) * LANES] = (o4[p * uq:(p + 1) * uq, :] * inv).astype(BF16)
    mix = jnp.dot(obuf[...], wo_ref[...], preferred_element_type=F32)
    _post_mixer(x, mix, *post_w, x1_ref, info_ref, er_ref, cnt_out_ref, cnt_ref)


N_ATTN_IN = 9


def _attn_prompt_kernel(*refs):
    ins, post_w, rest = refs[:N_ATTN_IN], refs[N_ATTN_IN:N_ATTN_IN + N_POST_W], refs[N_ATTN_IN + N_POST_W:]
    _attn_mixer_body(*ins, post_w, None, *rest, nunit=TM // UQ_PROMPT, uq=UQ_PROMPT, prompt=True)


def _attn_sample_kernel(*refs):
    ins, post_w = refs[:N_ATTN_IN], refs[N_ATTN_IN:N_ATTN_IN + N_POST_W]
    cnt_in_ref, rest = refs[N_ATTN_IN + N_POST_W], refs[N_ATTN_IN + N_POST_W + 1 + N_ALIASED:]
    _attn_mixer_body(*ins, post_w, cnt_in_ref, *rest, nunit=TM // DEC_SEQ, uq=DEC_SEQ, prompt=False)


def _attn_mixer(x, sinks, wq, wo, post_w, rope_p, rope_s, kv_p, kv_s):
    tiles_per_seq = SEQ // TM
    smem = pl.BlockSpec(memory_space=pltpu.SMEM)
    tail_specs = [_full(wo.shape)] + [_full(a.shape) for a in post_w]
    kd, vd = kv_p
    ptile = lambda b, t: b * tiles_per_seq + t
    pair_rows = N_HEADS // N_KV_HEADS // 2
    x1, info, er, cnt = pl.pallas_call(
        _attn_prompt_kernel,
        grid=(BATCH, tiles_per_seq),
        in_specs=[smem, pl.BlockSpec((TM, D_MODEL), lambda b, t: (ptile(b, t), 0)), _full(wq.shape)]
                 + [pl.BlockSpec((TM, LANES), lambda b, t: (t, 0))] * 3
                 + [pl.BlockSpec((1,) + kd.shape[1:], lambda b, t: (b, 0, 0, 0))] * 2 + tail_specs,
        out_specs=_token_out_specs(ptile) + [_full(CNT_SHAPE.shape)],
        out_shape=TOKEN_OUT_SHAPES + [CNT_SHAPE],
        scratch_shapes=[CNT_SCRATCH, pltpu.VMEM((TM, D_MODEL), BF16),
                        pltpu.VMEM((2, pair_rows * UQ_PROMPT, 2 * KEY_WIN), F32)],
        compiler_params=pltpu.CompilerParams(dimension_semantics=("arbitrary", "arbitrary"),
                                             vmem_limit_bytes=VMEM_LIMIT),
        name="attn_mixer_prompt",
    )(sinks, x, wq, *rope_p, kd, vd, wo, *post_w)
    nb = TM // DEC_SEQ
    kd, vd = kv_s
    n_in = N_ATTN_IN + N_POST_W + 1
    x1, info, er, cnt = pl.pallas_call(
        _attn_sample_kernel,
        grid=(T_SAMPLE // TM,),
        in_specs=[smem, pl.BlockSpec((TM, D_MODEL), lambda i: (PROMPT_TILES + i, 0)), _full(wq.shape)]
                 + [_full((TM, LANES))] * 3
                 + [pl.BlockSpec((nb,) + kd.shape[1:], lambda i: (i, 0, 0, 0))] * 2 + tail_specs
                 + [_full(CNT_SHAPE.shape)] + [ANY_SPEC] * N_ALIASED,
        out_specs=_token_out_specs(lambda i: PROMPT_TILES + i) + [_full(CNT_SHAPE.shape)],
        out_shape=TOKEN_OUT_SHAPES + [CNT_SHAPE],
        scratch_shapes=[CNT_SCRATCH, pltpu.VMEM((TM, D_MODEL), BF16),
                        pltpu.VMEM((1, pair_rows * DEC_SEQ, 2 * KEY_WIN), F32)],
        input_output_aliases={n_in + k: k for k in range(N_ALIASED)},
        compiler_params=pltpu.CompilerParams(dimension_semantics=("arbitrary",),
                                             vmem_limit_bytes=VMEM_LIMIT),
        name="attn_mixer_sample",
    )(sinks, x, wq, *rope_s, kd, vd, wo, *post_w, cnt, x1, info, er)
    return x1, info, er, cnt


N_TILES = T_ALL // TM


def _row_copy_wait(src_rows, dst_rows, sem):
    pltpu.make_async_copy(src_rows, dst_rows, sem).wait()


PAIR = 4
N_PAIRS = N_TILES // PAIR
assert N_TILES % PAIR == 0 and PROMPT_TILES % PAIR == 0
GATHER_LEAD = 3
assert 0 < GATHER_LEAD < PAIR
POS_BLOCK = (PAIR, 2, TM)


def _dispatch_kernel(zrow_ref, zflag_ref, pos_ref, x1_ref, xs_hbm, zbuf, sem, zsem):
    i = pl.program_id(0)

    @pl.when(i == 0)
    def _():
        zbuf[...] = jnp.zeros_like(zbuf)

        def zero_tile(e):
            return pltpu.make_async_copy(zbuf, xs_hbm.at[pl.ds(pl.multiple_of(zrow_ref[e], MOE_TM), MOE_TM)], zsem)

        for e in range(N_EXPERTS):
            @pl.when(zflag_ref[e] > 0)
            def _():
                zero_tile(e).start()
        for e in range(N_EXPERTS):
            @pl.when(zflag_ref[e] > 0)
            def _():
                zero_tile(e).wait()

    for h in range(PAIR):
        for j in range(TM):
            for slot in range(2):
                pltpu.make_async_copy(x1_ref.at[pl.ds(h * TM + j, 1)],
                                      xs_hbm.at[pl.ds(pos_ref[h, slot, j], 1)], sem).start(priority=slot)
    for slot in range(2):
        _row_copy_wait(x1_ref, xs_hbm.at[pl.ds(0, PAIR * TM)], sem)


def _dispatch(x1, pos, zrow, zflag):
    return pl.pallas_call(
        _dispatch_kernel,
        grid_spec=pltpu.PrefetchScalarGridSpec(
            num_scalar_prefetch=2,
            grid=(N_PAIRS,),
            in_specs=[pl.BlockSpec(POS_BLOCK, lambda i, *_: (i, 0, 0), memory_space=pltpu.SMEM),
                      pl.BlockSpec((PAIR * TM, D_MODEL), lambda i, *_: (i, 0))],
            out_specs=ANY_SPEC,
            scratch_shapes=[pltpu.VMEM((MOE_TM, D_MODEL), F32), pltpu.SemaphoreType.DMA,
                            pltpu.SemaphoreType.DMA],
        ),
        out_shape=jax.ShapeDtypeStruct((MOE_ROWS, D_MODEL), F32),
        compiler_params=pltpu.CompilerParams(dimension_semantics=("arbitrary",), vmem_limit_bytes=VMEM_LIMIT),
        name="moe_dispatch",
    )(zrow, zflag, pos, x1)


def _moe_kernel(te_ref, slot_ref, next_ref, nv_ref, xs_ref, wg_hbm, wu_hbm, wd_hbm, ys_ref,
                wg_st, wu_st, wd_st, wgu_bf, wd_bf, sem, *, base):
    i = pl.program_id(0)

    def fetch(e, slot):
        return (pltpu.make_async_copy(wg_hbm.at[base + e], wg_st.at[slot], sem.at[slot, 0]),
                pltpu.make_async_copy(wu_hbm.at[base + e], wu_st.at[slot], sem.at[slot, 1]),
                pltpu.make_async_copy(wd_hbm.at[base + e], wd_st.at[slot], sem.at[slot, 2]))

    @pl.when(i == 0)
    def _():
        for c in fetch(te_ref[0], slot_ref[0]):
            c.start()

    @pl.when(i < nv_ref[0])
    def _():
        e = te_ref[i]
        slot = slot_ref[i]

        @pl.when((i == 0) | (e != te_ref[jnp.maximum(i - 1, 0)]))
        def _():
            for c in fetch(e, slot):
                c.wait()
            wgu_bf[:, :EXPERT_FF] = wg_st[slot].astype(BF16)
            wgu_bf[:, EXPERT_FF:] = wu_st[slot].astype(BF16)
            wd_bf[...] = wd_st[slot].astype(BF16)

            @pl.when(next_ref[i] >= 0)
            def _():
                for c in fetch(next_ref[i], 1 - slot):
                    c.start()

        hgu = jnp.dot(xs_ref[...].astype(BF16), wgu_bf[...], preferred_element_type=F32)
        hg = hgu[:, :EXPERT_FF]
        h = hg * jax.nn.sigmoid(hg) * hgu[:, EXPERT_FF:]
        ys_ref[...] = jnp.dot(h.astype(BF16), wd_bf[...], preferred_element_type=F32)


def _moe_experts(xs, tile_expert, tile_slot, tile_next, n_valid, wg, wu, wd, layer):
    row = lambda i, te, sl, nx, nv: (jnp.minimum(i, nv[0] - 1), 0)
    return pl.pallas_call(
        functools.partial(_moe_kernel, base=layer * N_EXPERTS),
        grid_spec=pltpu.PrefetchScalarGridSpec(
            num_scalar_prefetch=4,
            grid=(MOE_TILES,),
            in_specs=[pl.BlockSpec((MOE_TM, D_MODEL), row), ANY_SPEC, ANY_SPEC, ANY_SPEC],
            out_specs=pl.BlockSpec((MOE_TM, D_MODEL), row),
            scratch_shapes=[pltpu.VMEM((2, D_MODEL, EXPERT_FF), F32), pltpu.VMEM((2, D_MODEL, EXPERT_FF), F32),
                            pltpu.VMEM((2, EXPERT_FF, D_MODEL), F32),
                            pltpu.VMEM((D_MODEL, 2 * EXPERT_FF), BF16), pltpu.VMEM((EXPERT_FF, D_MODEL), BF16),
                            pltpu.SemaphoreType.DMA((2, 3))],
        ),
        out_shape=jax.ShapeDtypeStruct((MOE_ROWS, D_MODEL), F32),
        compiler_params=pltpu.CompilerParams(dimension_semantics=("arbitrary",), vmem_limit_bytes=VMEM_LIMIT),
        name="moe_experts",
    )(tile_expert, tile_slot, tile_next, n_valid, xs, wg, wu, wd)


def _ffn_ln_kernel(pos_ref, pos_next_ref, x1_ref, info_ref, g_ref, b_ref, *rest, split, with_kv):
    kv_in, rest = (rest[:4], rest[4:]) if with_kv else ((), rest)
    ys_hbm, out_refs, bufs, sem = rest[0], rest[1:-(PAIR + 1)], rest[-(PAIR + 1):-1], rest[-1]
    i = pl.program_id(0)

    def gather(p_ref, h, buf):
        for j in range(TM):
            for slot in range(2):
                pltpu.make_async_copy(ys_hbm.at[pl.ds(p_ref[h, slot, j], 1)],
                                      bufs[buf].at[slot, pl.ds(j, 1)], sem.at[buf]).start(priority=slot)

    def drain(buf):
        for slot in range(2):
            _row_copy_wait(ys_hbm.at[pl.ds(0, TM)], bufs[buf].at[slot], sem.at[buf])

    @pl.when(i == 0)
    def _():
        for h in range(GATHER_LEAD):
            gather(pos_ref, h, h)

    for h in range(PAIR):
        drain(h)
        ahead = h + GATHER_LEAD
        if ahead < PAIR:
            gather(pos_ref, ahead, ahead)
        else:
            gather(pos_next_ref, ahead - PAIR, ahead - PAIR)
        rows = slice(h * TM, (h + 1) * TM)
        info = info_ref[rows, :]
        f = info[:, 2:3] * bufs[h][0] + info[:, 3:4] * bufs[h][1]
        x2 = _layer_norm(DEEPNORM_ALPHA * x1_ref[rows, :] + f, g_ref[...], b_ref[...])
        if with_kv:
            wkv_ref, c_ref, sa_ref, sb_ref = kv_in
            k_ref, v_ref = out_refs[-2:]
            kv = jnp.dot(x2.astype(BF16), wkv_ref[...], preferred_element_type=F32)
            k_ref[rows, :] = _rope_lanes(kv[:, :KV_WIDTH], c_ref[rows, :], sa_ref[rows, :], sb_ref[rows, :])
            v_ref[rows, :] = kv[:, KV_WIDTH:]
        if split:
            @pl.when(i < PROMPT_TILES // PAIR)
            def _():
                out_refs[0][rows, :] = x2

            @pl.when(i >= PROMPT_TILES // PAIR)
            def _():
                out_refs[1][rows, :] = x2
        else:
            out_refs[0][rows, :] = x2

    @pl.when(i == N_PAIRS - 1)
    def _():
        for h in range(GATHER_LEAD):
            drain(h)


def _ffn_ln(x1, ys, pos, info, g, b, split, kv_w=None):
    tok = lambda i: (i, 0)
    rows = PAIR * TM
    prompt_steps = PROMPT_TILES // PAIR
    pos_spec = lambda fn: pl.BlockSpec(POS_BLOCK, fn, memory_space=pltpu.SMEM)
    if split:
        out_specs = [pl.BlockSpec((rows, D_MODEL), lambda i: (jnp.minimum(i, prompt_steps - 1), 0)),
                     pl.BlockSpec((rows, D_MODEL), lambda i: (jnp.maximum(i - prompt_steps, 0), 0))]
        out_shape = [jax.ShapeDtypeStruct((T_PROMPT, D_MODEL), F32), jax.ShapeDtypeStruct((T_SAMPLE, D_MODEL), F32)]
    else:
        out_specs = [pl.BlockSpec((rows, D_MODEL), tok)]
        out_shape = [jax.ShapeDtypeStruct((T_ALL, D_MODEL), F32)]
    kv_args, kv_specs = (), []
    if kv_w is not None:
        wkv, rope_all = kv_w
        kv_args = (wkv,) + tuple(rope_all)
        kv_specs = [_full(wkv.shape)] + [pl.BlockSpec((rows, LANES), tok)] * 3
        out_specs = out_specs + [pl.BlockSpec((rows, KV_WIDTH), tok)] * 2
        out_shape = out_shape + [jax.ShapeDtypeStruct((T_ALL, KV_WIDTH), F32)] * 2
    return pl.pallas_call(
        functools.partial(_ffn_ln_kernel, split=split, with_kv=kv_w is not None),
        grid=(N_PAIRS,),
        in_specs=[pos_spec(lambda i: (i, 0, 0)), pos_spec(lambda i: (jnp.minimum(i + 1, N_PAIRS - 1), 0, 0)),
                  pl.BlockSpec((rows, D_MODEL), tok), pl.BlockSpec((rows, LANES), tok), _full(g.shape),
                  _full(b.shape)] + kv_specs + [ANY_SPEC],
        out_specs=out_specs,
        out_shape=out_shape,
        scratch_shapes=[pltpu.VMEM((2, TM, D_MODEL), F32)] * PAIR + [pltpu.SemaphoreType.DMA((PAIR,))],
        compiler_params=pltpu.CompilerParams(dimension_semantics=("arbitrary",), vmem_limit_bytes=VMEM_LIMIT),
        name="ffn_ln",
    )(pos, pos, x1, info, g, b, *kv_args, ys)


def _moe_layer(x1, info, er, cnt, wg, wu, wd, layer, g, b, split, kv_w):
    counts = cnt[0, :N_EXPERTS].astype(jnp.int32)
    tiles_per = (counts + MOE_TM - 1) // MOE_TM
    tile_end = jnp.cumsum(tiles_per)
    pad_start = (tile_end - tiles_per) * MOE_TM
    ids = jnp.arange(N_EXPERTS, dtype=jnp.int32)
    half = ER_ROWS // 2
    hit = er[None, :, 0:2, :] == ids[:, None, None, None]
    pos = jnp.sum(jnp.where(hit, pad_start[:, None, None, None], 0), axis=0) + er[:, half:half + 2, :]
    n_valid = tile_end[-1:]
    tile_ids = jnp.minimum(jnp.arange(MOE_TILES, dtype=jnp.int32), n_valid[0] - 1)
    tile_expert = jnp.sum(tile_ids[:, None] >= tile_end[None, :], axis=1, dtype=jnp.int32)
    nonempty = tiles_per > 0
    slot_of = (jnp.cumsum(nonempty.astype(jnp.int32)) - 1) % 2
    later = jnp.where(nonempty[None, :] & (ids[None, :] > ids[:, None]), ids[None, :], N_EXPERTS)
    next_of = jnp.min(later, axis=1)
    next_of = jnp.where(next_of < N_EXPERTS, next_of, -1)
    zrow = jnp.maximum(tile_end - 1, 0) * MOE_TM
    mine = tile_expert[:, None] == ids[None, :]
    tile_slot = jnp.sum(jnp.where(mine, slot_of[None, :], 0), axis=1)
    tile_next = jnp.sum(jnp.where(mine, next_of[None, :], 0), axis=1)
    xs = _dispatch(x1, pos, zrow, tiles_per)
    ys = _moe_experts(xs, tile_expert, tile_slot, tile_next, n_valid, wg, wu, wd, layer)
    return _ffn_ln(x1, ys, pos, info, g, b, split, kv_w)


def _rope_tables(pos):
    half = ROT_DIM // 2
    inv_freq = ROPE_THETA ** (-jnp.arange(0, ROT_DIM, 2, dtype=F32) / ROT_DIM)
    ang = pos.astype(F32)[:, None] * inv_freq[None, :]
    cos, sin = jnp.cos(ang), jnp.sin(ang)
    n = pos.shape[0]
    ones = jnp.ones((n, HEAD_DIM - ROT_DIM), F32)
    zeros_h = jnp.zeros((n, half), F32)
    zeros_r = jnp.zeros((n, HEAD_DIM - ROT_DIM), F32)
    c = jnp.concatenate([cos, cos, ones], axis=1)
    sa = jnp.concatenate([-sin, zeros_h, zeros_r], axis=1)
    sb = jnp.concatenate([zeros_h, sin, zeros_r], axis=1)
    rep = LANES // HEAD_DIM
    return tuple(jnp.tile(t, (1, rep)) for t in (c, sa, sb))


def _dup_heads(kv):
    b, l, _ = kv.shape
    h = kv.reshape(b, l, N_KV_HEADS, HEAD_DIM).transpose(0, 2, 1, 3)
    return jnp.concatenate([h, h], axis=-1).astype(BF16)


def kernel(x_prompt, x_sample, state_conv, cache_k, cache_v, ln_mix_g, ln_mix_b, ln_ffn_g, ln_ffn_b, conv_w_in, conv_w_dw, conv_ln_g, conv_ln_b, conv_w_out, w_kv, attn_w_q, attn_sinks, attn_w_o, router_w_group, router_b_group, router_w_expert, router_b_expert, expert_w_gate, expert_w_up, expert_w_down):
    x = None
    wg = expert_w_gate.reshape(DEPTH * N_EXPERTS, D_MODEL, EXPERT_FF)
    wu = expert_w_up.reshape(DEPTH * N_EXPERTS, D_MODEL, EXPERT_FF)
    wd = expert_w_down.reshape(DEPTH * N_EXPERTS, EXPERT_FF, D_MODEL)
    row = lambda v: v.reshape(1, -1)

    def router_weights(layer):
        w = jnp.concatenate([router_w_group[layer], router_w_expert[layer]], axis=1)
        w = jnp.pad(w, ((0, 0), (0, LANES - w.shape[1])))
        w_hi = w.astype(BF16)
        w_lo = (w - w_hi.astype(F32)).astype(BF16)
        bias = jnp.concatenate([router_b_group[layer], router_b_expert[layer].reshape(-1)])
        return w_hi, w_lo, jnp.pad(bias, (0, LANES - bias.shape[0])).reshape(1, LANES)

    pos_p = jnp.arange(SEQ, dtype=jnp.int32)
    pos_s = PAST_LEN + jnp.arange(DEC_SEQ, dtype=jnp.int32)
    rope_p = _rope_tables(pos_p)
    rope_s1 = _rope_tables(pos_s)
    rope_s = tuple(jnp.tile(t, (TM // DEC_SEQ, 1)) for t in rope_s1)
    rope_all = tuple(jnp.concatenate([jnp.tile(p, (BATCH, 1)), jnp.tile(s, (DEC_BATCH, 1))], axis=0)
                     for p, s in zip(rope_p, rope_s1))

    tails, us = [], []
    kv_p = kv_s = None
    outs_kv = None
    for layer in range(DEPTH):
        post_w = (row(ln_mix_g[layer]), row(ln_mix_b[layer])) + router_weights(layer)
        if layer < N_A_LAYERS:
            st = jnp.pad(state_conv[layer], ((0, 0), (STATE_PAD - (CONV_WIDTH - 1), 0), (0, 0)))
            wdw = jnp.pad(conv_w_dw[layer], ((0, STATE_PAD - CONV_WIDTH), (0, 0)))
            wdw = wdw.reshape(STATE_PAD, N_LANE_CHUNKS, LANES).transpose(1, 0, 2)
            w = (conv_w_in[layer].astype(BF16), wdw, row(conv_ln_g[layer]), row(conv_ln_b[layer]),
                 conv_w_out[layer].astype(BF16)) + post_w
            if layer == 0:
                x_in = (x_prompt.reshape(T_PROMPT, D_MODEL), x_sample.reshape(T_SAMPLE, D_MODEL), 0)
            else:
                x_in = (x, x, PROMPT_TILES)
            x1, info, er, cnt, tail, u_s = _conv_mixer(*x_in, st, w)
            tails.append(tail[:, STATE_PAD - (CONV_WIDTH - 1):, :])
            us.append(u_s.reshape(DEC_BATCH, DEC_SEQ, D_MODEL)[:, DEC_SEQ - (CONV_WIDTH - 1):, :])
        else:
            j = layer - N_A_LAYERS
            x1, info, er, cnt = _attn_mixer(x, attn_sinks[j], attn_w_q[j].astype(BF16), attn_w_o[j].astype(BF16),
                                        post_w, rope_p, rope_s, kv_p, kv_s)
        last_a = layer == N_A_LAYERS - 1
        outs = _moe_layer(x1, info, er, cnt, wg, wu, wd, layer, row(ln_ffn_g[layer]), row(ln_ffn_b[layer]),
                          split=layer == DEPTH - 1, kv_w=(w_kv.astype(BF16), rope_all) if last_a else None)
        x = outs if layer == DEPTH - 1 else outs[0]
        if last_a:
            k, v = outs[1:]
            k_p = k[:T_PROMPT].reshape(BATCH, SEQ, KV_WIDTH)
            v_p = v[:T_PROMPT].reshape(BATCH, SEQ, KV_WIDTH)
            k_s = jnp.concatenate([cache_k.reshape(DEC_BATCH, WINDOW, KV_WIDTH),
                                   k[T_PROMPT:].reshape(DEC_BATCH, DEC_SEQ, KV_WIDTH)], axis=1)
            v_s = jnp.concatenate([cache_v.reshape(DEC_BATCH, WINDOW, KV_WIDTH),
                                   v[T_PROMPT:].reshape(DEC_BATCH, DEC_SEQ, KV_WIDTH)], axis=1)
            cache_shape = (-1, WINDOW, N_KV_HEADS, HEAD_DIM)
            outs_kv = (k_p[:, -WINDOW:].reshape(cache_shape), v_p[:, -WINDOW:].reshape(cache_shape),
                       k_s[:, -WINDOW:].reshape(cache_shape), v_s[:, -WINDOW:].reshape(cache_shape))
            front = ((0, 0), (WINDOW, 0), (0, 0))
            back = ((0, 0), (0, KEY_WIN - WINDOW - DEC_SEQ), (0, 0))
            kv_p = (_dup_heads(jnp.pad(k_p, front)), _dup_heads(jnp.pad(v_p, front)))
            kv_s = (_dup_heads(jnp.pad(k_s, back)), _dup_heads(jnp.pad(v_s, back)))
    y_prompt = x[0].reshape(BATCH, SEQ, D_MODEL)
    y_sample = x[1].reshape(DEC_BATCH, DEC_SEQ, D_MODEL)
    return (y_prompt, y_sample, jnp.stack(tails, axis=0), jnp.stack(us, axis=0)) + outs_kv
```
